```python
import jax, jax.numpy as jnp
from jax import lax
import numpy as np

D_MODEL = 1024
BATCH = 8
SEQ = 4096
DEPTH = 4

HEAD_DIM = 64
MEM_DIM = D_MODEL // 4
MEM_HEADS = MEM_DIM // HEAD_DIM
MEM_LEN = 256
FOX_DIM = D_MODEL - MEM_DIM
FOX_HEADS = FOX_DIM // HEAD_DIM
CONV_DIM = D_MODEL - MEM_DIM
CONV_WIDTH = 31
Q_BLOCK = 128
D_FF = ((8 * D_MODEL // 3 + 127) // 128) * 128
N_EXPERTS = 8
TOP_K = 2
EXPERT_FF = 7 * D_MODEL // 2
EXPERT_BLOCK = 128
N_FOX = (DEPTH + 1) // 2
N_CONV = DEPTH // 2
EPS = 1e-6
NEG_INF = -1e30

kernel_name = "fox_conformer_memory_moe_hybrid"


def rms_norm(x, g):
    xf = x.astype(jnp.float32)
    y = xf * lax.rsqrt(jnp.mean(xf * xf, axis=-1, keepdims=True) + EPS)
    return (y * g.astype(jnp.float32)).astype(x.dtype)


def layer_norm(x, g, b):
    xf = x.astype(jnp.float32)
    mu = jnp.mean(xf, axis=-1, keepdims=True)
    var = jnp.mean(jnp.square(xf - mu), axis=-1, keepdims=True)
    y = (xf - mu) * lax.rsqrt(var + EPS)
    return (y * g.astype(jnp.float32) + b.astype(jnp.float32)).astype(x.dtype)


def forget_attention(q, k, v, log_f):
    B, S, H, hd = q.shape
    nb = S // Q_BLOCK
    c = jnp.cumsum(log_f, axis=1).transpose(0, 2, 1)
    q_blocks = q.reshape(B, nb, Q_BLOCK, H, hd).transpose(1, 0, 2, 3, 4)
    c_blocks = c.reshape(B, H, nb, Q_BLOCK).transpose(2, 0, 1, 3)
    k_pos = jnp.arange(S)
    scale = hd ** -0.5

    def one_block(args):
        q_b, c_b, blk = args
        s = jnp.einsum('bqhd,bkhd->bhqk', q_b, k, preferred_element_type=jnp.float32) * scale
        s = s + (c_b[..., :, None] - c[:, :, None, :])
        q_pos = blk * Q_BLOCK + jnp.arange(Q_BLOCK)
        s = jnp.where(k_pos[None, :] <= q_pos[:, None], s, NEG_INF)
        p = jax.nn.softmax(s, axis=-1).astype(v.dtype)
        return jnp.einsum('bhqk,bkhd->bqhd', p, v)

    o = lax.map(one_block, (q_blocks, c_blocks, jnp.arange(nb)))
    return o.transpose(1, 0, 2, 3, 4).reshape(B, S, H * hd)


def memory_attention(xq, mk, mv, g_q, g_k):
    B, S, _ = xq.shape
    M = mk.shape[1]
    q = rms_norm(xq.reshape(B, S, MEM_HEADS, HEAD_DIM), g_q)
    k = rms_norm(mk.reshape(B, M, MEM_HEADS, HEAD_DIM), g_k)
    v = mv.reshape(B, M, MEM_HEADS, HEAD_DIM)
    s = jnp.einsum('bshd,bmhd->bhsm', q, k, preferred_element_type=jnp.float32) * (HEAD_DIM ** -0.5)
    p = jax.nn.softmax(s, axis=-1).astype(v.dtype)
    return jnp.einsum('bhsm,bmhd->bshd', p, v).reshape(B, S, MEM_DIM)


def conformer_conv(glu_in, dw, dw_b, ln_g, ln_b):
    a, gate = jnp.split(glu_in, 2, axis=-1)
    u = a * jax.nn.sigmoid(gate)
    u = lax.conv_general_dilated(
        u, dw[:, None, :].astype(u.dtype), window_strides=(1,),
        padding=[(CONV_WIDTH - 1, 0)], dimension_numbers=('NWC', 'WIO', 'NWC'),
        feature_group_count=CONV_DIM) + dw_b.astype(u.dtype)
    return jax.nn.silu(layer_norm(u, ln_g, ln_b))


def swiglu(h, w_gate, w_up, w_down):
    return (jax.nn.silu(h @ w_gate) * (h @ w_up)) @ w_down


def moe_swiglu(h, w_router, w_gate, w_up, w_down):
    B, S, D = h.shape
    xt = h.reshape(-1, D)
    T = xt.shape[0]
    A = T * TOP_K
    logits = (xt @ w_router).astype(jnp.float32)
    top_logit, top_idx = lax.top_k(logits, TOP_K)
    gates = jax.nn.softmax(top_logit, axis=-1)
    flat_e = top_idx.reshape(-1)
    order = jnp.argsort(flat_e)
    sorted_e = flat_e[order]
    sizes = jnp.bincount(flat_e, length=N_EXPERTS).astype(jnp.int32)
    padded = ((sizes + EXPERT_BLOCK - 1) // EXPERT_BLOCK) * EXPERT_BLOCK
    pad_end = jnp.cumsum(padded)
    pad_start = pad_end - padded
    start = jnp.cumsum(sizes) - sizes
    dest = pad_start[sorted_e] + (jnp.arange(A, dtype=jnp.int32) - start[sorted_e])
    n_blocks = -(-A // EXPERT_BLOCK) + N_EXPERTS
    rows = n_blocks * EXPERT_BLOCK
    tok_buf = jnp.full((rows,), T, jnp.int32).at[dest].set((order // TOP_K).astype(jnp.int32))
    gate_buf = jnp.zeros((rows,), jnp.float32).at[dest].set(gates.reshape(-1)[order])
    block_start = jnp.arange(n_blocks, dtype=jnp.int32) * EXPERT_BLOCK
    block_expert = jnp.minimum(jnp.searchsorted(pad_end, block_start, side='right'), N_EXPERTS - 1)
    xt_pad = jnp.concatenate([xt, jnp.zeros((1, D), xt.dtype)], axis=0)
    xs = xt_pad[tok_buf].reshape(n_blocks, EXPERT_BLOCK, D)

    def expert_block(args):
        xb, e = args
        return swiglu(xb, w_gate[e], w_up[e], w_down[e])

    ys = lax.map(expert_block, (xs, block_expert)).reshape(rows, D)
    ys = ys * gate_buf[:, None].astype(ys.dtype)
    out = jnp.zeros((T + 1, D), ys.dtype).at[tok_buf].add(ys)[:T]
    return out.reshape(B, S, D)


def setup_inputs(seed: int = 0) -> dict:
    key = jax.random.key(seed)
    ks = iter(jax.random.split(key, 40))

    def nrm(shape, scale):
        return jax.random.normal(next(ks), shape, jnp.float32) * scale

    D = D_MODEL
    fox_in = 3 * FOX_DIM + FOX_HEADS + MEM_DIM
    conv_in = 2 * CONV_DIM + MEM_DIM
    return {
        "x": nrm((BATCH, SEQ, D), 1.0),
        "mem": nrm((BATCH, MEM_LEN, D), 1.0),
        "norm_mix": 1.0 + nrm((DEPTH, D), 0.02),
        "norm_mem": 1.0 + nrm((DEPTH, D), 0.02),
        "norm_ffn": 1.0 + nrm((DEPTH, D), 0.02),
        "w_mem_kv": nrm((DEPTH, D, 2 * MEM_DIM), D ** -0.5),
        "g_mq": 1.0 + nrm((DEPTH, HEAD_DIM), 0.02),
        "g_mk": 1.0 + nrm((DEPTH, HEAD_DIM), 0.02),
        "fox_w_in": nrm((N_FOX, D, fox_in), D ** -0.5),
        "fox_b_f": 3.0 + nrm((N_FOX, FOX_HEADS), 0.5),
        "fox_g_q": 1.0 + nrm((N_FOX, HEAD_DIM), 0.02),
        "fox_g_k": 1.0 + nrm((N_FOX, HEAD_DIM), 0.02),
        "fox_w_out": nrm((N_FOX, FOX_DIM + MEM_DIM, D), 0.5 * D ** -0.5),
        "conv_w_in": nrm((N_CONV, D, conv_in), D ** -0.5),
        "conv_b_in": nrm((N_CONV, 2 * CONV_DIM), 0.02),
        "conv_dw": nrm((N_CONV, CONV_WIDTH, CONV_DIM), CONV_WIDTH ** -0.5),
        "conv_dw_b": nrm((N_CONV, CONV_DIM), 0.02),
        "conv_ln_g": 1.0 + nrm((N_CONV, CONV_DIM), 0.02),
        "conv_ln_b": nrm((N_CONV, CONV_DIM), 0.02),
        "conv_w_out": nrm((N_CONV, CONV_DIM + MEM_DIM, D), 0.5 * D ** -0.5),
        "ffn_w_gate": nrm((N_FOX, D, D_FF), D ** -0.5),
        "ffn_w_up": nrm((N_FOX, D, D_FF), D ** -0.5),
        "ffn_w_down": nrm((N_FOX, D_FF, D), 0.5 * D_FF ** -0.5),
        "moe_router": nrm((N_CONV, D, N_EXPERTS), D ** -0.5),
        "moe_w_gate": nrm((N_CONV, N_EXPERTS, D, EXPERT_FF), D ** -0.5),
        "moe_w_up": nrm((N_CONV, N_EXPERTS, D, EXPERT_FF), D ** -0.5),
        "moe_w_down": nrm((N_CONV, N_EXPERTS, EXPERT_FF, D), 0.5 * EXPERT_FF ** -0.5),
    }


def reference(x, mem, norm_mix, norm_mem, norm_ffn, w_mem_kv, g_mq, g_mk,
              fox_w_in, fox_b_f, fox_g_q, fox_g_k, fox_w_out,
              conv_w_in, conv_b_in, conv_dw, conv_dw_b, conv_ln_g, conv_ln_b, conv_w_out,
              ffn_w_gate, ffn_w_up, ffn_w_down,
              moe_router, moe_w_gate, moe_w_up, moe_w_down):
    B, S, D = x.shape
    h = x
    for i in range(DEPTH):
        j = i // 2
        u = rms_norm(h, norm_mix[i])
        m = rms_norm(mem, norm_mem[i])
        mk, mv = jnp.split(m @ w_mem_kv[i], 2, axis=-1)
        if i % 2 == 0:
            proj = u @ fox_w_in[j]
            q, k, v, f_logit, mq = jnp.split(
                proj, [FOX_DIM, 2 * FOX_DIM, 3 * FOX_DIM, 3 * FOX_DIM + FOX_HEADS], axis=-1)
            q = rms_norm(q.reshape(B, S, FOX_HEADS, HEAD_DIM), fox_g_q[j])
            k = rms_norm(k.reshape(B, S, FOX_HEADS, HEAD_DIM), fox_g_k[j])
            v = v.reshape(B, S, FOX_HEADS, HEAD_DIM)
            log_f = jax.nn.log_sigmoid((f_logit + fox_b_f[j]).astype(jnp.float32))
            mix = forget_attention(q, k, v, log_f)
            w_out = fox_w_out[j]
        else:
            proj = u @ conv_w_in[j]
            glu_in, mq = jnp.split(proj, [2 * CONV_DIM], axis=-1)
            mix = conformer_conv(glu_in + conv_b_in[j], conv_dw[j], conv_dw_b[j],
                                 conv_ln_g[j], conv_ln_b[j])
            w_out = conv_w_out[j]
        mem_out = memory_attention(mq, mk, mv, g_mq[i], g_mk[i])
        h = h + jnp.concatenate([mix, mem_out], axis=-1) @ w_out
        z = rms_norm(h, norm_ffn[i])
        if i % 2 == 0:
            h = h + swiglu(z, ffn_w_gate[j], ffn_w_up[j], ffn_w_down[j])
        else:
            h = h + moe_swiglu(z, moe_router[j], moe_w_gate[j], moe_w_up[j], moe_w_down[j])
    return h
```

```python
import functools

import jax
import jax.numpy as jnp
from jax import lax
from jax.experimental import pallas as pl
from jax.experimental.pallas import tpu as pltpu

F32 = jnp.float32
BF16 = jnp.bfloat16

D_MODEL = 1024
HEAD_DIM = 64
MEM_DIM = 256
MIX_DIM = D_MODEL - MEM_DIM
FOX_HEADS = MIX_DIM // HEAD_DIM
CONV_WIDTH = 31
N_EXPERTS = 8
TOP_K = 2
DEPTH = 4
EPS = 1e-6
NEG_INF = -1e30

LANES = 128
SUBLANES = 8
VMEM_LIMIT_BYTES = 56 * 1024 * 1024

ROW_TILE = 512
ATTN_Q_TILE = 256
ATTN_KV_TILE = 256
CONV_TILE = 256
CONV_HALO = 32
CONV_ROW_CHUNK = 64
FFN_CHUNK = 256
EXPERT_ROWS = 1024
EXPERT_SUB = 256
EXPERT_FF_TILE = 512
F_LANES = LANES


def _params(*sem):
    return pltpu.CompilerParams(dimension_semantics=sem, vmem_limit_bytes=VMEM_LIMIT_BYTES)


def _resident(shape):
    nd = len(shape)
    return pl.BlockSpec(shape, lambda *_: (0,) * nd, pipeline_mode=pl.Buffered(1))


def _rms(x, g):
    ms = jnp.mean(x * x, axis=-1, keepdims=True)
    return x * lax.rsqrt(ms + EPS) * g


def _head_rms(x, ones_bd, g):
    ss = jnp.dot((x * x).astype(BF16), ones_bd, preferred_element_type=F32)
    return x * lax.rsqrt(ss * (1.0 / HEAD_DIM) + EPS) * g


def _silu(x):
    return x * (1.0 / (1.0 + jnp.exp(-x)))


def _log_sigmoid(x):
    return jnp.minimum(x, 0.0) - jnp.log(1.0 + jnp.exp(-jnp.abs(x)))


def _row_prefix_sum(x):
    n = x.shape[0]
    row = lax.broadcasted_iota(jnp.int32, x.shape, 0)
    s = 1
    while s < n:
        x = x + jnp.where(row >= s, pltpu.roll(x, s, 0), 0.0)
        s *= 2
    return x


def _fox_inproj_kernel(tiles_per_seq, h_ref, gn_ref, w_ref, bd_ref, gq_ref, gk_ref, gmq_ref,
                       bf_ref, q_ref, k_ref, v_ref, mq_ref, c_ref, carry_ref):
    xb = _rms(h_ref[...], gn_ref[...]).astype(BF16)
    m = MIX_DIM
    q = jnp.dot(xb, w_ref[:, 0:m], preferred_element_type=F32)
    q_ref[...] = _head_rms(q, bd_ref[...], gq_ref[...]).astype(BF16)
    k = jnp.dot(xb, w_ref[:, m:2 * m], preferred_element_type=F32)
    k_ref[...] = _head_rms(k, bd_ref[...], gk_ref[...]).astype(BF16)
    v_ref[...] = jnp.dot(xb, w_ref[:, 2 * m:3 * m], preferred_element_type=F32).astype(BF16)
    mq = jnp.dot(xb, w_ref[:, 3 * m:3 * m + MEM_DIM], preferred_element_type=F32)
    mq_ref[...] = _head_rms(mq, bd_ref[0:MEM_DIM, 0:MEM_DIM], gmq_ref[...]).astype(BF16)

    f = jnp.dot(xb, w_ref[:, 3 * m + MEM_DIM:], preferred_element_type=F32) + bf_ref[...]
    lane = lax.broadcasted_iota(jnp.int32, f.shape, 1)
    lf = jnp.where(lane < FOX_HEADS, _log_sigmoid(f), 0.0)

    @pl.when(pl.program_id(0) % tiles_per_seq == 0)
    def _():
        carry_ref[...] = jnp.zeros_like(carry_ref)

    c = _row_prefix_sum(lf) + carry_ref[0:1, :]
    c_ref[...] = c
    carry_ref[0:1, :] = c[-1:, :]


def _fox_inproj(h, seq_len, gn, w, bd, gq, gk, gmq, bf):
    t = h.shape[0]
    tm = min(ROW_TILE, seq_len)
    assert seq_len % tm == 0
    row = lambda n: pl.BlockSpec((tm, n), lambda i: (i, 0))
    return pl.pallas_call(
        functools.partial(_fox_inproj_kernel, seq_len // tm),
        grid=(t // tm,),
        in_specs=[row(D_MODEL), _resident(gn.shape), _resident(w.shape), _resident(bd.shape),
                  _resident(gq.shape), _resident(gk.shape), _resident(gmq.shape),
                  _resident(bf.shape)],
        out_specs=[row(MIX_DIM), row(MIX_DIM), row(MIX_DIM), row(MEM_DIM), row(F_LANES)],
        out_shape=[jax.ShapeDtypeStruct((t, MIX_DIM), BF16)] * 3
        + [jax.ShapeDtypeStruct((t, MEM_DIM), BF16), jax.ShapeDtypeStruct((t, F_LANES), F32)],
        scratch_shapes=[pltpu.VMEM((SUBLANES, F_LANES), F32)],
        compiler_params=_params("arbitrary"),
        name="fox_inproj",
    )(h, gn, w, bd, gq, gk, gmq, bf)


def _conv_inproj_kernel(h_ref, gn_ref, w_ref, b_ref, bd_ref, gmq_ref, u_ref, mq_ref):
    xb = _rms(h_ref[...], gn_ref[...]).astype(BF16)
    m = MIX_DIM
    a = jnp.dot(xb, w_ref[:, 0:m], preferred_element_type=F32) + b_ref[:, 0:m]
    g = jnp.dot(xb, w_ref[:, m:2 * m], preferred_element_type=F32) + b_ref[:, m:2 * m]
    u_ref[...] = a * (1.0 / (1.0 + jnp.exp(-g)))
    mq = jnp.dot(xb, w_ref[:, 2 * m:], preferred_element_type=F32)
    mq_ref[...] = _head_rms(mq, bd_ref[...], gmq_ref[...]).astype(BF16)


def _conv_inproj(h, seq_len, gn, w, b, bd, gmq):
    t = h.shape[0]
    tm = min(ROW_TILE, seq_len)
    assert t % tm == 0
    row = lambda n: pl.BlockSpec((tm, n), lambda i: (i, 0))
    return pl.pallas_call(
        _conv_inproj_kernel,
        grid=(t // tm,),
        in_specs=[row(D_MODEL), _resident(gn.shape), _resident(w.shape), _resident(b.shape),
                  _resident(bd.shape), _resident(gmq.shape)],
        out_specs=[row(MIX_DIM), row(MEM_DIM)],
        out_shape=[jax.ShapeDtypeStruct((t, MIX_DIM), F32),
                   jax.ShapeDtypeStruct((t, MEM_DIM), BF16)],
        compiler_params=_params("parallel"),
        name="conv_inproj",
    )(h, gn, w, b, bd, gmq)


def _mem_kv_kernel(m_ref, gn_ref, w_ref, bd_ref, gk_ref, mk_ref, mv_ref):
    xb = _rms(m_ref[...], gn_ref[...]).astype(BF16)
    mk = jnp.dot(xb, w_ref[:, 0:MEM_DIM], preferred_element_type=F32)
    mk_ref[...] = _head_rms(mk, bd_ref[...], gk_ref[...]).astype(BF16)
    mv_ref[...] = jnp.dot(xb, w_ref[:, MEM_DIM:], preferred_element_type=F32).astype(BF16)


def _mem_kv(mem2d, mem_len, gn, w, bd, gk):
    rows = mem2d.shape[0]
    row = lambda n: pl.BlockSpec((mem_len, n), lambda i: (i, 0))
    return pl.pallas_call(
        _mem_kv_kernel,
        grid=(rows // mem_len,),
        in_specs=[row(D_MODEL), _resident(gn.shape), _resident(w.shape), _resident(bd.shape),
                  _resident(gk.shape)],
        out_specs=[row(MEM_DIM), row(MEM_DIM)],
        out_shape=[jax.ShapeDtypeStruct((rows, MEM_DIM), BF16)] * 2,
        compiler_params=_params("parallel"),
        name="mem_kv",
    )(mem2d, gn, w, bd, gk)


def _head_lane_mask(shape, head_in_pair):
    lane = lax.broadcasted_iota(jnp.int32, shape, len(shape) - 1)
    return (lane // HEAD_DIM) == head_in_pair


def _softmax_block(s, m, l, acc, v):
    m_new = jnp.maximum(m, jnp.max(s, axis=-1, keepdims=True))
    alpha = jnp.exp(m - m_new)
    p = jnp.exp(s - m_new)
    l = alpha * l + jnp.sum(p, axis=-1, keepdims=True)
    acc = alpha * acc + jnp.dot(p.astype(BF16), v, preferred_element_type=F32)
    return m_new, l, acc


def _fox_attn_kernel(tq, tk, q_ref, k_ref, v_ref, c_ref, o_ref):
    i = pl.program_id(2)
    q2 = q_ref[0]
    n_full = (i * tq) // tk
    q_pos = i * tq + lax.broadcasted_iota(jnp.int32, (tq, tk), 0)
    k_off = lax.broadcasted_iota(jnp.int32, (tq, tk), 1)
    outs = []
    for hh in range(2):
        qh = jnp.where(_head_lane_mask(q2.shape, hh), q2, jnp.zeros_like(q2))

        def block(j, carry, masked):
            m, l, acc = carry
            start = pl.multiple_of(j * tk, tk)
            kb = k_ref[0, pl.ds(start, tk), :]
            s = lax.dot_general(qh, kb, (((1,), (1,)), ((), ())), preferred_element_type=F32)
            s = s - c_ref[0, 0, hh:hh + 1, pl.ds(start, tk)]
            if masked:
                s = jnp.where(start + k_off <= q_pos, s, NEG_INF)
            return _softmax_block(s, m, l, acc, v_ref[0, pl.ds(start, tk), :])

        init = (jnp.full((tq, 1), NEG_INF, F32), jnp.zeros((tq, 1), F32),
                jnp.zeros((tq, LANES), F32))
        carry = lax.fori_loop(0, n_full, functools.partial(block, masked=False), init)
        carry = lax.fori_loop(n_full, n_full + tq // tk,
                              functools.partial(block, masked=True), carry)
        _, l, acc = carry
        outs.append(acc * (1.0 / l))
    o_ref[0] = jnp.where(_head_lane_mask(outs[0].shape, 0), outs[0], outs[1]).astype(o_ref.dtype)


def _fox_attn(q, k, v, c_t):
    b, s, _ = q.shape
    tq = min(ATTN_Q_TILE, s)
    tk = min(ATTN_KV_TILE, tq)
    assert s % tq == 0 and tq % tk == 0
    seq = pl.BlockSpec((1, s, LANES), lambda bi, hp, i: (bi, 0, hp))
    return pl.pallas_call(
        functools.partial(_fox_attn_kernel, tq, tk),
        grid=(b, FOX_HEADS // 2, s // tq),
        in_specs=[pl.BlockSpec((1, tq, LANES), lambda bi, hp, i: (bi, i, hp)), seq, seq,
                  pl.BlockSpec((1, 1, 2, s), lambda bi, hp, i: (bi, hp, 0, 0))],
        out_specs=pl.BlockSpec((1, tq, LANES), lambda bi, hp, i: (bi, i, hp)),
        out_shape=jax.ShapeDtypeStruct((b, s, MIX_DIM), BF16),
        compiler_params=_params("parallel", "parallel", "arbitrary"),
        name="fox_attn",
    )(q, k, v, c_t)


def _mem_attn_kernel(q_ref, k_ref, v_ref, o_ref):
    for pair in range(MEM_DIM // LANES):
        lanes = slice(pair * LANES, (pair + 1) * LANES)
        q2 = q_ref[:, lanes]
        k2 = k_ref[:, lanes]
        v2 = v_ref[:, lanes]
        outs = []
        for hh in range(2):
            qh = jnp.where(_head_lane_mask(q2.shape, hh), q2, jnp.zeros_like(q2))
            s = lax.dot_general(qh, k2, (((1,), (1,)), ((), ())), preferred_element_type=F32)
            p = jnp.exp(s - jnp.max(s, axis=-1, keepdims=True))
            l = jnp.sum(p, axis=-1, keepdims=True)
            outs.append(jnp.dot(p.astype(BF16), v2, preferred_element_type=F32) * (1.0 / l))
        o_ref[:, lanes] = jnp.where(_head_lane_mask(outs[0].shape, 0), outs[0],
                                    outs[1]).astype(o_ref.dtype)


def _mem_attn(mq, mk, mv, seq_len, mem_len):
    t = mq.shape[0]
    tm = min(ROW_TILE, seq_len)
    per_seq = seq_len // tm
    kv = pl.BlockSpec((mem_len, MEM_DIM), lambda i: (i // per_seq, 0))
    return pl.pallas_call(
        _mem_attn_kernel,
        grid=(t // tm,),
        in_specs=[pl.BlockSpec((tm, MEM_DIM), lambda i: (i, 0)), kv, kv],
        out_specs=pl.BlockSpec((tm, MEM_DIM), lambda i: (i, 0)),
        out_shape=jax.ShapeDtypeStruct((t, MEM_DIM), BF16),
        compiler_params=_params("parallel"),
        name="mem_attn",
    )(mq, mk, mv)


def _conv_kernel(ts, u_ref, dw_ref, dwb_ref, lng_ref, lnb_ref, o_ref, ext_ref, acc_ref):
    @pl.when(pl.program_id(1) == 0)
    def _():
        ext_ref[0:CONV_HALO, :] = jnp.zeros((CONV_HALO, MIX_DIM), F32)

    @pl.when(pl.program_id(1) > 0)
    def _():
        ext_ref[0:CONV_HALO, :] = ext_ref[ts:ts + CONV_HALO, :]

    ext_ref[CONV_HALO:CONV_HALO + ts, :] = u_ref[0]

    base = CONV_HALO - (CONV_WIDTH - 1)
    n_shift = SUBLANES
    rc = min(CONV_ROW_CHUNK, ts)
    for r0 in range(0, ts, rc):
        for l0 in range(0, MIX_DIM, LANES):
            acc = jnp.zeros((rc, LANES), F32) + dwb_ref[:, l0:l0 + LANES]
            for b in range(n_shift):
                n_a = (CONV_WIDTH - 1 - b) // n_shift + 1
                win = ext_ref[r0 + base + b:r0 + base + b + rc + (n_a - 1) * n_shift,
                              l0:l0 + LANES]
                for a in range(n_a):
                    j = a * n_shift + b
                    acc = acc + dw_ref[j:j + 1, l0:l0 + LANES] * win[a * n_shift:
                                                                      a * n_shift + rc, :]
            acc_ref[r0:r0 + rc, l0:l0 + LANES] = acc

    y = acc_ref[...]
    mu = jnp.mean(y, axis=-1, keepdims=True)
    yc = y - mu
    var = jnp.mean(yc * yc, axis=-1, keepdims=True)
    z = yc * lax.rsqrt(var + EPS) * lng_ref[...] + lnb_ref[...]
    o_ref[0] = _silu(z).astype(o_ref.dtype)


def _conv_module(u, dw, dwb, lng, lnb):
    b, s, _ = u.shape
    ts = min(CONV_TILE, s)
    assert s % ts == 0 and ts >= CONV_HALO
    blk = pl.BlockSpec((1, ts, MIX_DIM), lambda bi, i: (bi, i, 0))
    return pl.pallas_call(
        functools.partial(_conv_kernel, ts),
        grid=(b, s // ts),
        in_specs=[blk, _resident(dw.shape), _resident(dwb.shape), _resident(lng.shape),
                  _resident(lnb.shape)],
        out_specs=blk,
        out_shape=jax.ShapeDtypeStruct((b, s, MIX_DIM), BF16),
        scratch_shapes=[pltpu.VMEM((ts + CONV_HALO, MIX_DIM), F32),
                        pltpu.VMEM((ts, MIX_DIM), F32)],
        compiler_params=_params("parallel", "arbitrary"),
        name="conv_module",
    )(u, dw, dwb, lng, lnb)


def _outproj_math(h_ref, mix_ref, mem_ref, w_ref, gn_ref):
    h2 = (h_ref[...]
          + jnp.dot(mix_ref[...], w_ref[0:MIX_DIM, :], preferred_element_type=F32)
          + jnp.dot(mem_ref[...], w_ref[MIX_DIM:, :], preferred_element_type=F32))
    return h2, _rms(h2, gn_ref[...])


def _outproj_dense_kernel(h_ref, mix_ref, mem_ref, w_ref, gn_ref, h2_ref, z_ref):
    h2, z = _outproj_math(h_ref, mix_ref, mem_ref, w_ref, gn_ref)
    h2_ref[...] = h2
    z_ref[...] = z.astype(BF16)


def _split_bf16(x):
    hi = x.astype(BF16)
    return hi, (x - hi.astype(F32)).astype(BF16)


def _outproj_router_kernel(h_ref, mix_ref, mem_ref, w_ref, gn_ref, wr_hi_ref, wr_lo_ref,
                           h2_ref, zp_ref, route_ref):
    h2, z = _outproj_math(h_ref, mix_ref, mem_ref, w_ref, gn_ref)
    h2_ref[...] = h2

    half = D_MODEL // 2
    lo_bits = pltpu.bitcast(z[:, :half].astype(BF16).astype(F32), jnp.uint32) >> 16
    hi_bits = pltpu.bitcast(z[:, half:].astype(BF16).astype(F32), jnp.uint32) & jnp.uint32(0xFFFF0000)
    zp_ref[...] = lo_bits | hi_bits

    z_hi, z_lo = _split_bf16(z)
    logits = (jnp.dot(z_hi, wr_hi_ref[...], preferred_element_type=F32)
              + jnp.dot(z_lo, wr_hi_ref[...], preferred_element_type=F32)
              + jnp.dot(z_hi, wr_lo_ref[...], preferred_element_type=F32))
    lane = lax.broadcasted_iota(jnp.int32, logits.shape, 1)
    lane_f = lane.astype(F32)
    logits = jnp.where(lane < N_EXPERTS, logits, -jnp.inf)
    l1 = jnp.max(logits, axis=-1, keepdims=True)
    e1 = jnp.min(jnp.where(logits == l1, lane_f, float(LANES)), axis=-1, keepdims=True)
    rest = jnp.where(lane_f == e1, -jnp.inf, logits)
    l2 = jnp.max(rest, axis=-1, keepdims=True)
    e2 = jnp.min(jnp.where(rest == l2, lane_f, float(LANES)), axis=-1, keepdims=True)
    g2 = 1.0 / (1.0 + jnp.exp(l1 - l2))
    g1 = 1.0 - g2
    route = jnp.where(lane == 0, g1, 0.0) + jnp.where(lane == 1, g2, 0.0)
    route = route + jnp.where(lane == 2, e1, 0.0)
    route_ref[...] = route + jnp.where(lane == 3, e2, 0.0)


def _outproj(h, mix, mem, w, gn, router=None):
    t = h.shape[0]
    tm = min(ROW_TILE, t)
    assert t % tm == 0
    row = lambda n: pl.BlockSpec((tm, n), lambda i: (i, 0))
    in_specs = [row(D_MODEL), row(MIX_DIM), row(MEM_DIM), _resident(w.shape), _resident(gn.shape)]
    if router is None:
        return pl.pallas_call(
            _outproj_dense_kernel, grid=(t // tm,), in_specs=in_specs,
            out_specs=[row(D_MODEL), row(D_MODEL)],
            out_shape=[jax.ShapeDtypeStruct((t, D_MODEL), F32),
                       jax.ShapeDtypeStruct((t, D_MODEL), BF16)],
            compiler_params=_params("parallel"), name="outproj_dense",
        )(h, mix, mem, w, gn)
    wr_hi, wr_lo = router
    return pl.pallas_call(
        _outproj_router_kernel, grid=(t // tm,),
        in_specs=in_specs + [_resident(wr_hi.shape), _resident(wr_lo.shape)],
        out_specs=[row(D_MODEL), row(D_MODEL // 2), row(LANES)],
        out_shape=[jax.ShapeDtypeStruct((t, D_MODEL), F32),
                   jax.ShapeDtypeStruct((t, D_MODEL // 2), jnp.uint32),
                   jax.ShapeDtypeStruct((t, LANES), F32)],
        compiler_params=_params("parallel"), name="outproj_router",
    )(h, mix, mem, w, gn, wr_hi, wr_lo)


def _ffn_kernel(h_ref, z_ref, wg_ref, wu_ref, wd_ref, o_ref, acc_ref):
    z = z_ref[...]
    d_ff = wg_ref.shape[1]
    acc_ref[...] = h_ref[...]
    for f0 in range(0, d_ff, FFN_CHUNK):
        g = jnp.dot(z, wg_ref[:, f0:f0 + FFN_CHUNK], preferred_element_type=F32)
        u = jnp.dot(z, wu_ref[:, f0:f0 + FFN_CHUNK], preferred_element_type=F32)
        a = (_silu(g) * u).astype(BF16)
        acc_ref[...] += jnp.dot(a, wd_ref[f0:f0 + FFN_CHUNK, :], preferred_element_type=F32)
    o_ref[...] = acc_ref[...]


def _ffn(h2, z, wg, wu, wd):
    t = h2.shape[0]
    tm = min(ROW_TILE, t)
    assert t % tm == 0 and wg.shape[1] % FFN_CHUNK == 0
    row = pl.BlockSpec((tm, D_MODEL), lambda i: (i, 0))
    return pl.pallas_call(
        _ffn_kernel, grid=(t // tm,),
        in_specs=[row, row, _resident(wg.shape), _resident(wu.shape), _resident(wd.shape)],
        out_specs=row,
        out_shape=jax.ShapeDtypeStruct((t, D_MODEL), F32),
        scratch_shapes=[pltpu.VMEM((tm, D_MODEL), F32)],
        compiler_params=_params("parallel"), name="ffn_dense",
    )(h2, z, wg, wu, wd)


def _row_copy(src, src_row, dst, dst_row, sem):
    return pltpu.make_async_copy(src.at[pl.ds(src_row, 1), :], dst.at[pl.ds(dst_row, 1), :], sem)


def _wait_rows(src, dst, n, sem):
    n8 = pl.multiple_of((n // SUBLANES) * SUBLANES, SUBLANES)

    @pl.when(n8 > 0)
    def _():
        pltpu.make_async_copy(src.at[pl.ds(0, n8), :], dst.at[pl.ds(0, n8), :], sem).wait()

    def one(r, _):
        _row_copy(src, 0, dst, 0, sem).wait()
        return 0
    lax.fori_loop(n8, n, one, 0)


def _experts_kernel(be_ref, valid_ref, nact_ref, tok_ref, dst_ref, zp_hbm, wg_ref, wu_ref, wd_ref,
                    y_hbm, xp_ref, x_ref, acc_ref, sem_in, sem_out):
    i = pl.program_id(0)
    f = pl.program_id(1)
    n_valid = valid_ref[i]
    active = i < nact_ref[0]
    half = D_MODEL // 2

    @pl.when((i == 0) & (f == 0))
    def _():
        xp_ref[...] = jnp.zeros_like(xp_ref)

    @pl.when(active & (f == 0))
    def _():
        def issue(r, _):
            _row_copy(zp_hbm, tok_ref[0, 0, r], xp_ref, r, sem_in).start()
            return 0
        lax.fori_loop(0, n_valid, issue, 0)

        _wait_rows(zp_hbm, xp_ref, n_valid, sem_in)
        xp = xp_ref[...]
        x_ref[:, :half] = pltpu.bitcast(xp << 16, F32).astype(BF16)
        x_ref[:, half:] = pltpu.bitcast(xp & jnp.uint32(0xFFFF0000), F32).astype(BF16)
        acc_ref[...] = jnp.zeros_like(acc_ref)

    @pl.when(active)
    def _():
        for r0 in range(0, EXPERT_ROWS, EXPERT_SUB):
            @pl.when(r0 < n_valid)
            def _():
                x = x_ref[r0:r0 + EXPERT_SUB, :]
                g = jnp.dot(x, wg_ref[0], preferred_element_type=F32)
                u = jnp.dot(x, wu_ref[0], preferred_element_type=F32)
                a = (_silu(g) * u).astype(BF16)
                acc_ref[r0:r0 + EXPERT_SUB, :] += jnp.dot(a, wd_ref[0],
                                                          preferred_element_type=F32)

    @pl.when(active & (f == pl.num_programs(1) - 1))
    def _():
        def issue(r, _):
            _row_copy(acc_ref, r, y_hbm, dst_ref[0, 0, r], sem_out).start()
            return 0
        lax.fori_loop(0, n_valid, issue, 0)
        _wait_rows(acc_ref, y_hbm, n_valid, sem_out)


def _experts(zp, plan, wg, wu, wd):
    t = zp.shape[0]
    blk_expert, valid, n_active, tok_buf, dst_buf = plan
    nb = blk_expert.shape[0]
    d_ff = wg.shape[2]
    tf = EXPERT_FF_TILE
    assert d_ff % tf == 0
    idx = pl.BlockSpec((1, 1, EXPERT_ROWS), lambda i, f, *_: (i, 0, 0), memory_space=pltpu.SMEM)
    grid_spec = pltpu.PrefetchScalarGridSpec(
        num_scalar_prefetch=3,
        grid=(nb, d_ff // tf),
        in_specs=[idx, idx, pl.BlockSpec(memory_space=pl.ANY),
                  pl.BlockSpec((1, D_MODEL, tf), lambda i, f, be, *_: (be[i], 0, f)),
                  pl.BlockSpec((1, D_MODEL, tf), lambda i, f, be, *_: (be[i], 0, f)),
                  pl.BlockSpec((1, tf, D_MODEL), lambda i, f, be, *_: (be[i], f, 0))],
        out_specs=pl.BlockSpec(memory_space=pl.ANY),
        scratch_shapes=[pltpu.VMEM((EXPERT_ROWS, D_MODEL // 2), jnp.uint32),
                        pltpu.VMEM((EXPERT_ROWS, D_MODEL), BF16),
                        pltpu.VMEM((EXPERT_ROWS, D_MODEL), F32),
                        pltpu.SemaphoreType.DMA(()), pltpu.SemaphoreType.DMA(())],
    )
    return pl.pallas_call(
        _experts_kernel, grid_spec=grid_spec,
        out_shape=jax.ShapeDtypeStruct((TOP_K * t, D_MODEL), F32),
        compiler_params=_params("arbitrary", "arbitrary"), name="moe_experts",
    )(blk_expert, valid, n_active, tok_buf, dst_buf, zp, wg, wu, wd)


def _route_plan(route, t):
    a = t * TOP_K
    nb = a // EXPERT_ROWS + N_EXPERTS
    rows = nb * EXPERT_ROWS
    flat_e = route[:, 2:4].astype(jnp.int32).reshape(a)
    onehot = (flat_e[:, None] == jnp.arange(N_EXPERTS, dtype=jnp.int32)[None, :]).astype(jnp.int32)
    csum = jnp.cumsum(onehot, axis=0)
    rank = jnp.sum(csum * onehot, axis=1) - 1
    sizes = csum[-1]
    padded = ((sizes + EXPERT_ROWS - 1) // EXPERT_ROWS) * EXPERT_ROWS
    pad_end = jnp.cumsum(padded)
    pad_start = pad_end - padded
    dest = jnp.sum(pad_start[None, :] * onehot, axis=1) + rank
    assign = jnp.arange(a, dtype=jnp.int32)
    tok = assign // TOP_K
    tok_buf = jnp.zeros((rows,), jnp.int32).at[dest].set(tok)
    dst_buf = jnp.zeros((rows,), jnp.int32).at[dest].set(tok + (assign % TOP_K) * t)
    blk_start = jnp.arange(nb, dtype=jnp.int32) * EXPERT_ROWS
    blk_expert = jnp.minimum(jnp.searchsorted(pad_end, blk_start, side="right"),
                             N_EXPERTS - 1).astype(jnp.int32)
    valid = jnp.clip(pad_start[blk_expert] + sizes[blk_expert] - blk_start, 0, EXPERT_ROWS)
    n_active = (pad_end[-1] // EXPERT_ROWS).reshape(1).astype(jnp.int32)
    return (blk_expert, valid.astype(jnp.int32), n_active,
            tok_buf.reshape(nb, 1, EXPERT_ROWS), dst_buf.reshape(nb, 1, EXPERT_ROWS))


def _combine_kernel(h_ref, y0_ref, y1_ref, route_ref, o_ref):
    g = route_ref[...]
    o_ref[...] = h_ref[...] + (g[:, 0:1] * y0_ref[...] + g[:, 1:2] * y1_ref[...])


def _combine(h2, y, route):
    t = h2.shape[0]
    tm = min(ROW_TILE, t)
    nt = t // tm
    row = pl.BlockSpec((tm, D_MODEL), lambda i: (i, 0))
    return pl.pallas_call(
        _combine_kernel, grid=(nt,),
        in_specs=[row, row, pl.BlockSpec((tm, D_MODEL), lambda i: (i + nt, 0)),
                  pl.BlockSpec((tm, LANES), lambda i: (i, 0))],
        out_specs=row,
        out_shape=jax.ShapeDtypeStruct((t, D_MODEL), F32),
        compiler_params=_params("parallel"), name="moe_combine",
    )(h2, y, y, route)


def _row(v):
    return v.reshape(1, -1).astype(F32)


def _tile_heads(g, n, scale=1.0):
    return _row(jnp.tile(g.astype(F32) * scale, n))


def kernel(x, mem, norm_mix, norm_mem, norm_ffn, w_mem_kv, g_mq, g_mk, fox_w_in, fox_b_f, fox_g_q, fox_g_k, fox_w_out, conv_w_in, conv_b_in, conv_dw, conv_dw_b, conv_ln_g, conv_ln_b, conv_w_out, ffn_w_gate, ffn_w_up, ffn_w_down, moe_router, moe_w_gate, moe_w_up, moe_w_down):
    b, s, d = x.shape
    mem_len = mem.shape[1]
    t = b * s
    assert d == D_MODEL
    scale = HEAD_DIM ** -0.5

    group = jnp.arange(MIX_DIM, dtype=jnp.int32) // HEAD_DIM
    ones_bd = (group[:, None] == group[None, :]).astype(BF16)
    ones_bd_mem = ones_bd[:MEM_DIM, :MEM_DIM]

    h = x.reshape(t, d)
    mem2d = mem.reshape(b * mem_len, d)
    for i in range(DEPTH):
        j = i // 2
        gmq = _tile_heads(g_mq[i], MEM_DIM // HEAD_DIM, scale)
        mk, mv = _mem_kv(mem2d, mem_len, _row(norm_mem[i]), w_mem_kv[i].astype(BF16),
                         ones_bd_mem, _tile_heads(g_mk[i], MEM_DIM // HEAD_DIM))
        if i % 2 == 0:
            w = fox_w_in[j]
            m3 = 3 * MIX_DIM
            w_cat = jnp.concatenate(
                [w[:, :m3], w[:, m3 + FOX_HEADS:], w[:, m3:m3 + FOX_HEADS],
                 jnp.zeros((d, F_LANES - FOX_HEADS), w.dtype)], axis=1).astype(BF16)
            bf = jnp.pad(fox_b_f[j].astype(F32), (0, F_LANES - FOX_HEADS)).reshape(1, F_LANES)
            q, k, v, mq, c = _fox_inproj(
                h, s, _row(norm_mix[i]), w_cat, ones_bd,
                _tile_heads(fox_g_q[j], FOX_HEADS, scale), _tile_heads(fox_g_k[j], FOX_HEADS),
                gmq, bf)
            c_t = c[:, :FOX_HEADS].reshape(b, s, FOX_HEADS // 2, 2).transpose(0, 2, 3, 1)
            mix = _fox_attn(q.reshape(b, s, MIX_DIM), k.reshape(b, s, MIX_DIM),
                            v.reshape(b, s, MIX_DIM), c_t).reshape(t, MIX_DIM)
            w_out = fox_w_out[j]
        else:
            u, mq = _conv_inproj(h, s, _row(norm_mix[i]), conv_w_in[j].astype(BF16),
                                 _row(conv_b_in[j]), ones_bd_mem, gmq)
            mix = _conv_module(u.reshape(b, s, MIX_DIM), conv_dw[j].astype(F32),
                               _row(conv_dw_b[j]), _row(conv_ln_g[j]),
                               _row(conv_ln_b[j])).reshape(t, MIX_DIM)
            w_out = conv_w_out[j]
        mem_out = _mem_attn(mq, mk, mv, s, mem_len)
        if i % 2 == 0:
            h2, z = _outproj(h, mix, mem_out, w_out.astype(BF16), _row(norm_ffn[i]))
            h = _ffn(h2, z, ffn_w_gate[j].astype(BF16), ffn_w_up[j].astype(BF16),
                     ffn_w_down[j].astype(BF16))
        else:
            wr = jnp.pad(moe_router[j].astype(F32), ((0, 0), (0, LANES - N_EXPERTS)))
            wr_hi = wr.astype(BF16)
            wr_lo = (wr - wr_hi.astype(F32)).astype(BF16)
            h2, zp, route = _outproj(h, mix, mem_out, w_out.astype(BF16), _row(norm_ffn[i]),
                                     router=(wr_hi, wr_lo))
            plan = _route_plan(route, t)
            y = _experts(zp, plan, moe_w_gate[j].astype(BF16), moe_w_up[j].astype(BF16),
                         moe_w_down[j].astype(BF16))
            h = _combine(h2, y, route)
    return h.reshape(b, s, d)
```

```python
import functools

import jax
import jax.numpy as jnp
from jax import lax
from jax.experimental import pallas as pl
from jax.experimental.pallas import tpu as pltpu

F32 = jnp.float32
BF16 = jnp.bfloat16

D_MODEL = 1024
HEAD_DIM = 64
MEM_DIM = 256
MIX_DIM = D_MODEL - MEM_DIM
FOX_HEADS = MIX_DIM // HEAD_DIM
CONV_WIDTH = 31
N_EXPERTS = 8
TOP_K = 2
DEPTH = 4
EPS = 1e-6
NEG_INF = -1e30

LANES = 128
SUBLANES = 8
VMEM_LIMIT_BYTES = 56 * 1024 * 1024

ROW_TILE = 512
ATTN_Q_TILE = 512
ATTN_KV_TILE = 512
CONV_TILE = 256
CONV_HALO = 32
CONV_ROW_CHUNK = 64
FFN_CHUNK = 256
EXPERT_ROWS = 1024
EXPERT_SUB = 256
EXPERT_FF_TILE = 512
F_LANES = LANES


def _params(*sem):
    return pltpu.CompilerParams(dimension_semantics=sem, vmem_limit_bytes=VMEM_LIMIT_BYTES)


def _resident(shape):
    nd = len(shape)
    return pl.BlockSpec(shape, lambda *_: (0,) * nd, pipeline_mode=pl.Buffered(1))


def _rms(x, g):
    ms = jnp.mean(x * x, axis=-1, keepdims=True)
    return x * lax.rsqrt(ms + EPS) * g


def _head_rms(x, ones_bd, g):
    ss = jnp.dot((x * x).astype(BF16), ones_bd, preferred_element_type=F32)
    return x * lax.rsqrt(ss * (1.0 / HEAD_DIM) + EPS) * g


def _silu(x):
    return x * (1.0 / (1.0 + jnp.exp(-x)))


def _log_sigmoid(x):
    return jnp.minimum(x, 0.0) - jnp.log(1.0 + jnp.exp(-jnp.abs(x)))


def _row_prefix_sum(x):
    n = x.shape[0]
    row = lax.broadcasted_iota(jnp.int32, x.shape, 0)
    s = 1
    while s < n:
        x = x + jnp.where(row >= s, pltpu.roll(x, s, 0), 0.0)
        s *= 2
    return x


def _fox_inproj_kernel(tiles_per_seq, h_ref, gn_ref, w_ref, bd_ref, gq_ref, gk_ref, gmq_ref,
                       bf_ref, q_ref, k_ref, v_ref, mq_ref, c_ref, carry_ref):
    xb = _rms(h_ref[...], gn_ref[...]).astype(BF16)
    m = MIX_DIM
    q = jnp.dot(xb, w_ref[:, 0:m], preferred_element_type=F32)
    q_ref[...] = _head_rms(q, bd_ref[...], gq_ref[...]).astype(BF16)
    k = jnp.dot(xb, w_ref[:, m:2 * m], preferred_element_type=F32)
    k_ref[...] = _head_rms(k, bd_ref[...], gk_ref[...]).astype(BF16)
    v_ref[...] = jnp.dot(xb, w_ref[:, 2 * m:3 * m], preferred_element_type=F32).astype(BF16)
    mq = jnp.dot(xb, w_ref[:, 3 * m:3 * m + MEM_DIM], preferred_element_type=F32)
    mq_ref[...] = _head_rms(mq, bd_ref[0:MEM_DIM, 0:MEM_DIM], gmq_ref[...]).astype(BF16)

    f = jnp.dot(xb, w_ref[:, 3 * m + MEM_DIM:], preferred_element_type=F32) + bf_ref[...]
    lane = lax.broadcasted_iota(jnp.int32, f.shape, 1)
    lf = jnp.where(lane < FOX_HEADS, _log_sigmoid(f), 0.0)

    @pl.when(pl.program_id(0) % tiles_per_seq == 0)
    def _():
        carry_ref[...] = jnp.zeros_like(carry_ref)

    c = _row_prefix_sum(lf) + carry_ref[0:1, :]
    c_ref[...] = c
    carry_ref[0:1, :] = c[-1:, :]


def _fox_inproj(h, seq_len, gn, w, bd, gq, gk, gmq, bf):
    t = h.shape[0]
    tm = min(ROW_TILE, seq_len)
    assert seq_len % tm == 0
    row = lambda n: pl.BlockSpec((tm, n), lambda i: (i, 0))
    return pl.pallas_call(
        functools.partial(_fox_inproj_kernel, seq_len // tm),
        grid=(t // tm,),
        in_specs=[row(D_MODEL), _resident(gn.shape), _resident(w.shape), _resident(bd.shape),
                  _resident(gq.shape), _resident(gk.shape), _resident(gmq.shape),
                  _resident(bf.shape)],
        out_specs=[row(MIX_DIM), row(MIX_DIM), row(MIX_DIM), row(MEM_DIM), row(F_LANES)],
        out_shape=[jax.ShapeDtypeStruct((t, MIX_DIM), BF16)] * 3
        + [jax.ShapeDtypeStruct((t, MEM_DIM), BF16), jax.ShapeDtypeStruct((t, F_LANES), F32)],
        scratch_shapes=[pltpu.VMEM((SUBLANES, F_LANES), F32)],
        compiler_params=_params("arbitrary"),
        name="fox_inproj",
    )(h, gn, w, bd, gq, gk, gmq, bf)


def _conv_inproj_kernel(h_ref, gn_ref, w_ref, b_ref, bd_ref, gmq_ref, u_ref, mq_ref):
    xb = _rms(h_ref[...], gn_ref[...]).astype(BF16)
    m = MIX_DIM
    a = jnp.dot(xb, w_ref[:, 0:m], preferred_element_type=F32) + b_ref[:, 0:m]
    g = jnp.dot(xb, w_ref[:, m:2 * m], preferred_element_type=F32) + b_ref[:, m:2 * m]
    u_ref[...] = a * (1.0 / (1.0 + jnp.exp(-g)))
    mq = jnp.dot(xb, w_ref[:, 2 * m:], preferred_element_type=F32)
    mq_ref[...] = _head_rms(mq, bd_ref[...], gmq_ref[...]).astype(BF16)


def _conv_inproj(h, seq_len, gn, w, b, bd, gmq):
    t = h.shape[0]
    tm = min(ROW_TILE, seq_len)
    assert t % tm == 0
    row = lambda n: pl.BlockSpec((tm, n), lambda i: (i, 0))
    return pl.pallas_call(
        _conv_inproj_kernel,
        grid=(t // tm,),
        in_specs=[row(D_MODEL), _resident(gn.shape), _resident(w.shape), _resident(b.shape),
                  _resident(bd.shape), _resident(gmq.shape)],
        out_specs=[row(MIX_DIM), row(MEM_DIM)],
        out_shape=[jax.ShapeDtypeStruct((t, MIX_DIM), F32),
                   jax.ShapeDtypeStruct((t, MEM_DIM), BF16)],
        compiler_params=_params("parallel"),
        name="conv_inproj",
    )(h, gn, w, b, bd, gmq)


def _mem_kv_kernel(m_ref, gn_ref, w_ref, bd_ref, gk_ref, mk_ref, mv_ref):
    xb = _rms(m_ref[...], gn_ref[...]).astype(BF16)
    mk = jnp.dot(xb, w_ref[:, 0:MEM_DIM], preferred_element_type=F32)
    mk_ref[...] = _head_rms(mk, bd_ref[...], gk_ref[...]).astype(BF16)
    mv_ref[...] = jnp.dot(xb, w_ref[:, MEM_DIM:], preferred_element_type=F32).astype(BF16)


def _mem_kv(mem2d, mem_len, gn, w, bd, gk):
    rows = mem2d.shape[0]
    row = lambda n: pl.BlockSpec((mem_len, n), lambda i: (i, 0))
    return pl.pallas_call(
        _mem_kv_kernel,
        grid=(rows // mem_len,),
        in_specs=[row(D_MODEL), _resident(gn.shape), _resident(w.shape), _resident(bd.shape),
                  _resident(gk.shape)],
        out_specs=[row(MEM_DIM), row(MEM_DIM)],
        out_shape=[jax.ShapeDtypeStruct((rows, MEM_DIM), BF16)] * 2,
        compiler_params=_params("parallel"),
        name="mem_kv",
    )(mem2d, gn, w, bd, gk)


def _head_lane_mask(shape, head_in_pair):
    lane = lax.broadcasted_iota(jnp.int32, shape, len(shape) - 1)
    return (lane // HEAD_DIM) == head_in_pair


def _fox_attn_kernel(tq, tk, q_ref, k_ref, v_ref, c_ref, o_ref, m_ref, l_ref, acc_ref):
    i = pl.program_id(2)
    q2 = q_ref[0]
    zero = jnp.zeros_like(q2)
    q_stack = jnp.concatenate([jnp.where(_head_lane_mask(q2.shape, hh), q2, zero)
                               for hh in range(2)], axis=0)
    m_ref[...] = jnp.full(m_ref.shape, NEG_INF, F32)
    l_ref[...] = jnp.zeros(l_ref.shape, F32)
    acc_ref[...] = jnp.zeros(acc_ref.shape, F32)
    n_full = (i * tq) // tk
    q_pos = i * tq + lax.broadcasted_iota(jnp.int32, (tq, tk), 0)
    k_off = lax.broadcasted_iota(jnp.int32, (tq, tk), 1)

    def block(j, _, masked):
        start = pl.multiple_of(j * tk, tk)
        kb = k_ref[0, pl.ds(start, tk), :]
        vb = v_ref[0, pl.ds(start, tk), :]
        s2 = lax.dot_general(q_stack, kb, (((1,), (1,)), ((), ())), preferred_element_type=F32)
        for hh in range(2):
            s = s2[hh * tq:(hh + 1) * tq] - c_ref[0, 0, hh:hh + 1, pl.ds(start, tk)]
            if masked:
                s = jnp.where(start + k_off <= q_pos, s, NEG_INF)
            m_old = m_ref[hh]
            m_new = jnp.maximum(m_old, jnp.max(s, axis=-1, keepdims=True))
            alpha = jnp.exp(m_old - m_new)
            ps = [jnp.exp(s[:, t0:t0 + LANES] - m_new) for t0 in range(0, tk, LANES)]
            l_ref[hh] = alpha * l_ref[hh] + functools.reduce(lambda a, b: a + b, ps)
            p = jnp.concatenate(ps, axis=1).astype(BF16)
            acc_ref[hh] = alpha * acc_ref[hh] + jnp.dot(p, vb, preferred_element_type=F32)
            m_ref[hh] = m_new
        return 0

    lax.fori_loop(0, n_full, functools.partial(block, masked=False), 0)
    lax.fori_loop(n_full, n_full + tq // tk, functools.partial(block, masked=True), 0)
    outs = [acc_ref[hh] * (1.0 / jnp.sum(l_ref[hh], axis=-1, keepdims=True)) for hh in range(2)]
    o_ref[0] = jnp.where(_head_lane_mask(outs[0].shape, 0), outs[0], outs[1]).astype(o_ref.dtype)


def _fox_attn(q, k, v, c_t):
    b, s, _ = q.shape
    tq = min(ATTN_Q_TILE, s)
    tk = min(ATTN_KV_TILE, tq)
    assert s % tq == 0 and tq % tk == 0
    seq = pl.BlockSpec((1, s, LANES), lambda bi, hp, i: (bi, 0, hp))
    return pl.pallas_call(
        functools.partial(_fox_attn_kernel, tq, tk),
        grid=(b, FOX_HEADS // 2, s // tq),
        in_specs=[pl.BlockSpec((1, tq, LANES), lambda bi, hp, i: (bi, i, hp)), seq, seq,
                  pl.BlockSpec((1, 1, 2, s), lambda bi, hp, i: (bi, hp, 0, 0))],
        out_specs=pl.BlockSpec((1, tq, LANES), lambda bi, hp, i: (bi, i, hp)),
        out_shape=jax.ShapeDtypeStruct((b, s, MIX_DIM), BF16),
        scratch_shapes=[pltpu.VMEM((2, tq, LANES), F32)] * 3,
        compiler_params=_params("parallel", "parallel", "arbitrary"),
        name="fox_attn",
    )(q, k, v, c_t)


def _mem_attn_kernel(q_ref, k_ref, v_ref, o_ref):
    for pair in range(MEM_DIM // LANES):
        lanes = slice(pair * LANES, (pair + 1) * LANES)
        q2 = q_ref[:, lanes]
        k2 = k_ref[:, lanes]
        v2 = v_ref[:, lanes]
        outs = []
        for hh in range(2):
            qh = jnp.where(_head_lane_mask(q2.shape, hh), q2, jnp.zeros_like(q2))
            s = lax.dot_general(qh, k2, (((1,), (1,)), ((), ())), preferred_element_type=F32)
            p = jnp.exp(s - jnp.max(s, axis=-1, keepdims=True))
            l = jnp.sum(p, axis=-1, keepdims=True)
            outs.append(jnp.dot(p.astype(BF16), v2, preferred_element_type=F32) * (1.0 / l))
        o_ref[:, lanes] = jnp.where(_head_lane_mask(outs[0].shape, 0), outs[0],
                                    outs[1]).astype(o_ref.dtype)


def _mem_attn(mq, mk, mv, seq_len, mem_len):
    t = mq.shape[0]
    tm = min(ROW_TILE, seq_len)
    per_seq = seq_len // tm
    kv = pl.BlockSpec((mem_len, MEM_DIM), lambda i: (i // per_seq, 0))
    return pl.pallas_call(
        _mem_attn_kernel,
        grid=(t // tm,),
        in_specs=[pl.BlockSpec((tm, MEM_DIM), lambda i: (i, 0)), kv, kv],
        out_specs=pl.BlockSpec((tm, MEM_DIM), lambda i: (i, 0)),
        out_shape=jax.ShapeDtypeStruct((t, MEM_DIM), BF16),
        compiler_params=_params("parallel"),
        name="mem_attn",
    )(mq, mk, mv)


def _conv_kernel(ts, u_ref, dw_ref, dwb_ref, lng_ref, lnb_ref, o_ref, ext_ref, acc_ref):
    @pl.when(pl.program_id(1) == 0)
    def _():
        ext_ref[0:CONV_HALO, :] = jnp.zeros((CONV_HALO, MIX_DIM), F32)

    @pl.when(pl.program_id(1) > 0)
    def _():
        ext_ref[0:CONV_HALO, :] = ext_ref[ts:ts + CONV_HALO, :]

    ext_ref[CONV_HALO:CONV_HALO + ts, :] = u_ref[0]

    base = CONV_HALO - (CONV_WIDTH - 1)
    n_shift = SUBLANES
    rc = min(CONV_ROW_CHUNK, ts)
    for r0 in range(0, ts, rc):
        for l0 in range(0, MIX_DIM, LANES):
            acc = jnp.zeros((rc, LANES), F32) + dwb_ref[:, l0:l0 + LANES]
            for b in range(n_shift):
                n_a = (CONV_WIDTH - 1 - b) // n_shift + 1
                win = ext_ref[r0 + base + b:r0 + base + b + rc + (n_a - 1) * n_shift,
                              l0:l0 + LANES]
                for a in range(n_a):
                    j = a * n_shift + b
                    acc = acc + dw_ref[j:j + 1, l0:l0 + LANES] * win[a * n_shift:
                                                                      a * n_shift + rc, :]
            acc_ref[r0:r0 + rc, l0:l0 + LANES] = acc

    y = acc_ref[...]
    mu = jnp.mean(y, axis=-1, keepdims=True)
    yc = y - mu
    var = jnp.mean(yc * yc, axis=-1, keepdims=True)
    z = yc * lax.rsqrt(var + EPS) * lng_ref[...] + lnb_ref[...]
    o_ref[0] = _silu(z).astype(o_ref.dtype)


def _conv_module(u, dw, dwb, lng, lnb):
    b, s, _ = u.shape
    ts = min(CONV_TILE, s)
    assert s % ts == 0 and ts >= CONV_HALO
    blk = pl.BlockSpec((1, ts, MIX_DIM), lambda bi, i: (bi, i, 0))
    return pl.pallas_call(
        functools.partial(_conv_kernel, ts),
        grid=(b, s // ts),
        in_specs=[blk, _resident(dw.shape), _resident(dwb.shape), _resident(lng.shape),
                  _resident(lnb.shape)],
        out_specs=blk,
        out_shape=jax.ShapeDtypeStruct((b, s, MIX_DIM), BF16),
        scratch_shapes=[pltpu.VMEM((ts + CONV_HALO, MIX_DIM), F32),
                        pltpu.VMEM((ts, MIX_DIM), F32)],
        compiler_params=_params("parallel", "arbitrary"),
        name="conv_module",
    )(u, dw, dwb, lng, lnb)


def _outproj_math(h_ref, mix_ref, mem_ref, w_ref, gn_ref):
    h2 = (h_ref[...]
          + jnp.dot(mix_ref[...], w_ref[0:MIX_DIM, :], preferred_element_type=F32)
          + jnp.dot(mem_ref[...], w_ref[MIX_DIM:, :], preferred_element_type=F32))
    return h2, _rms(h2, gn_ref[...])


def _outproj_dense_kernel(h_ref, mix_ref, mem_ref, w_ref, gn_ref, h2_ref, z_ref):
    h2, z = _outproj_math(h_ref, mix_ref, mem_ref, w_ref, gn_ref)
    h2_ref[...] = h2
    z_ref[...] = z.astype(BF16)


def _split_bf16(x):
    hi = x.astype(BF16)
    return hi, (x - hi.astype(F32)).astype(BF16)


def _outproj_router_kernel(h_ref, mix_ref, mem_ref, w_ref, gn_ref, wr_hi_ref, wr_lo_ref,
                           h2_ref, zp_ref, route_ref):
    h2, z = _outproj_math(h_ref, mix_ref, mem_ref, w_ref, gn_ref)
    h2_ref[...] = h2

    half = D_MODEL // 2
    lo_bits = pltpu.bitcast(z[:, :half].astype(BF16).astype(F32), jnp.uint32) >> 16
    hi_bits = pltpu.bitcast(z[:, half:].astype(BF16).astype(F32), jnp.uint32) & jnp.uint32(0xFFFF0000)
    zp_ref[...] = lo_bits | hi_bits

    z_hi, z_lo = _split_bf16(z)
    logits = (jnp.dot(z_hi, wr_hi_ref[...], preferred_element_type=F32)
              + jnp.dot(z_lo, wr_hi_ref[...], preferred_element_type=F32)
              + jnp.dot(z_hi, wr_lo_ref[...], preferred_element_type=F32))
    lane = lax.broadcasted_iota(jnp.int32, logits.shape, 1)
    lane_f = lane.astype(F32)
    logits = jnp.where(lane < N_EXPERTS, logits, -jnp.inf)
    l1 = jnp.max(logits, axis=-1, keepdims=True)
    e1 = jnp.min(jnp.where(logits == l1, lane_f, float(LANES)), axis=-1, keepdims=True)
    rest = jnp.where(lane_f == e1, -jnp.inf, logits)
    l2 = jnp.max(rest, axis=-1, keepdims=True)
    e2 = jnp.min(jnp.where(rest == l2, lane_f, float(LANES)), axis=-1, keepdims=True)
    g2 = 1.0 / (1.0 + jnp.exp(l1 - l2))
    g1 = 1.0 - g2
    route = jnp.where(lane == 0, g1, 0.0) + jnp.where(lane == 1, g2, 0.0)
    route = route + jnp.where(lane == 2, e1, 0.0)
    route_ref[...] = route + jnp.where(lane == 3, e2, 0.0)


def _outproj(h, mix, mem, w, gn, router=None):
    t = h.shape[0]
    tm = min(ROW_TILE, t)
    assert t % tm == 0
    row = lambda n: pl.BlockSpec((tm, n), lambda i: (i, 0))
    in_specs = [row(D_MODEL), row(MIX_DIM), row(MEM_DIM), _resident(w.shape), _resident(gn.shape)]
    if router is None:
        return pl.pallas_call(
            _outproj_dense_kernel, grid=(t // tm,), in_specs=in_specs,
            out_specs=[row(D_MODEL), row(D_MODEL)],
            out_shape=[jax.ShapeDtypeStruct((t, D_MODEL), F32),
                       jax.ShapeDtypeStruct((t, D_MODEL), BF16)],
            compiler_params=_params("parallel"), name="outproj_dense",
        )(h, mix, mem, w, gn)
    wr_hi, wr_lo = router
    return pl.pallas_call(
        _outproj_router_kernel, grid=(t // tm,),
        in_specs=in_specs + [_resident(wr_hi.shape), _resident(wr_lo.shape)],
        out_specs=[row(D_MODEL), row(D_MODEL // 2), row(LANES)],
        out_shape=[jax.ShapeDtypeStruct((t, D_MODEL), F32),
                   jax.ShapeDtypeStruct((t, D_MODEL // 2), jnp.uint32),
                   jax.ShapeDtypeStruct((t, LANES), F32)],
        compiler_params=_params("parallel"), name="outproj_router",
    )(h, mix, mem, w, gn, wr_hi, wr_lo)


def _ffn_kernel(h_ref, z_ref, wg_ref, wu_ref, wd_ref, o_ref, acc_ref):
    z = z_ref[...]
    d_ff = wg_ref.shape[1]
    acc_ref[...] = h_ref[...]
    for f0 in range(0, d_ff, FFN_CHUNK):
        g = jnp.dot(z, wg_ref[:, f0:f0 + FFN_CHUNK], preferred_element_type=F32)
        u = jnp.dot(z, wu_ref[:, f0:f0 + FFN_CHUNK], preferred_element_type=F32)
        a = (_silu(g) * u).astype(BF16)
        acc_ref[...] += jnp.dot(a, wd_ref[f0:f0 + FFN_CHUNK, :], preferred_element_type=F32)
    o_ref[...] = acc_ref[...]


def _ffn(h2, z, wg, wu, wd):
    t = h2.shape[0]
    tm = min(ROW_TILE, t)
    assert t % tm == 0 and wg.shape[1] % FFN_CHUNK == 0
    row = pl.BlockSpec((tm, D_MODEL), lambda i: (i, 0))
    return pl.pallas_call(
        _ffn_kernel, grid=(t // tm,),
        in_specs=[row, row, _resident(wg.shape), _resident(wu.shape), _resident(wd.shape)],
        out_specs=row,
        out_shape=jax.ShapeDtypeStruct((t, D_MODEL), F32),
        scratch_shapes=[pltpu.VMEM((tm, D_MODEL), F32)],
        compiler_params=_params("parallel"), name="ffn_dense",
    )(h2, z, wg, wu, wd)


def _row_copy(src, src_row, dst, dst_row, sem):
    return pltpu.make_async_copy(src.at[pl.ds(src_row, 1), :], dst.at[pl.ds(dst_row, 1), :], sem)


def _wait_rows(src, dst, n, sem):
    n8 = pl.multiple_of((n // SUBLANES) * SUBLANES, SUBLANES)

    @pl.when(n8 > 0)
    def _():
        pltpu.make_async_copy(src.at[pl.ds(0, n8), :], dst.at[pl.ds(0, n8), :], sem).wait()

    def one(r, _):
        _row_copy(src, 0, dst, 0, sem).wait()
        return 0
    lax.fori_loop(n8, n, one, 0)


def _experts_kernel(be_ref, valid_ref, nact_ref, tok_ref, dst_ref, zp_hbm, wg_ref, wu_ref, wd_ref,
                    y_hbm, xp_ref, x_ref, acc_ref, sem_in, sem_out):
    i = pl.program_id(0)
    f = pl.program_id(1)
    n_valid = valid_ref[i]
    active = i < nact_ref[0]
    half = D_MODEL // 2

    @pl.when((i == 0) & (f == 0))
    def _():
        xp_ref[...] = jnp.zeros_like(xp_ref)

    @pl.when(active & (f == 0))
    def _():
        def issue(r, _):
            _row_copy(zp_hbm, tok_ref[0, 0, r], xp_ref, r, sem_in).start()
            return 0
        lax.fori_loop(0, n_valid, issue, 0)

        _wait_rows(zp_hbm, xp_ref, n_valid, sem_in)
        xp = xp_ref[...]
        x_ref[:, :half] = pltpu.bitcast(xp << 16, F32).astype(BF16)
        x_ref[:, half:] = pltpu.bitcast(xp & jnp.uint32(0xFFFF0000), F32).astype(BF16)
        acc_ref[...] = jnp.zeros_like(acc_ref)

    @pl.when(active)
    def _():
        for r0 in range(0, EXPERT_ROWS, EXPERT_SUB):
            @pl.when(r0 < n_valid)
            def _():
                x = x_ref[r0:r0 + EXPERT_SUB, :]
                g = jnp.dot(x, wg_ref[0], preferred_element_type=F32)
                u = jnp.dot(x, wu_ref[0], preferred_element_type=F32)
                a = (_silu(g) * u).astype(BF16)
                acc_ref[r0:r0 + EXPERT_SUB, :] += jnp.dot(a, wd_ref[0],
                                                          preferred_element_type=F32)

    @pl.when(active & (f == pl.num_programs(1) - 1))
    def _():
        def issue(r, _):
            _row_copy(acc_ref, r, y_hbm, dst_ref[0, 0, r], sem_out).start()
            return 0
        lax.fori_loop(0, n_valid, issue, 0)
        _wait_rows(acc_ref, y_hbm, n_valid, sem_out)


def _experts(zp, plan, wg, wu, wd):
    t = zp.shape[0]
    blk_expert, valid, n_active, tok_buf, dst_buf = plan
    nb = blk_expert.shape[0]
    d_ff = wg.shape[2]
    tf = EXPERT_FF_TILE
    assert d_ff % tf == 0
    idx = pl.BlockSpec((1, 1, EXPERT_ROWS), lambda i, f, *_: (i, 0, 0), memory_space=pltpu.SMEM)
    grid_spec = pltpu.PrefetchScalarGridSpec(
        num_scalar_prefetch=3,
        grid=(nb, d_ff // tf),
        in_specs=[idx, idx, pl.BlockSpec(memory_space=pl.ANY),
                  pl.BlockSpec((1, D_MODEL, tf), lambda i, f, be, *_: (be[i], 0, f)),
                  pl.BlockSpec((1, D_MODEL, tf), lambda i, f, be, *_: (be[i], 0, f)),
                  pl.BlockSpec((1, tf, D_MODEL), lambda i, f, be, *_: (be[i], f, 0))],
        out_specs=pl.BlockSpec(memory_space=pl.ANY),
        scratch_shapes=[pltpu.VMEM((EXPERT_ROWS, D_MODEL // 2), jnp.uint32),
                        pltpu.VMEM((EXPERT_ROWS, D_MODEL), BF16),
                        pltpu.VMEM((EXPERT_ROWS, D_MODEL), F32),
                        pltpu.SemaphoreType.DMA(()), pltpu.SemaphoreType.DMA(())],
    )
    return pl.pallas_call(
        _experts_kernel, grid_spec=grid_spec,
        out_shape=jax.ShapeDtypeStruct((TOP_K * t, D_MODEL), F32),
        compiler_params=_params("arbitrary", "arbitrary"), name="moe_experts",
    )(blk_expert, valid, n_active, tok_buf, dst_buf, zp, wg, wu, wd)


def _route_plan(route, t):
    a = t * TOP_K
    nb = a // EXPERT_ROWS + N_EXPERTS
    rows = nb * EXPERT_ROWS
    flat_e = route[:, 2:4].astype(jnp.int32).reshape(a)
    onehot = (flat_e[:, None] == jnp.arange(N_EXPERTS, dtype=jnp.int32)[None, :]).astype(jnp.int32)
    csum = jnp.cumsum(onehot, axis=0)
    rank = jnp.sum(csum * onehot, axis=1) - 1
    sizes = csum[-1]
    padded = ((sizes + EXPERT_ROWS - 1) // EXPERT_ROWS) * EXPERT_ROWS
    pad_end = jnp.cumsum(padded)
    pad_start = pad_end - padded
    dest = jnp.sum(pad_start[None, :] * onehot, axis=1) + rank
    assign = jnp.arange(a, dtype=jnp.int32)
    tok = assign // TOP_K
    tok_buf = jnp.zeros((rows,), jnp.int32).at[dest].set(tok)
    dst_buf = jnp.zeros((rows,), jnp.int32).at[dest].set(tok + (assign % TOP_K) * t)
    blk_start = jnp.arange(nb, dtype=jnp.int32) * EXPERT_ROWS
    blk_expert = jnp.minimum(jnp.searchsorted(pad_end, blk_start, side="right"),
                             N_EXPERTS - 1).astype(jnp.int32)
    valid = jnp.clip(pad_start[blk_expert] + sizes[blk_expert] - blk_start, 0, EXPERT_ROWS)
    n_active = (pad_end[-1] // EXPERT_ROWS).reshape(1).astype(jnp.int32)
    return (blk_expert, valid.astype(jnp.int32), n_active,
            tok_buf.reshape(nb, 1, EXPERT_ROWS), dst_buf.reshape(nb, 1, EXPERT_ROWS))


def _combine_kernel(h_ref, y0_ref, y1_ref, route_ref, o_ref):
    g = route_ref[...]
    o_ref[...] = h_ref[...] + (g[:, 0:1] * y0_ref[...] + g[:, 1:2] * y1_ref[...])


def _combine(h2, y, route):
    t = h2.shape[0]
    tm = min(ROW_TILE, t)
    nt = t // tm
    row = pl.BlockSpec((tm, D_MODEL), lambda i: (i, 0))
    return pl.pallas_call(
        _combine_kernel, grid=(nt,),
        in_specs=[row, row, pl.BlockSpec((tm, D_MODEL), lambda i: (i + nt, 0)),
                  pl.BlockSpec((tm, LANES), lambda i: (i, 0))],
        out_specs=row,
        out_shape=jax.ShapeDtypeStruct((t, D_MODEL), F32),
        compiler_params=_params("parallel"), name="moe_combine",
    )(h2, y, y, route)


def _row(v):
    return v.reshape(1, -1).astype(F32)


def _tile_heads(g, n, scale=1.0):
    return _row(jnp.tile(g.astype(F32) * scale, n))


def kernel(x, mem, norm_mix, norm_mem, norm_ffn, w_mem_kv, g_mq, g_mk, fox_w_in, fox_b_f, fox_g_q, fox_g_k, fox_w_out, conv_w_in, conv_b_in, conv_dw, conv_dw_b, conv_ln_g, conv_ln_b, conv_w_out, ffn_w_gate, ffn_w_up, ffn_w_down, moe_router, moe_w_gate, moe_w_up, moe_w_down):
    b, s, d = x.shape
    mem_len = mem.shape[1]
    t = b * s
    assert d == D_MODEL
    scale = HEAD_DIM ** -0.5

    group = jnp.arange(MIX_DIM, dtype=jnp.int32) // HEAD_DIM
    ones_bd = (group[:, None] == group[None, :]).astype(BF16)
    ones_bd_mem = ones_bd[:MEM_DIM, :MEM_DIM]

    h = x.reshape(t, d)
    mem2d = mem.reshape(b * mem_len, d)
    for i in range(DEPTH):
        j = i // 2
        gmq = _tile_heads(g_mq[i], MEM_DIM // HEAD_DIM, scale)
        mk, mv = _mem_kv(mem2d, mem_len, _row(norm_mem[i]), w_mem_kv[i].astype(BF16),
                         ones_bd_mem, _tile_heads(g_mk[i], MEM_DIM // HEAD_DIM))
        if i % 2 == 0:
            w = fox_w_in[j]
            m3 = 3 * MIX_DIM
            w_cat = jnp.concatenate(
                [w[:, :m3], w[:, m3 + FOX_HEADS:], w[:, m3:m3 + FOX_HEADS],
                 jnp.zeros((d, F_LANES - FOX_HEADS), w.dtype)], axis=1).astype(BF16)
            bf = jnp.pad(fox_b_f[j].astype(F32), (0, F_LANES - FOX_HEADS)).reshape(1, F_LANES)
            q, k, v, mq, c = _fox_inproj(
                h, s, _row(norm_mix[i]), w_cat, ones_bd,
                _tile_heads(fox_g_q[j], FOX_HEADS, scale), _tile_heads(fox_g_k[j], FOX_HEADS),
                gmq, bf)
            c_t = c[:, :FOX_HEADS].reshape(b, s, FOX_HEADS // 2, 2).transpose(0, 2, 3, 1)
            mix = _fox_attn(q.reshape(b, s, MIX_DIM), k.reshape(b, s, MIX_DIM),
                            v.reshape(b, s, MIX_DIM), c_t).reshape(t, MIX_DIM)
            w_out = fox_w_out[j]
        else:
            u, mq = _conv_inproj(h, s, _row(norm_mix[i]), conv_w_in[j].astype(BF16),
                                 _row(conv_b_in[j]), ones_bd_mem, gmq)
            mix = _conv_module(u.reshape(b, s, MIX_DIM), conv_dw[j].astype(F32),
                               _row(conv_dw_b[j]), _row(conv_ln_g[j]),
                               _row(conv_ln_b[j])).reshape(t, MIX_DIM)
            w_out = conv_w_out[j]
        mem_out = _mem_attn(mq, mk, mv, s, mem_len)
        if i % 2 == 0:
            h2, z = _outproj(h, mix, mem_out, w_out.astype(BF16), _row(norm_ffn[i]))
            h = _ffn(h2, z, ffn_w_gate[j].astype(BF16), ffn_w_up[j].astype(BF16),
                     ffn_w_down[j].astype(BF16))
        else:
            wr = jnp.pad(moe_router[j].astype(F32), ((0, 0), (0, LANES - N_EXPERTS)))
            wr_hi = wr.astype(BF16)
            wr_lo = (wr - wr_hi.astype(F32)).astype(BF16)
            h2, zp, route = _outproj(h, mix, mem_out, w_out.astype(BF16), _row(norm_ffn[i]),
                                     router=(wr_hi, wr_lo))
            plan = _route_plan(route, t)
            y = _experts(zp, plan, moe_w_gate[j].astype(BF16), moe_w_up[j].astype(BF16),
                         moe_w_down[j].astype(BF16))
            h = _combine(h2, y, route)
    return h.reshape(b, s, d)
```

```python
import functools

import jax
import jax.numpy as jnp
from jax import lax
from jax.experimental import pallas as pl
from jax.experimental.pallas import tpu as pltpu

F32 = jnp.float32
BF16 = jnp.bfloat16

D_MODEL = 1024
HEAD_DIM = 64
MEM_DIM = 256
MIX_DIM = D_MODEL - MEM_DIM
FOX_HEADS = MIX_DIM // HEAD_DIM
CONV_WIDTH = 31
N_EXPERTS = 8
TOP_K = 2
DEPTH = 4
EPS = 1e-6
NEG_INF = -1e30

LANES = 128
SUBLANES = 8
VMEM_LIMIT_BYTES = 56 * 1024 * 1024

ROW_TILE = 512
ATTN_Q_TILE = 512
ATTN_KV_TILE = 512
CONV_TILE = 256
CONV_HALO = 32
CONV_ROW_CHUNK = 64
FFN_CHUNK = 256
EXPERT_ROWS = 1024
EXPERT_SUB = 256
EXPERT_FF_TILE = 1792
F_LANES = LANES


def _params(*sem):
    return pltpu.CompilerParams(dimension_semantics=sem, vmem_limit_bytes=VMEM_LIMIT_BYTES)


def _resident(shape):
    nd = len(shape)
    return pl.BlockSpec(shape, lambda *_: (0,) * nd, pipeline_mode=pl.Buffered(1))


def _rms(x, g):
    ms = jnp.mean(x * x, axis=-1, keepdims=True)
    return x * lax.rsqrt(ms + EPS) * g


def _head_rms(x, ones_bd, g):
    ss = jnp.dot((x * x).astype(BF16), ones_bd, preferred_element_type=F32)
    return x * lax.rsqrt(ss * (1.0 / HEAD_DIM) + EPS) * g


def _silu(x):
    return x * (1.0 / (1.0 + jnp.exp(-x)))


def _log_sigmoid(x):
    return jnp.minimum(x, 0.0) - jnp.log(1.0 + jnp.exp(-jnp.abs(x)))


def _row_prefix_sum(x):
    n = x.shape[0]
    row = lax.broadcasted_iota(jnp.int32, x.shape, 0)
    s = 1
    while s < n:
        x = x + jnp.where(row >= s, pltpu.roll(x, s, 0), 0.0)
        s *= 2
    return x


def _fox_inproj_kernel(tiles_per_seq, h_ref, gn_ref, w_ref, bd_ref, gq_ref, gk_ref, gmq_ref,
                       bf_ref, q_ref, k_ref, v_ref, mq_ref, c_ref, carry_ref):
    xb = _rms(h_ref[...], gn_ref[...]).astype(BF16)
    m = MIX_DIM
    q = jnp.dot(xb, w_ref[:, 0:m], preferred_element_type=F32)
    q_ref[...] = _head_rms(q, bd_ref[...], gq_ref[...]).astype(BF16)
    k = jnp.dot(xb, w_ref[:, m:2 * m], preferred_element_type=F32)
    k_ref[...] = _head_rms(k, bd_ref[...], gk_ref[...]).astype(BF16)
    v_ref[...] = jnp.dot(xb, w_ref[:, 2 * m:3 * m], preferred_element_type=F32).astype(BF16)
    mq = jnp.dot(xb, w_ref[:, 3 * m:3 * m + MEM_DIM], preferred_element_type=F32)
    mq_ref[...] = _head_rms(mq, bd_ref[0:MEM_DIM, 0:MEM_DIM], gmq_ref[...]).astype(BF16)

    f = jnp.dot(xb, w_ref[:, 3 * m + MEM_DIM:], preferred_element_type=F32) + bf_ref[...]
    lane = lax.broadcasted_iota(jnp.int32, f.shape, 1)
    lf = jnp.where(lane < FOX_HEADS, _log_sigmoid(f), 0.0)

    @pl.when(pl.program_id(0) % tiles_per_seq == 0)
    def _():
        carry_ref[...] = jnp.zeros_like(carry_ref)

    c = _row_prefix_sum(lf) + carry_ref[0:1, :]
    c_ref[...] = c
    carry_ref[0:1, :] = c[-1:, :]


def _fox_inproj(h, seq_len, gn, w, bd, gq, gk, gmq, bf):
    t = h.shape[0]
    tm = min(ROW_TILE, seq_len)
    assert seq_len % tm == 0
    row = lambda n: pl.BlockSpec((tm, n), lambda i: (i, 0))
    return pl.pallas_call(
        functools.partial(_fox_inproj_kernel, seq_len // tm),
        grid=(t // tm,),
        in_specs=[row(D_MODEL), _resident(gn.shape), _resident(w.shape), _resident(bd.shape),
                  _resident(gq.shape), _resident(gk.shape), _resident(gmq.shape),
                  _resident(bf.shape)],
        out_specs=[row(MIX_DIM), row(MIX_DIM), row(MIX_DIM), row(MEM_DIM), row(F_LANES)],
        out_shape=[jax.ShapeDtypeStruct((t, MIX_DIM), BF16)] * 3
        + [jax.ShapeDtypeStruct((t, MEM_DIM), BF16), jax.ShapeDtypeStruct((t, F_LANES), F32)],
        scratch_shapes=[pltpu.VMEM((SUBLANES, F_LANES), F32)],
        compiler_params=_params("arbitrary"),
        name="fox_inproj",
    )(h, gn, w, bd, gq, gk, gmq, bf)


def _conv_inproj_kernel(h_ref, gn_ref, w_ref, b_ref, bd_ref, gmq_ref, u_ref, mq_ref):
    xb = _rms(h_ref[...], gn_ref[...]).astype(BF16)
    m = MIX_DIM
    a = jnp.dot(xb, w_ref[:, 0:m], preferred_element_type=F32) + b_ref[:, 0:m]
    g = jnp.dot(xb, w_ref[:, m:2 * m], preferred_element_type=F32) + b_ref[:, m:2 * m]
    u_ref[...] = a * (1.0 / (1.0 + jnp.exp(-g)))
    mq = jnp.dot(xb, w_ref[:, 2 * m:], preferred_element_type=F32)
    mq_ref[...] = _head_rms(mq, bd_ref[...], gmq_ref[...]).astype(BF16)


def _conv_inproj(h, seq_len, gn, w, b, bd, gmq):
    t = h.shape[0]
    tm = min(ROW_TILE, seq_len)
    assert t % tm == 0
    row = lambda n: pl.BlockSpec((tm, n), lambda i: (i, 0))
    return pl.pallas_call(
        _conv_inproj_kernel,
        grid=(t // tm,),
        in_specs=[row(D_MODEL), _resident(gn.shape), _resident(w.shape), _resident(b.shape),
                  _resident(bd.shape), _resident(gmq.shape)],
        out_specs=[row(MIX_DIM), row(MEM_DIM)],
        out_shape=[jax.ShapeDtypeStruct((t, MIX_DIM), F32),
                   jax.ShapeDtypeStruct((t, MEM_DIM), BF16)],
        compiler_params=_params("parallel"),
        name="conv_inproj",
    )(h, gn, w, b, bd, gmq)


def _mem_kv_kernel(m_ref, gn_ref, w_ref, bd_ref, gk_ref, mk_ref, mv_ref):
    xb = _rms(m_ref[...], gn_ref[...]).astype(BF16)
    mk = jnp.dot(xb, w_ref[:, 0:MEM_DIM], preferred_element_type=F32)
    mk_ref[...] = _head_rms(mk, bd_ref[...], gk_ref[...]).astype(BF16)
    mv_ref[...] = jnp.dot(xb, w_ref[:, MEM_DIM:], preferred_element_type=F32).astype(BF16)


def _mem_kv(mem2d, mem_len, gn, w, bd, gk):
    rows = mem2d.shape[0]
    row = lambda n: pl.BlockSpec((mem_len, n), lambda i: (i, 0))
    return pl.pallas_call(
        _mem_kv_kernel,
        grid=(rows // mem_len,),
        in_specs=[row(D_MODEL), _resident(gn.shape), _resident(w.shape), _resident(bd.shape),
                  _resident(gk.shape)],
        out_specs=[row(MEM_DIM), row(MEM_DIM)],
        out_shape=[jax.ShapeDtypeStruct((rows, MEM_DIM), BF16)] * 2,
        compiler_params=_params("parallel"),
        name="mem_kv",
    )(mem2d, gn, w, bd, gk)


def _head_lane_mask(shape, head_in_pair):
    lane = lax.broadcasted_iota(jnp.int32, shape, len(shape) - 1)
    return (lane // HEAD_DIM) == head_in_pair


def _fox_attn_kernel(tq, tk, q_ref, k_ref, v_ref, c_ref, o_ref, m_ref, l_ref, acc_ref):
    i = pl.program_id(2)
    q2 = q_ref[0]
    zero = jnp.zeros_like(q2)
    q_stack = jnp.concatenate([jnp.where(_head_lane_mask(q2.shape, hh), q2, zero)
                               for hh in range(2)], axis=0)
    m_ref[...] = jnp.full(m_ref.shape, NEG_INF, F32)
    l_ref[...] = jnp.zeros(l_ref.shape, F32)
    acc_ref[...] = jnp.zeros(acc_ref.shape, F32)
    n_full = (i * tq) // tk
    q_pos = i * tq + lax.broadcasted_iota(jnp.int32, (tq, tk), 0)
    k_off = lax.broadcasted_iota(jnp.int32, (tq, tk), 1)

    def block(j, _, masked):
        start = pl.multiple_of(j * tk, tk)
        kb = k_ref[0, pl.ds(start, tk), :]
        vb = v_ref[0, pl.ds(start, tk), :]
        s2 = lax.dot_general(q_stack, kb, (((1,), (1,)), ((), ())), preferred_element_type=F32)
        for hh in range(2):
            s = s2[hh * tq:(hh + 1) * tq] - c_ref[0, 0, hh:hh + 1, pl.ds(start, tk)]
            if masked:
                s = jnp.where(start + k_off <= q_pos, s, NEG_INF)
            m_old = m_ref[hh]
            m_new = jnp.maximum(m_old, jnp.max(s, axis=-1, keepdims=True))
            alpha = jnp.exp(m_old - m_new)
            ps = [jnp.exp(s[:, t0:t0 + LANES] - m_new) for t0 in range(0, tk, LANES)]
            l_ref[hh] = alpha * l_ref[hh] + functools.reduce(lambda a, b: a + b, ps)
            p = jnp.concatenate(ps, axis=1).astype(BF16)
            acc_ref[hh] = alpha * acc_ref[hh] + jnp.dot(p, vb, preferred_element_type=F32)
            m_ref[hh] = m_new
        return 0

    lax.fori_loop(0, n_full, functools.partial(block, masked=False), 0)
    lax.fori_loop(n_full, n_full + tq // tk, functools.partial(block, masked=True), 0)
    outs = [acc_ref[hh] * (1.0 / jnp.sum(l_ref[hh], axis=-1, keepdims=True)) for hh in range(2)]
    o_ref[0] = jnp.where(_head_lane_mask(outs[0].shape, 0), outs[0], outs[1]).astype(o_ref.dtype)


def _fox_attn(q, k, v, c_t):
    b, s, _ = q.shape
    tq = min(ATTN_Q_TILE, s)
    tk = min(ATTN_KV_TILE, tq)
    assert s % tq == 0 and tq % tk == 0
    seq = pl.BlockSpec((1, s, LANES), lambda bi, hp, i: (bi, 0, hp))
    return pl.pallas_call(
        functools.partial(_fox_attn_kernel, tq, tk),
        grid=(b, FOX_HEADS // 2, s // tq),
        in_specs=[pl.BlockSpec((1, tq, LANES), lambda bi, hp, i: (bi, i, hp)), seq, seq,
                  pl.BlockSpec((1, 1, 2, s), lambda bi, hp, i: (bi, hp, 0, 0))],
        out_specs=pl.BlockSpec((1, tq, LANES), lambda bi, hp, i: (bi, i, hp)),
        out_shape=jax.ShapeDtypeStruct((b, s, MIX_DIM), BF16),
        scratch_shapes=[pltpu.VMEM((2, tq, LANES), F32)] * 3,
        compiler_params=_params("parallel", "parallel", "arbitrary"),
        name="fox_attn",
    )(q, k, v, c_t)


def _mem_attn_kernel(q_ref, k_ref, v_ref, o_ref):
    for pair in range(MEM_DIM // LANES):
        lanes = slice(pair * LANES, (pair + 1) * LANES)
        q2 = q_ref[:, lanes]
        k2 = k_ref[:, lanes]
        v2 = v_ref[:, lanes]
        outs = []
        for hh in range(2):
            qh = jnp.where(_head_lane_mask(q2.shape, hh), q2, jnp.zeros_like(q2))
            s = lax.dot_general(qh, k2, (((1,), (1,)), ((), ())), preferred_element_type=F32)
            p = jnp.exp(s - jnp.max(s, axis=-1, keepdims=True))
            l = jnp.sum(p, axis=-1, keepdims=True)
            outs.append(jnp.dot(p.astype(BF16), v2, preferred_element_type=F32) * (1.0 / l))
        o_ref[:, lanes] = jnp.where(_head_lane_mask(outs[0].shape, 0), outs[0],
                                    outs[1]).astype(o_ref.dtype)


def _mem_attn(mq, mk, mv, seq_len, mem_len):
    t = mq.shape[0]
    tm = min(ROW_TILE, seq_len)
    per_seq = seq_len // tm
    kv = pl.BlockSpec((mem_len, MEM_DIM), lambda i: (i // per_seq, 0))
    return pl.pallas_call(
        _mem_attn_kernel,
        grid=(t // tm,),
        in_specs=[pl.BlockSpec((tm, MEM_DIM), lambda i: (i, 0)), kv, kv],
        out_specs=pl.BlockSpec((tm, MEM_DIM), lambda i: (i, 0)),
        out_shape=jax.ShapeDtypeStruct((t, MEM_DIM), BF16),
        compiler_params=_params("parallel"),
        name="mem_attn",
    )(mq, mk, mv)


def _conv_kernel(ts, u_ref, dw_ref, dwb_ref, lng_ref, lnb_ref, o_ref, ext_ref, acc_ref):
    @pl.when(pl.program_id(1) == 0)
    def _():
        ext_ref[0:CONV_HALO, :] = jnp.zeros((CONV_HALO, MIX_DIM), F32)

    @pl.when(pl.program_id(1) > 0)
    def _():
        ext_ref[0:CONV_HALO, :] = ext_ref[ts:ts + CONV_HALO, :]

    ext_ref[CONV_HALO:CONV_HALO + ts, :] = u_ref[0]

    base = CONV_HALO - (CONV_WIDTH - 1)
    n_shift = SUBLANES
    rc = min(CONV_ROW_CHUNK, ts)
    for r0 in range(0, ts, rc):
        for l0 in range(0, MIX_DIM, LANES):
            acc = jnp.zeros((rc, LANES), F32) + dwb_ref[:, l0:l0 + LANES]
            for b in range(n_shift):
                n_a = (CONV_WIDTH - 1 - b) // n_shift + 1
                win = ext_ref[r0 + base + b:r0 + base + b + rc + (n_a - 1) * n_shift,
                              l0:l0 + LANES]
                for a in range(n_a):
                    j = a * n_shift + b
                    acc = acc + dw_ref[j:j + 1, l0:l0 + LANES] * win[a * n_shift:
                                                                      a * n_shift + rc, :]
            acc_ref[r0:r0 + rc, l0:l0 + LANES] = acc

    y = acc_ref[...]
    mu = jnp.mean(y, axis=-1, keepdims=True)
    yc = y - mu
    var = jnp.mean(yc * yc, axis=-1, keepdims=True)
    z = yc * lax.rsqrt(var + EPS) * lng_ref[...] + lnb_ref[...]
    o_ref[0] = _silu(z).astype(o_ref.dtype)


def _conv_module(u, dw, dwb, lng, lnb):
    b, s, _ = u.shape
    ts = min(CONV_TILE, s)
    assert s % ts == 0 and ts >= CONV_HALO
    blk = pl.BlockSpec((1, ts, MIX_DIM), lambda bi, i: (bi, i, 0))
    return pl.pallas_call(
        functools.partial(_conv_kernel, ts),
        grid=(b, s // ts),
        in_specs=[blk, _resident(dw.shape), _resident(dwb.shape), _resident(lng.shape),
                  _resident(lnb.shape)],
        out_specs=blk,
        out_shape=jax.ShapeDtypeStruct((b, s, MIX_DIM), BF16),
        scratch_shapes=[pltpu.VMEM((ts + CONV_HALO, MIX_DIM), F32),
                        pltpu.VMEM((ts, MIX_DIM), F32)],
        compiler_params=_params("parallel", "arbitrary"),
        name="conv_module",
    )(u, dw, dwb, lng, lnb)


def _outproj_math(h_ref, mix_ref, mem_ref, w_ref, gn_ref):
    h2 = (h_ref[...]
          + jnp.dot(mix_ref[...], w_ref[0:MIX_DIM, :], preferred_element_type=F32)
          + jnp.dot(mem_ref[...], w_ref[MIX_DIM:, :], preferred_element_type=F32))
    return h2, _rms(h2, gn_ref[...])


def _outproj_dense_kernel(h_ref, mix_ref, mem_ref, w_ref, gn_ref, h2_ref, z_ref):
    h2, z = _outproj_math(h_ref, mix_ref, mem_ref, w_ref, gn_ref)
    h2_ref[...] = h2
    z_ref[...] = z.astype(BF16)


def _split_bf16(x):
    hi = x.astype(BF16)
    return hi, (x - hi.astype(F32)).astype(BF16)


def _outproj_router_kernel(h_ref, mix_ref, mem_ref, w_ref, gn_ref, wr_hi_ref, wr_lo_ref,
                           h2_ref, zp_ref, route_ref):
    h2, z = _outproj_math(h_ref, mix_ref, mem_ref, w_ref, gn_ref)
    h2_ref[...] = h2

    half = D_MODEL // 2
    lo_bits = pltpu.bitcast(z[:, :half].astype(BF16).astype(F32), jnp.uint32) >> 16
    hi_bits = pltpu.bitcast(z[:, half:].astype(BF16).astype(F32), jnp.uint32) & jnp.uint32(0xFFFF0000)
    zp_ref[...] = lo_bits | hi_bits

    z_hi, z_lo = _split_bf16(z)
    logits = (jnp.dot(z_hi, wr_hi_ref[...], preferred_element_type=F32)
              + jnp.dot(z_lo, wr_hi_ref[...], preferred_element_type=F32)
              + jnp.dot(z_hi, wr_lo_ref[...], preferred_element_type=F32))
    lane = lax.broadcasted_iota(jnp.int32, logits.shape, 1)
    lane_f = lane.astype(F32)
    logits = jnp.where(lane < N_EXPERTS, logits, -jnp.inf)
    l1 = jnp.max(logits, axis=-1, keepdims=True)
    e1 = jnp.min(jnp.where(logits == l1, lane_f, float(LANES)), axis=-1, keepdims=True)
    rest = jnp.where(lane_f == e1, -jnp.inf, logits)
    l2 = jnp.max(rest, axis=-1, keepdims=True)
    e2 = jnp.min(jnp.where(rest == l2, lane_f, float(LANES)), axis=-1, keepdims=True)
    g2 = 1.0 / (1.0 + jnp.exp(l1 - l2))
    g1 = 1.0 - g2
    route = jnp.where(lane == 0, g1, 0.0) + jnp.where(lane == 1, g2, 0.0)
    route = route + jnp.where(lane == 2, e1, 0.0)
    route_ref[...] = route + jnp.where(lane == 3, e2, 0.0)


def _outproj(h, mix, mem, w, gn, router=None):
    t = h.shape[0]
    tm = min(ROW_TILE, t)
    assert t % tm == 0
    row = lambda n: pl.BlockSpec((tm, n), lambda i: (i, 0))
    in_specs = [row(D_MODEL), row(MIX_DIM), row(MEM_DIM), _resident(w.shape), _resident(gn.shape)]
    if router is None:
        return pl.pallas_call(
            _outproj_dense_kernel, grid=(t // tm,), in_specs=in_specs,
            out_specs=[row(D_MODEL), row(D_MODEL)],
            out_shape=[jax.ShapeDtypeStruct((t, D_MODEL), F32),
                       jax.ShapeDtypeStruct((t, D_MODEL), BF16)],
            compiler_params=_params("parallel"), name="outproj_dense",
        )(h, mix, mem, w, gn)
    wr_hi, wr_lo = router
    return pl.pallas_call(
        _outproj_router_kernel, grid=(t // tm,),
        in_specs=in_specs + [_resident(wr_hi.shape), _resident(wr_lo.shape)],
        out_specs=[row(D_MODEL), row(D_MODEL // 2), row(LANES)],
        out_shape=[jax.ShapeDtypeStruct((t, D_MODEL), F32),
                   jax.ShapeDtypeStruct((t, D_MODEL // 2), jnp.uint32),
                   jax.ShapeDtypeStruct((t, LANES), F32)],
        compiler_params=_params("parallel"), name="outproj_router",
    )(h, mix, mem, w, gn, wr_hi, wr_lo)


def _ffn_kernel(h_ref, z_ref, wg_ref, wu_ref, wd_ref, o_ref, acc_ref):
    z = z_ref[...]
    d_ff = wg_ref.shape[1]
    acc_ref[...] = h_ref[...]
    for f0 in range(0, d_ff, FFN_CHUNK):
        g = jnp.dot(z, wg_ref[:, f0:f0 + FFN_CHUNK], preferred_element_type=F32)
        u = jnp.dot(z, wu_ref[:, f0:f0 + FFN_CHUNK], preferred_element_type=F32)
        a = (_silu(g) * u).astype(BF16)
        acc_ref[...] += jnp.dot(a, wd_ref[f0:f0 + FFN_CHUNK, :], preferred_element_type=F32)
    o_ref[...] = acc_ref[...]


def _ffn(h2, z, wg, wu, wd):
    t = h2.shape[0]
    tm = min(ROW_TILE, t)
    assert t % tm == 0 and wg.shape[1] % FFN_CHUNK == 0
    row = pl.BlockSpec((tm, D_MODEL), lambda i: (i, 0))
    return pl.pallas_call(
        _ffn_kernel, grid=(t // tm,),
        in_specs=[row, row, _resident(wg.shape), _resident(wu.shape), _resident(wd.shape)],
        out_specs=row,
        out_shape=jax.ShapeDtypeStruct((t, D_MODEL), F32),
        scratch_shapes=[pltpu.VMEM((tm, D_MODEL), F32)],
        compiler_params=_params("parallel"), name="ffn_dense",
    )(h2, z, wg, wu, wd)


def _row_copy(src, src_row, dst, dst_row, sem):
    return pltpu.make_async_copy(src.at[pl.ds(src_row, 1), :], dst.at[pl.ds(dst_row, 1), :], sem)


def _block_copy(src, dst, sem):
    return pltpu.make_async_copy(src.at[pl.ds(0, EXPERT_ROWS), :], dst.at[pl.ds(0, EXPERT_ROWS), :],
                                 sem)


def _experts_kernel(n_f, be_ref, nact_ref, tok_cur_ref, tok_next_ref, dst_prev_ref, dst_cur_ref,
                    zp_hbm, wg_ref, wu_ref, wd_ref, y_hbm, xp_ref, x_ref, acc_ref, stage_ref,
                    sem_in, sem_out):
    del be_ref
    i = pl.program_id(0)
    f = pl.program_id(1)
    n_active = nact_ref[0]
    active = i < n_active
    has_next = i + 1 < n_active
    has_prev = i >= 1
    first_step = f == 0
    last_step = f == n_f - 1
    half = D_MODEL // 2

    def gather_next(r):
        _row_copy(zp_hbm, tok_next_ref[0, 0, r], xp_ref, r, sem_in).start()

    def scatter_prev(r):
        _row_copy(stage_ref, r, y_hbm, dst_prev_ref[0, 0, r], sem_out).start()

    def rolled(fn):
        def body(r, _):
            fn(r)
            return 0
        lax.fori_loop(0, EXPERT_ROWS, body, 0)

    def compute(phase, inline_dma=None):
        for r0 in range(0, EXPERT_ROWS, EXPERT_SUB):
            rows = slice(r0, r0 + EXPERT_SUB)
            x = x_ref[rows, :]
            g = jnp.dot(x, wg_ref[0, 0], preferred_element_type=F32)
            u = jnp.dot(x, wu_ref[0, 0], preferred_element_type=F32)
            a = (_silu(g) * u).astype(BF16)
            c = jnp.dot(a, wd_ref[0, 0], preferred_element_type=F32)
            if phase == "first":
                acc_ref[rows, :] = c
            elif phase == "mid":
                acc_ref[rows, :] += c
            else:
                stage_ref[rows, :] = acc_ref[rows, :] + c
            if inline_dma is not None:
                for r in range(r0, r0 + EXPERT_SUB):
                    inline_dma(r)

    @pl.when((i == 0) & first_step)
    def _():
        stage_ref[...] = jnp.zeros(stage_ref.shape, F32)
        sink = pltpu.make_async_copy(
            stage_ref, y_hbm.at[pl.ds(y_hbm.shape[0] - EXPERT_ROWS, EXPERT_ROWS), :], sem_out)
        sink.start()
        sink.wait()

        def first(r):
            _row_copy(zp_hbm, tok_cur_ref[0, 0, r], xp_ref, r, sem_in).start()
        rolled(first)

    @pl.when(active & first_step)
    def _():
        _block_copy(zp_hbm, xp_ref, sem_in).wait()
        xp = xp_ref[...]
        x_ref[:, :half] = pltpu.bitcast(xp << 16, F32).astype(BF16)
        x_ref[:, half:] = pltpu.bitcast(xp & jnp.uint32(0xFFFF0000), F32).astype(BF16)

    @pl.when(active & first_step & has_prev)
    def _():
        compute("first", scatter_prev)

    @pl.when(active & first_step & jnp.logical_not(has_prev))
    def _():
        compute("first")

    if n_f > 2:
        @pl.when(active & jnp.logical_not(first_step) & jnp.logical_not(last_step))
        def _():
            compute("mid")

    @pl.when(active & last_step & has_prev)
    def _():
        _block_copy(stage_ref, y_hbm, sem_out).wait()

    @pl.when(active & last_step & has_next)
    def _():
        compute("last", gather_next)

    @pl.when(active & last_step & jnp.logical_not(has_next))
    def _():
        compute("last")

        def last(r):
            _row_copy(stage_ref, r, y_hbm, dst_cur_ref[0, 0, r], sem_out).start()
        rolled(last)
        _block_copy(stage_ref, y_hbm, sem_out).wait()


def _experts(zp, plan, layer, wg, wu, wd):
    t = zp.shape[0]
    blk_expert, n_active, tok_buf, dst_buf = plan
    nb = blk_expert.shape[0]
    d_ff = wg.shape[3]
    tf = EXPERT_FF_TILE
    n_f = d_ff // tf
    assert d_ff % tf == 0 and n_f >= 2
    last = nb - 1

    def idx(shift):
        return pl.BlockSpec((1, 1, EXPERT_ROWS),
                            lambda i, f, *_: (jnp.clip(i + shift, 0, last), 0, 0),
                            memory_space=pltpu.SMEM)

    grid_spec = pltpu.PrefetchScalarGridSpec(
        num_scalar_prefetch=2,
        grid=(nb, n_f),
        in_specs=[idx(0), idx(1), idx(-1), idx(0), pl.BlockSpec(memory_space=pl.ANY),
                  pl.BlockSpec((1, 1, D_MODEL, tf), lambda i, f, be, *_: (layer, be[i], 0, f)),
                  pl.BlockSpec((1, 1, D_MODEL, tf), lambda i, f, be, *_: (layer, be[i], 0, f)),
                  pl.BlockSpec((1, 1, tf, D_MODEL), lambda i, f, be, *_: (layer, be[i], f, 0))],
        out_specs=pl.BlockSpec(memory_space=pl.ANY),
        scratch_shapes=[pltpu.VMEM((EXPERT_ROWS, D_MODEL // 2), jnp.uint32),
                        pltpu.VMEM((EXPERT_ROWS, D_MODEL), BF16),
                        pltpu.VMEM((EXPERT_ROWS, D_MODEL), F32),
                        pltpu.VMEM((EXPERT_ROWS, D_MODEL), F32),
                        pltpu.SemaphoreType.DMA(()), pltpu.SemaphoreType.DMA(())],
    )
    return pl.pallas_call(
        functools.partial(_experts_kernel, n_f), grid_spec=grid_spec,
        out_shape=jax.ShapeDtypeStruct((TOP_K * t + EXPERT_ROWS, D_MODEL), F32),
        compiler_params=_params("arbitrary", "arbitrary"), name="moe_experts",
    )(blk_expert, n_active, tok_buf, tok_buf, dst_buf, dst_buf, zp, wg, wu, wd)


def _route_plan(route, t):
    a = t * TOP_K
    nb = a // EXPERT_ROWS + N_EXPERTS
    rows = nb * EXPERT_ROWS
    flat_e = route[:, 2:4].astype(jnp.int32).reshape(a)
    experts = jnp.arange(N_EXPERTS, dtype=jnp.int32)
    sizes = jnp.sum((flat_e[:, None] == experts[None, :]).astype(jnp.int32), axis=0)
    start = jnp.cumsum(sizes) - sizes
    padded = ((sizes + EXPERT_ROWS - 1) // EXPERT_ROWS) * EXPERT_ROWS
    pad_end = jnp.cumsum(padded)
    pad_start = pad_end - padded
    sorted_assign = jnp.sort(flat_e * a + jnp.arange(a, dtype=jnp.int32)) % a
    blk_start = jnp.arange(nb, dtype=jnp.int32) * EXPERT_ROWS
    blk_expert = jnp.minimum(jnp.searchsorted(pad_end, blk_start, side="right"),
                             N_EXPERTS - 1).astype(jnp.int32)
    row = jnp.arange(rows, dtype=jnp.int32)
    row_e = jnp.repeat(blk_expert, EXPERT_ROWS)
    rank = row - pad_start[row_e]
    real = (rank >= 0) & (rank < sizes[row_e])
    assign = sorted_assign[jnp.clip(start[row_e] + rank, 0, a - 1)]
    tok = assign // TOP_K
    tok_buf = jnp.where(real, tok, 0)
    dst_buf = jnp.where(real, tok + (assign % TOP_K) * t, TOP_K * t + row % EXPERT_ROWS)
    n_active = (pad_end[-1] // EXPERT_ROWS).reshape(1).astype(jnp.int32)
    return (blk_expert, n_active, tok_buf.reshape(nb, 1, EXPERT_ROWS),
            dst_buf.reshape(nb, 1, EXPERT_ROWS))


def _combine_kernel(h_ref, y0_ref, y1_ref, route_ref, o_ref):
    g = route_ref[...]
    o_ref[...] = h_ref[...] + (g[:, 0:1] * y0_ref[...] + g[:, 1:2] * y1_ref[...])


def _combine(h2, y, route):
    t = h2.shape[0]
    tm = min(ROW_TILE, t)
    nt = t // tm
    row = pl.BlockSpec((tm, D_MODEL), lambda i: (i, 0))
    return pl.pallas_call(
        _combine_kernel, grid=(nt,),
        in_specs=[row, row, pl.BlockSpec((tm, D_MODEL), lambda i: (i + nt, 0)),
                  pl.BlockSpec((tm, LANES), lambda i: (i, 0))],
        out_specs=row,
        out_shape=jax.ShapeDtypeStruct((t, D_MODEL), F32),
        compiler_params=_params("parallel"), name="moe_combine",
    )(h2, y, y, route)


def _row(v):
    return v.reshape(1, -1).astype(F32)


def _tile_heads(g, n, scale=1.0):
    return _row(jnp.tile(g.astype(F32) * scale, n))


def kernel(x, mem, norm_mix, norm_mem, norm_ffn, w_mem_kv, g_mq, g_mk, fox_w_in, fox_b_f, fox_g_q, fox_g_k, fox_w_out, conv_w_in, conv_b_in, conv_dw, conv_dw_b, conv_ln_g, conv_ln_b, conv_w_out, ffn_w_gate, ffn_w_up, ffn_w_down, moe_router, moe_w_gate, moe_w_up, moe_w_down):
    b, s, d = x.shape
    mem_len = mem.shape[1]
    t = b * s
    assert d == D_MODEL
    scale = HEAD_DIM ** -0.5

    group = jnp.arange(MIX_DIM, dtype=jnp.int32) // HEAD_DIM
    ones_bd = (group[:, None] == group[None, :]).astype(BF16)
    ones_bd_mem = ones_bd[:MEM_DIM, :MEM_DIM]

    moe_wg, moe_wu, moe_wd = (w.astype(BF16) for w in (moe_w_gate, moe_w_up, moe_w_down))
    h = x.reshape(t, d)
    mem2d = mem.reshape(b * mem_len, d)
    for i in range(DEPTH):
        j = i // 2
        gmq = _tile_heads(g_mq[i], MEM_DIM // HEAD_DIM, scale)
        mk, mv = _mem_kv(mem2d, mem_len, _row(norm_mem[i]), w_mem_kv[i].astype(BF16),
                         ones_bd_mem, _tile_heads(g_mk[i], MEM_DIM // HEAD_DIM))
        if i % 2 == 0:
            w = fox_w_in[j]
            m3 = 3 * MIX_DIM
            w_cat = jnp.concatenate(
                [w[:, :m3], w[:, m3 + FOX_HEADS:], w[:, m3:m3 + FOX_HEADS],
                 jnp.zeros((d, F_LANES - FOX_HEADS), w.dtype)], axis=1).astype(BF16)
            bf = jnp.pad(fox_b_f[j].astype(F32), (0, F_LANES - FOX_HEADS)).reshape(1, F_LANES)
            q, k, v, mq, c = _fox_inproj(
                h, s, _row(norm_mix[i]), w_cat, ones_bd,
                _tile_heads(fox_g_q[j], FOX_HEADS, scale), _tile_heads(fox_g_k[j], FOX_HEADS),
                gmq, bf)
            c_t = c[:, :FOX_HEADS].reshape(b, s, FOX_HEADS // 2, 2).transpose(0, 2, 3, 1)
            mix = _fox_attn(q.reshape(b, s, MIX_DIM), k.reshape(b, s, MIX_DIM),
                            v.reshape(b, s, MIX_DIM), c_t).reshape(t, MIX_DIM)
            w_out = fox_w_out[j]
        else:
            u, mq = _conv_inproj(h, s, _row(norm_mix[i]), conv_w_in[j].astype(BF16),
                                 _row(conv_b_in[j]), ones_bd_mem, gmq)
            mix = _conv_module(u.reshape(b, s, MIX_DIM), conv_dw[j].astype(F32),
                               _row(conv_dw_b[j]), _row(conv_ln_g[j]),
                               _row(conv_ln_b[j])).reshape(t, MIX_DIM)
            w_out = conv_w_out[j]
        mem_out = _mem_attn(mq, mk, mv, s, mem_len)
        if i % 2 == 0:
            h2, z = _outproj(h, mix, mem_out, w_out.astype(BF16), _row(norm_ffn[i]))
            h = _ffn(h2, z, ffn_w_gate[j].astype(BF16), ffn_w_up[j].astype(BF16),
                     ffn_w_down[j].astype(BF16))
        else:
            wr = jnp.pad(moe_router[j].astype(F32), ((0, 0), (0, LANES - N_EXPERTS)))
            wr_hi = wr.astype(BF16)
            wr_lo = (wr - wr_hi.astype(F32)).astype(BF16)
            h2, zp, route = _outproj(h, mix, mem_out, w_out.astype(BF16), _row(norm_ffn[i]),
                                     router=(wr_hi, wr_lo))
            plan = _route_plan(route, t)
            y = _experts(zp, plan, j, moe_wg, moe_wu, moe_wd)
            h = _combine(h2, y, route)
    return h.reshape(b, s, d)
```

```python
import functools

import jax
import jax.numpy as jnp
from jax import lax
from jax.experimental import pallas as pl
from jax.experimental.pallas import tpu as pltpu

F32 = jnp.float32
BF16 = jnp.bfloat16

D_MODEL = 1024
HEAD_DIM = 64
MEM_DIM = 256
MIX_DIM = D_MODEL - MEM_DIM
FOX_HEADS = MIX_DIM // HEAD_DIM
CONV_WIDTH = 31
N_EXPERTS = 8
TOP_K = 2
DEPTH = 4
EPS = 1e-6
NEG_INF = -1e30

LANES = 128
SUBLANES = 8
VMEM_LIMIT_BYTES = 56 * 1024 * 1024

ROW_TILE = 512
ATTN_Q_TILE = 512
ATTN_KV_TILE = 512
CONV_TILE = 256
CONV_HALO = 32
CONV_ROW_CHUNK = 64
FFN_CHUNK = 256
EXPERT_ROWS = 1024
EXPERT_SUB = 256
EXPERT_FF_TILE = 1792
F_LANES = LANES
PACKED_SLABS = D_MODEL // 2 // LANES
OUT_SLABS = D_MODEL // LANES


def _params(*sem):
    return pltpu.CompilerParams(dimension_semantics=sem, vmem_limit_bytes=VMEM_LIMIT_BYTES)


def _resident(shape):
    nd = len(shape)
    return pl.BlockSpec(shape, lambda *_: (0,) * nd, pipeline_mode=pl.Buffered(1))


def _rms(x, g):
    ms = jnp.mean(x * x, axis=-1, keepdims=True)
    return x * lax.rsqrt(ms + EPS) * g


def _head_rms(x, ones_bd, g):
    ss = jnp.dot((x * x).astype(BF16), ones_bd, preferred_element_type=F32)
    return x * lax.rsqrt(ss * (1.0 / HEAD_DIM) + EPS) * g


def _silu(x):
    return x * (1.0 / (1.0 + jnp.exp(-x)))


def _log_sigmoid(x):
    return jnp.minimum(x, 0.0) - jnp.log(1.0 + jnp.exp(-jnp.abs(x)))


def _row_prefix_sum(x):
    n = x.shape[0]
    row = lax.broadcasted_iota(jnp.int32, x.shape, 0)
    s = 1
    while s < n:
        x = x + jnp.where(row >= s, pltpu.roll(x, s, 0), 0.0)
        s *= 2
    return x


def _fox_inproj_kernel(tiles_per_seq, h_ref, gn_ref, w_ref, bd_ref, gq_ref, gk_ref, gmq_ref,
                       bf_ref, q_ref, k_ref, v_ref, mq_ref, c_ref, carry_ref):
    xb = _rms(h_ref[...], gn_ref[...]).astype(BF16)
    m = MIX_DIM
    q = jnp.dot(xb, w_ref[:, 0:m], preferred_element_type=F32)
    q_ref[...] = _head_rms(q, bd_ref[...], gq_ref[...]).astype(BF16)
    k = jnp.dot(xb, w_ref[:, m:2 * m], preferred_element_type=F32)
    k_ref[...] = _head_rms(k, bd_ref[...], gk_ref[...]).astype(BF16)
    v_ref[...] = jnp.dot(xb, w_ref[:, 2 * m:3 * m], preferred_element_type=F32).astype(BF16)
    mq = jnp.dot(xb, w_ref[:, 3 * m:3 * m + MEM_DIM], preferred_element_type=F32)
    mq_ref[...] = _head_rms(mq, bd_ref[0:MEM_DIM, 0:MEM_DIM], gmq_ref[...]).astype(BF16)

    f = jnp.dot(xb, w_ref[:, 3 * m + MEM_DIM:], preferred_element_type=F32) + bf_ref[...]
    lane = lax.broadcasted_iota(jnp.int32, f.shape, 1)
    lf = jnp.where(lane < FOX_HEADS, _log_sigmoid(f), 0.0)

    @pl.when(pl.program_id(0) % tiles_per_seq == 0)
    def _():
        carry_ref[...] = jnp.zeros_like(carry_ref)

    c = _row_prefix_sum(lf) + carry_ref[0:1, :]
    c_ref[...] = c
    carry_ref[0:1, :] = c[-1:, :]


def _fox_inproj(h, seq_len, gn, w, bd, gq, gk, gmq, bf):
    t = h.shape[0]
    tm = min(ROW_TILE, seq_len)
    assert seq_len % tm == 0
    row = lambda n: pl.BlockSpec((tm, n), lambda i: (i, 0))
    return pl.pallas_call(
        functools.partial(_fox_inproj_kernel, seq_len // tm),
        grid=(t // tm,),
        in_specs=[row(D_MODEL), _resident(gn.shape), _resident(w.shape), _resident(bd.shape),
                  _resident(gq.shape), _resident(gk.shape), _resident(gmq.shape),
                  _resident(bf.shape)],
        out_specs=[row(MIX_DIM), row(MIX_DIM), row(MIX_DIM), row(MEM_DIM), row(F_LANES)],
        out_shape=[jax.ShapeDtypeStruct((t, MIX_DIM), BF16)] * 3
        + [jax.ShapeDtypeStruct((t, MEM_DIM), BF16), jax.ShapeDtypeStruct((t, F_LANES), F32)],
        scratch_shapes=[pltpu.VMEM((SUBLANES, F_LANES), F32)],
        compiler_params=_params("arbitrary"),
        name="fox_inproj",
    )(h, gn, w, bd, gq, gk, gmq, bf)


def _conv_inproj_kernel(h_ref, gn_ref, w_ref, b_ref, bd_ref, gmq_ref, u_ref, mq_ref):
    xb = _rms(h_ref[...], gn_ref[...]).astype(BF16)
    m = MIX_DIM
    a = jnp.dot(xb, w_ref[:, 0:m], preferred_element_type=F32) + b_ref[:, 0:m]
    g = jnp.dot(xb, w_ref[:, m:2 * m], preferred_element_type=F32) + b_ref[:, m:2 * m]
    u_ref[...] = a * (1.0 / (1.0 + jnp.exp(-g)))
    mq = jnp.dot(xb, w_ref[:, 2 * m:], preferred_element_type=F32)
    mq_ref[...] = _head_rms(mq, bd_ref[...], gmq_ref[...]).astype(BF16)


def _conv_inproj(h, seq_len, gn, w, b, bd, gmq):
    t = h.shape[0]
    tm = min(ROW_TILE, seq_len)
    assert t % tm == 0
    row = lambda n: pl.BlockSpec((tm, n), lambda i: (i, 0))
    return pl.pallas_call(
        _conv_inproj_kernel,
        grid=(t // tm,),
        in_specs=[row(D_MODEL), _resident(gn.shape), _resident(w.shape), _resident(b.shape),
                  _resident(bd.shape), _resident(gmq.shape)],
        out_specs=[row(MIX_DIM), row(MEM_DIM)],
        out_shape=[jax.ShapeDtypeStruct((t, MIX_DIM), F32),
                   jax.ShapeDtypeStruct((t, MEM_DIM), BF16)],
        compiler_params=_params("parallel"),
        name="conv_inproj",
    )(h, gn, w, b, bd, gmq)


def _mem_kv_kernel(m_ref, gn_ref, w_ref, bd_ref, gk_ref, mk_ref, mv_ref):
    xb = _rms(m_ref[...], gn_ref[...]).astype(BF16)
    mk = jnp.dot(xb, w_ref[:, 0:MEM_DIM], preferred_element_type=F32)
    mk_ref[...] = _head_rms(mk, bd_ref[...], gk_ref[...]).astype(BF16)
    mv_ref[...] = jnp.dot(xb, w_ref[:, MEM_DIM:], preferred_element_type=F32).astype(BF16)


def _mem_kv(mem2d, mem_len, gn, w, bd, gk):
    rows = mem2d.shape[0]
    row = lambda n: pl.BlockSpec((mem_len, n), lambda i: (i, 0))
    return pl.pallas_call(
        _mem_kv_kernel,
        grid=(rows // mem_len,),
        in_specs=[row(D_MODEL), _resident(gn.shape), _resident(w.shape), _resident(bd.shape),
                  _resident(gk.shape)],
        out_specs=[row(MEM_DIM), row(MEM_DIM)],
        out_shape=[jax.ShapeDtypeStruct((rows, MEM_DIM), BF16)] * 2,
        compiler_params=_params("parallel"),
        name="mem_kv",
    )(mem2d, gn, w, bd, gk)


def _head_lane_mask(shape, head_in_pair):
    lane = lax.broadcasted_iota(jnp.int32, shape, len(shape) - 1)
    return (lane // HEAD_DIM) == head_in_pair


def _fox_attn_kernel(tq, tk, q_ref, k_ref, v_ref, c_ref, o_ref, m_ref, l_ref, acc_ref):
    i = pl.program_id(2)
    q2 = q_ref[0]
    zero = jnp.zeros_like(q2)
    q_stack = jnp.concatenate([jnp.where(_head_lane_mask(q2.shape, hh), q2, zero)
                               for hh in range(2)], axis=0)
    m_ref[...] = jnp.full(m_ref.shape, NEG_INF, F32)
    l_ref[...] = jnp.zeros(l_ref.shape, F32)
    acc_ref[...] = jnp.zeros(acc_ref.shape, F32)
    n_full = (i * tq) // tk
    q_pos = i * tq + lax.broadcasted_iota(jnp.int32, (tq, tk), 0)
    k_off = lax.broadcasted_iota(jnp.int32, (tq, tk), 1)

    def block(j, _, masked):
        start = pl.multiple_of(j * tk, tk)
        kb = k_ref[0, pl.ds(start, tk), :]
        vb = v_ref[0, pl.ds(start, tk), :]
        s2 = lax.dot_general(q_stack, kb, (((1,), (1,)), ((), ())), preferred_element_type=F32)
        for hh in range(2):
            s = s2[hh * tq:(hh + 1) * tq] - c_ref[0, 0, hh:hh + 1, pl.ds(start, tk)]
            if masked:
                s = jnp.where(start + k_off <= q_pos, s, NEG_INF)
            m_old = m_ref[hh]
            m_new = jnp.maximum(m_old, jnp.max(s, axis=-1, keepdims=True))
            alpha = jnp.exp(m_old - m_new)
            ps = [jnp.exp(s[:, t0:t0 + LANES] - m_new) for t0 in range(0, tk, LANES)]
            l_ref[hh] = alpha * l_ref[hh] + functools.reduce(lambda a, b: a + b, ps)
            p = jnp.concatenate(ps, axis=1).astype(BF16)
            acc_ref[hh] = alpha * acc_ref[hh] + jnp.dot(p, vb, preferred_element_type=F32)
            m_ref[hh] = m_new
        return 0

    lax.fori_loop(0, n_full, functools.partial(block, masked=False), 0)
    lax.fori_loop(n_full, n_full + tq // tk, functools.partial(block, masked=True), 0)
    outs = [acc_ref[hh] * (1.0 / jnp.sum(l_ref[hh], axis=-1, keepdims=True)) for hh in range(2)]
    o_ref[0] = jnp.where(_head_lane_mask(outs[0].shape, 0), outs[0], outs[1]).astype(o_ref.dtype)


def _fox_attn(q, k, v, c_t):
    b, s, _ = q.shape
    tq = min(ATTN_Q_TILE, s)
    tk = min(ATTN_KV_TILE, tq)
    assert s % tq == 0 and tq % tk == 0
    seq = pl.BlockSpec((1, s, LANES), lambda bi, hp, i: (bi, 0, hp))
    return pl.pallas_call(
        functools.partial(_fox_attn_kernel, tq, tk),
        grid=(b, FOX_HEADS // 2, s // tq),
        in_specs=[pl.BlockSpec((1, tq, LANES), lambda bi, hp, i: (bi, i, hp)), seq, seq,
                  pl.BlockSpec((1, 1, 2, s), lambda bi, hp, i: (bi, hp, 0, 0))],
        out_specs=pl.BlockSpec((1, tq, LANES), lambda bi, hp, i: (bi, i, hp)),
        out_shape=jax.ShapeDtypeStruct((b, s, MIX_DIM), BF16),
        scratch_shapes=[pltpu.VMEM((2, tq, LANES), F32)] * 3,
        compiler_params=_params("parallel", "parallel", "arbitrary"),
        name="fox_attn",
    )(q, k, v, c_t)


def _mem_attn_kernel(q_ref, k_ref, v_ref, o_ref):
    for pair in range(MEM_DIM // LANES):
        lanes = slice(pair * LANES, (pair + 1) * LANES)
        q2 = q_ref[:, lanes]
        k2 = k_ref[:, lanes]
        v2 = v_ref[:, lanes]
        outs = []
        for hh in range(2):
            qh = jnp.where(_head_lane_mask(q2.shape, hh), q2, jnp.zeros_like(q2))
            s = lax.dot_general(qh, k2, (((1,), (1,)), ((), ())), preferred_element_type=F32)
            p = jnp.exp(s - jnp.max(s, axis=-1, keepdims=True))
            l = jnp.sum(p, axis=-1, keepdims=True)
            outs.append(jnp.dot(p.astype(BF16), v2, preferred_element_type=F32) * (1.0 / l))
        o_ref[:, lanes] = jnp.where(_head_lane_mask(outs[0].shape, 0), outs[0],
                                    outs[1]).astype(o_ref.dtype)


def _mem_attn(mq, mk, mv, seq_len, mem_len):
    t = mq.shape[0]
    tm = min(ROW_TILE, seq_len)
    per_seq = seq_len // tm
    kv = pl.BlockSpec((mem_len, MEM_DIM), lambda i: (i // per_seq, 0))
    return pl.pallas_call(
        _mem_attn_kernel,
        grid=(t // tm,),
        in_specs=[pl.BlockSpec((tm, MEM_DIM), lambda i: (i, 0)), kv, kv],
        out_specs=pl.BlockSpec((tm, MEM_DIM), lambda i: (i, 0)),
        out_shape=jax.ShapeDtypeStruct((t, MEM_DIM), BF16),
        compiler_params=_params("parallel"),
        name="mem_attn",
    )(mq, mk, mv)


def _conv_kernel(ts, u_ref, dw_ref, dwb_ref, lng_ref, lnb_ref, o_ref, ext_ref, acc_ref):
    @pl.when(pl.program_id(1) == 0)
    def _():
        ext_ref[0:CONV_HALO, :] = jnp.zeros((CONV_HALO, MIX_DIM), F32)

    @pl.when(pl.program_id(1) > 0)
    def _():
        ext_ref[0:CONV_HALO, :] = ext_ref[ts:ts + CONV_HALO, :]

    ext_ref[CONV_HALO:CONV_HALO + ts, :] = u_ref[0]

    base = CONV_HALO - (CONV_WIDTH - 1)
    n_shift = SUBLANES
    rc = min(CONV_ROW_CHUNK, ts)
    for r0 in range(0, ts, rc):
        for l0 in range(0, MIX_DIM, LANES):
            acc = jnp.zeros((rc, LANES), F32) + dwb_ref[:, l0:l0 + LANES]
            for b in range(n_shift):
                n_a = (CONV_WIDTH - 1 - b) // n_shift + 1
                win = ext_ref[r0 + base + b:r0 + base + b + rc + (n_a - 1) * n_shift,
                              l0:l0 + LANES]
                for a in range(n_a):
                    j = a * n_shift + b
                    acc = acc + dw_ref[j:j + 1, l0:l0 + LANES] * win[a * n_shift:
                                                                      a * n_shift + rc, :]
            acc_ref[r0:r0 + rc, l0:l0 + LANES] = acc

    y = acc_ref[...]
    mu = jnp.mean(y, axis=-1, keepdims=True)
    yc = y - mu
    var = jnp.mean(yc * yc, axis=-1, keepdims=True)
    z = yc * lax.rsqrt(var + EPS) * lng_ref[...] + lnb_ref[...]
    o_ref[0] = _silu(z).astype(o_ref.dtype)


def _conv_module(u, dw, dwb, lng, lnb):
    b, s, _ = u.shape
    ts = min(CONV_TILE, s)
    assert s % ts == 0 and ts >= CONV_HALO
    blk = pl.BlockSpec((1, ts, MIX_DIM), lambda bi, i: (bi, i, 0))
    return pl.pallas_call(
        functools.partial(_conv_kernel, ts),
        grid=(b, s // ts),
        in_specs=[blk, _resident(dw.shape), _resident(dwb.shape), _resident(lng.shape),
                  _resident(lnb.shape)],
        out_specs=blk,
        out_shape=jax.ShapeDtypeStruct((b, s, MIX_DIM), BF16),
        scratch_shapes=[pltpu.VMEM((ts + CONV_HALO, MIX_DIM), F32),
                        pltpu.VMEM((ts, MIX_DIM), F32)],
        compiler_params=_params("parallel", "arbitrary"),
        name="conv_module",
    )(u, dw, dwb, lng, lnb)


def _outproj_math(h_ref, mix_ref, mem_ref, w_ref, gn_ref):
    h2 = (h_ref[...]
          + jnp.dot(mix_ref[...], w_ref[0:MIX_DIM, :], preferred_element_type=F32)
          + jnp.dot(mem_ref[...], w_ref[MIX_DIM:, :], preferred_element_type=F32))
    return h2, _rms(h2, gn_ref[...])


def _outproj_dense_kernel(h_ref, mix_ref, mem_ref, w_ref, gn_ref, h2_ref, z_ref):
    h2, z = _outproj_math(h_ref, mix_ref, mem_ref, w_ref, gn_ref)
    h2_ref[...] = h2
    z_ref[...] = z.astype(BF16)


def _split_bf16(x):
    hi = x.astype(BF16)
    return hi, (x - hi.astype(F32)).astype(BF16)


def _outproj_router_kernel(h_ref, mix_ref, mem_ref, w_ref, gn_ref, wr_hi_ref, wr_lo_ref,
                           h2_ref, zp_ref, route_ref):
    h2, z = _outproj_math(h_ref, mix_ref, mem_ref, w_ref, gn_ref)
    h2_ref[...] = h2

    half = D_MODEL // 2
    lo_bits = pltpu.bitcast(z[:, :half].astype(BF16).astype(F32), jnp.uint32) >> 16
    hi_bits = pltpu.bitcast(z[:, half:].astype(BF16).astype(F32), jnp.uint32) & jnp.uint32(0xFFFF0000)
    packed = lo_bits | hi_bits
    for j in range(half // LANES):
        zp_ref[pl.ds(j, packed.shape[0], stride=PACKED_SLABS), :] = packed[:, j * LANES:(j + 1) * LANES]

    z_hi, z_lo = _split_bf16(z)
    logits = (jnp.dot(z_hi, wr_hi_ref[...], preferred_element_type=F32)
              + jnp.dot(z_lo, wr_hi_ref[...], preferred_element_type=F32)
              + jnp.dot(z_hi, wr_lo_ref[...], preferred_element_type=F32))
    lane = lax.broadcasted_iota(jnp.int32, logits.shape, 1)
    lane_f = lane.astype(F32)
    logits = jnp.where(lane < N_EXPERTS, logits, -jnp.inf)
    l1 = jnp.max(logits, axis=-1, keepdims=True)
    e1 = jnp.min(jnp.where(logits == l1, lane_f, float(LANES)), axis=-1, keepdims=True)
    rest = jnp.where(lane_f == e1, -jnp.inf, logits)
    l2 = jnp.max(rest, axis=-1, keepdims=True)
    e2 = jnp.min(jnp.where(rest == l2, lane_f, float(LANES)), axis=-1, keepdims=True)
    g2 = 1.0 / (1.0 + jnp.exp(l1 - l2))
    g1 = 1.0 - g2
    route = jnp.where(lane == 0, g1, 0.0) + jnp.where(lane == 1, g2, 0.0)
    route = route + jnp.where(lane == 2, e1, 0.0)
    route_ref[...] = route + jnp.where(lane == 3, e2, 0.0)


def _outproj(h, mix, mem, w, gn, router=None):
    t = h.shape[0]
    tm = min(ROW_TILE, t)
    assert t % tm == 0
    row = lambda n: pl.BlockSpec((tm, n), lambda i: (i, 0))
    in_specs = [row(D_MODEL), row(MIX_DIM), row(MEM_DIM), _resident(w.shape), _resident(gn.shape)]
    if router is None:
        return pl.pallas_call(
            _outproj_dense_kernel, grid=(t // tm,), in_specs=in_specs,
            out_specs=[row(D_MODEL), row(D_MODEL)],
            out_shape=[jax.ShapeDtypeStruct((t, D_MODEL), F32),
                       jax.ShapeDtypeStruct((t, D_MODEL), BF16)],
            compiler_params=_params("parallel"), name="outproj_dense",
        )(h, mix, mem, w, gn)
    wr_hi, wr_lo = router
    return pl.pallas_call(
        _outproj_router_kernel, grid=(t // tm,),
        in_specs=in_specs + [_resident(wr_hi.shape), _resident(wr_lo.shape)],
        out_specs=[row(D_MODEL), pl.BlockSpec((tm * PACKED_SLABS, LANES), lambda i: (i, 0)),
                   row(LANES)],
        out_shape=[jax.ShapeDtypeStruct((t, D_MODEL), F32),
                   jax.ShapeDtypeStruct((t * PACKED_SLABS, LANES), jnp.uint32),
                   jax.ShapeDtypeStruct((t, LANES), F32)],
        compiler_params=_params("parallel"), name="outproj_router",
    )(h, mix, mem, w, gn, wr_hi, wr_lo)


def _ffn_kernel(h_ref, z_ref, wg_ref, wu_ref, wd_ref, o_ref, acc_ref):
    z = z_ref[...]
    d_ff = wg_ref.shape[1]
    acc_ref[...] = h_ref[...]
    for f0 in range(0, d_ff, FFN_CHUNK):
        g = jnp.dot(z, wg_ref[:, f0:f0 + FFN_CHUNK], preferred_element_type=F32)
        u = jnp.dot(z, wu_ref[:, f0:f0 + FFN_CHUNK], preferred_element_type=F32)
        a = (_silu(g) * u).astype(BF16)
        acc_ref[...] += jnp.dot(a, wd_ref[f0:f0 + FFN_CHUNK, :], preferred_element_type=F32)
    o_ref[...] = acc_ref[...]


def _ffn(h2, z, wg, wu, wd):
    t = h2.shape[0]
    tm = min(ROW_TILE, t)
    assert t % tm == 0 and wg.shape[1] % FFN_CHUNK == 0
    row = pl.BlockSpec((tm, D_MODEL), lambda i: (i, 0))
    return pl.pallas_call(
        _ffn_kernel, grid=(t // tm,),
        in_specs=[row, row, _resident(wg.shape), _resident(wu.shape), _resident(wd.shape)],
        out_specs=row,
        out_shape=jax.ShapeDtypeStruct((t, D_MODEL), F32),
        scratch_shapes=[pltpu.VMEM((tm, D_MODEL), F32)],
        compiler_params=_params("parallel"), name="ffn_dense",
    )(h2, z, wg, wu, wd)


def _token_copy(src, src_row, dst, dst_row, slabs, sem):
    return pltpu.make_async_copy(src.at[pl.ds(src_row, slabs), :], dst.at[pl.ds(dst_row, slabs), :],
                                 sem)


def _block_copy(src, dst, slabs, sem):
    n = EXPERT_ROWS * slabs
    return pltpu.make_async_copy(src.at[pl.ds(0, n), :], dst.at[pl.ds(0, n), :], sem)


def _experts_kernel(n_f, be_ref, nact_ref, tok_cur_ref, tok_next_ref, dst_prev_ref, dst_cur_ref,
                    zp_hbm, wg_ref, wu_ref, wd_ref, y_hbm, xp_ref, x_ref, acc_ref, stage_ref,
                    sem_in, sem_out):
    del be_ref
    i = pl.program_id(0)
    f = pl.program_id(1)
    n_active = nact_ref[0]
    active = i < n_active
    has_next = i + 1 < n_active
    has_prev = i >= 1
    first_step = f == 0
    last_step = f == n_f - 1
    half = D_MODEL // 2

    def gather(idx_ref, r):
        src = pl.multiple_of(idx_ref[0, 0, r], PACKED_SLABS)
        _token_copy(zp_hbm, src, xp_ref, r * PACKED_SLABS, PACKED_SLABS, sem_in).start()

    def scatter(idx_ref, r):
        dst = pl.multiple_of(idx_ref[0, 0, r], OUT_SLABS)
        _token_copy(stage_ref, r * OUT_SLABS, y_hbm, dst, OUT_SLABS, sem_out).start()

    gather_next = functools.partial(gather, tok_next_ref)
    scatter_prev = functools.partial(scatter, dst_prev_ref)

    def rolled(fn):
        def body(r, _):
            fn(r)
            return 0
        lax.fori_loop(0, EXPERT_ROWS, body, 0)

    def compute(phase, inline_dma=None):
        def sub_block(s, _):
            r0 = pl.multiple_of(s * EXPERT_SUB, EXPERT_SUB)
            rows = pl.ds(r0, EXPERT_SUB)
            x = x_ref[rows, :]
            g = jnp.dot(x, wg_ref[0, 0], preferred_element_type=F32)
            u = jnp.dot(x, wu_ref[0, 0], preferred_element_type=F32)
            a = (_silu(g) * u).astype(BF16)
            c = jnp.dot(a, wd_ref[0, 0], preferred_element_type=F32)
            if phase == "first":
                acc_ref[rows, :] = c
            elif phase == "mid":
                acc_ref[rows, :] += c
            else:
                out = acc_ref[rows, :] + c
                for j in range(OUT_SLABS):
                    stage_ref[pl.ds(r0 * OUT_SLABS + j, EXPERT_SUB, stride=OUT_SLABS), :] = (
                        out[:, j * LANES:(j + 1) * LANES])
            if inline_dma is not None:
                for k in range(EXPERT_SUB):
                    inline_dma(r0 + k)
            return 0
        lax.fori_loop(0, EXPERT_ROWS // EXPERT_SUB, sub_block, 0)

    @pl.when((i == 0) & first_step)
    def _():
        stage_ref[...] = jnp.zeros(stage_ref.shape, F32)
        n_sink = EXPERT_ROWS * OUT_SLABS
        sink = pltpu.make_async_copy(
            stage_ref, y_hbm.at[pl.ds(y_hbm.shape[0] - n_sink, n_sink), :], sem_out)
        sink.start()
        sink.wait()
        rolled(functools.partial(gather, tok_cur_ref))

    @pl.when(active & first_step)
    def _():
        _block_copy(zp_hbm, xp_ref, PACKED_SLABS, sem_in).wait()
        for j in range(PACKED_SLABS):
            xp = xp_ref[pl.ds(j, EXPERT_ROWS, stride=PACKED_SLABS), :]
            lanes = slice(j * LANES, (j + 1) * LANES)
            x_ref[:, lanes] = pltpu.bitcast(xp << 16, F32).astype(BF16)
            x_ref[:, half + j * LANES:half + (j + 1) * LANES] = pltpu.bitcast(
                xp & jnp.uint32(0xFFFF0000), F32).astype(BF16)

    @pl.when(active & first_step & has_prev)
    def _():
        compute("first", scatter_prev)

    @pl.when(active & first_step & jnp.logical_not(has_prev))
    def _():
        compute("first")

    if n_f > 2:
        @pl.when(active & jnp.logical_not(first_step) & jnp.logical_not(last_step))
        def _():
            compute("mid")

    @pl.when(active & last_step & has_prev)
    def _():
        _block_copy(stage_ref, y_hbm, OUT_SLABS, sem_out).wait()

    @pl.when(active & last_step & has_next)
    def _():
        compute("last", gather_next)

    @pl.when(active & last_step & jnp.logical_not(has_next))
    def _():
        compute("last")
        rolled(functools.partial(scatter, dst_cur_ref))
        _block_copy(stage_ref, y_hbm, OUT_SLABS, sem_out).wait()


def _experts(zp, plan, layer, wg, wu, wd):
    t = zp.shape[0] // PACKED_SLABS
    blk_expert, n_active, tok_buf, dst_buf = plan
    nb = blk_expert.shape[0]
    d_ff = wg.shape[3]
    tf = EXPERT_FF_TILE
    n_f = d_ff // tf
    assert d_ff % tf == 0 and n_f >= 2
    last = nb - 1

    def idx(shift):
        return pl.BlockSpec((1, 1, EXPERT_ROWS),
                            lambda i, f, *_: (jnp.clip(i + shift, 0, last), 0, 0),
                            memory_space=pltpu.SMEM)

    grid_spec = pltpu.PrefetchScalarGridSpec(
        num_scalar_prefetch=2,
        grid=(nb, n_f),
        in_specs=[idx(0), idx(1), idx(-1), idx(0), pl.BlockSpec(memory_space=pl.ANY),
                  pl.BlockSpec((1, 1, D_MODEL, tf), lambda i, f, be, *_: (layer, be[i], 0, f)),
                  pl.BlockSpec((1, 1, D_MODEL, tf), lambda i, f, be, *_: (layer, be[i], 0, f)),
                  pl.BlockSpec((1, 1, tf, D_MODEL), lambda i, f, be, *_: (layer, be[i], f, 0))],
        out_specs=pl.BlockSpec(memory_space=pl.ANY),
        scratch_shapes=[pltpu.VMEM((EXPERT_ROWS * PACKED_SLABS, LANES), jnp.uint32),
                        pltpu.VMEM((EXPERT_ROWS, D_MODEL), BF16),
                        pltpu.VMEM((EXPERT_ROWS, D_MODEL), F32),
                        pltpu.VMEM((EXPERT_ROWS * OUT_SLABS, LANES), F32),
                        pltpu.SemaphoreType.DMA(()), pltpu.SemaphoreType.DMA(())],
    )
    return pl.pallas_call(
        functools.partial(_experts_kernel, n_f), grid_spec=grid_spec,
        out_shape=jax.ShapeDtypeStruct(((TOP_K * t + EXPERT_ROWS) * OUT_SLABS, LANES), F32),
        compiler_params=_params("arbitrary", "arbitrary"), name="moe_experts",
    )(blk_expert, n_active, tok_buf, tok_buf, dst_buf, dst_buf, zp, wg, wu, wd)


def _route_plan(route, t):
    a = t * TOP_K
    nb = a // EXPERT_ROWS + N_EXPERTS
    rows = nb * EXPERT_ROWS
    flat_e = route[:, 2:4].astype(jnp.int32).reshape(a)
    experts = jnp.arange(N_EXPERTS, dtype=jnp.int32)
    sizes = jnp.sum((flat_e[:, None] == experts[None, :]).astype(jnp.int32), axis=0)
    start = jnp.cumsum(sizes) - sizes
    padded = ((sizes + EXPERT_ROWS - 1) // EXPERT_ROWS) * EXPERT_ROWS
    pad_end = jnp.cumsum(padded)
    pad_start = pad_end - padded
    sorted_assign = jnp.sort(flat_e * a + jnp.arange(a, dtype=jnp.int32)) % a
    blk_start = jnp.arange(nb, dtype=jnp.int32) * EXPERT_ROWS
    blk_expert = jnp.minimum(jnp.searchsorted(pad_end, blk_start, side="right"),
                             N_EXPERTS - 1).astype(jnp.int32)
    row = jnp.arange(rows, dtype=jnp.int32)
    row_e = jnp.repeat(blk_expert, EXPERT_ROWS)
    rank = row - pad_start[row_e]
    real = (rank >= 0) & (rank < sizes[row_e])
    assign = sorted_assign[jnp.clip(start[row_e] + rank, 0, a - 1)]
    tok = assign // TOP_K
    tok_buf = jnp.where(real, tok, 0)
    dst_buf = jnp.where(real, tok + (assign % TOP_K) * t, TOP_K * t + row % EXPERT_ROWS)
    n_active = (pad_end[-1] // EXPERT_ROWS).reshape(1).astype(jnp.int32)
    return (blk_expert, n_active, (tok_buf * PACKED_SLABS).reshape(nb, 1, EXPERT_ROWS),
            (dst_buf * OUT_SLABS).reshape(nb, 1, EXPERT_ROWS))


def _combine_kernel(h_ref, y0_ref, y1_ref, route_ref, o_ref):
    g = route_ref[...]
    for j in range(OUT_SLABS):
        lanes = slice(j * LANES, (j + 1) * LANES)
        rows = pl.ds(j, h_ref.shape[0], stride=OUT_SLABS)
        o_ref[:, lanes] = h_ref[:, lanes] + (g[:, 0:1] * y0_ref[rows, :] + g[:, 1:2] * y1_ref[rows, :])


def _combine(h2, y, route):
    t = h2.shape[0]
    tm = min(ROW_TILE, t)
    nt = t // tm
    row = pl.BlockSpec((tm, D_MODEL), lambda i: (i, 0))
    return pl.pallas_call(
        _combine_kernel, grid=(nt,),
        in_specs=[row, pl.BlockSpec((tm * OUT_SLABS, LANES), lambda i: (i, 0)),
                  pl.BlockSpec((tm * OUT_SLABS, LANES), lambda i: (i + nt, 0)),
                  pl.BlockSpec((tm, LANES), lambda i: (i, 0))],
        out_specs=row,
        out_shape=jax.ShapeDtypeStruct((t, D_MODEL), F32),
        compiler_params=_params("parallel"), name="moe_combine",
    )(h2, y, y, route)


def _row(v):
    return v.reshape(1, -1).astype(F32)


def _tile_heads(g, n, scale=1.0):
    return _row(jnp.tile(g.astype(F32) * scale, n))


def kernel(x, mem, norm_mix, norm_mem, norm_ffn, w_mem_kv, g_mq, g_mk, fox_w_in, fox_b_f, fox_g_q, fox_g_k, fox_w_out, conv_w_in, conv_b_in, conv_dw, conv_dw_b, conv_ln_g, conv_ln_b, conv_w_out, ffn_w_gate, ffn_w_up, ffn_w_down, moe_router, moe_w_gate, moe_w_up, moe_w_down):
    b, s, d = x.shape
    mem_len = mem.shape[1]
    t = b * s
    assert d == D_MODEL
    scale = HEAD_DIM ** -0.5

    group = jnp.arange(MIX_DIM, dtype=jnp.int32) // HEAD_DIM
    ones_bd = (group[:, None] == group[None, :]).astype(BF16)
    ones_bd_mem = ones_bd[:MEM_DIM, :MEM_DIM]

    moe_wg, moe_wu, moe_wd = (w.astype(BF16) for w in (moe_w_gate, moe_w_up, moe_w_down))
    h = x.reshape(t, d)
    mem2d = mem.reshape(b * mem_len, d)
    for i in range(DEPTH):
        j = i // 2
        gmq = _tile_heads(g_mq[i], MEM_DIM // HEAD_DIM, scale)
        mk, mv = _mem_kv(mem2d, mem_len, _row(norm_mem[i]), w_mem_kv[i].astype(BF16),
                         ones_bd_mem, _tile_heads(g_mk[i], MEM_DIM // HEAD_DIM))
        if i % 2 == 0:
            w = fox_w_in[j]
            m3 = 3 * MIX_DIM
            w_cat = jnp.concatenate(
                [w[:, :m3], w[:, m3 + FOX_HEADS:], w[:, m3:m3 + FOX_HEADS],
                 jnp.zeros((d, F_LANES - FOX_HEADS), w.dtype)], axis=1).astype(BF16)
            bf = jnp.pad(fox_b_f[j].astype(F32), (0, F_LANES - FOX_HEADS)).reshape(1, F_LANES)
            q, k, v, mq, c = _fox_inproj(
                h, s, _row(norm_mix[i]), w_cat, ones_bd,
                _tile_heads(fox_g_q[j], FOX_HEADS, scale), _tile_heads(fox_g_k[j], FOX_HEADS),
                gmq, bf)
            c_t = c[:, :FOX_HEADS].reshape(b, s, FOX_HEADS // 2, 2).transpose(0, 2, 3, 1)
            mix = _fox_attn(q.reshape(b, s, MIX_DIM), k.reshape(b, s, MIX_DIM),
                            v.reshape(b, s, MIX_DIM), c_t).reshape(t, MIX_DIM)
            w_out = fox_w_out[j]
        else:
            u, mq = _conv_inproj(h, s, _row(norm_mix[i]), conv_w_in[j].astype(BF16),
                                 _row(conv_b_in[j]), ones_bd_mem, gmq)
            mix = _conv_module(u.reshape(b, s, MIX_DIM), conv_dw[j].astype(F32),
                               _row(conv_dw_b[j]), _row(conv_ln_g[j]),
                               _row(conv_ln_b[j])).reshape(t, MIX_DIM)
            w_out = conv_w_out[j]
        mem_out = _mem_attn(mq, mk, mv, s, mem_len)
        if i % 2 == 0:
            h2, z = _outproj(h, mix, mem_out, w_out.astype(BF16), _row(norm_ffn[i]))
            h = _ffn(h2, z, ffn_w_gate[j].astype(BF16), ffn_w_up[j].astype(BF16),
                     ffn_w_down[j].astype(BF16))
        else:
            wr = jnp.pad(moe_router[j].astype(F32), ((0, 0), (0, LANES - N_EXPERTS)))
            wr_hi = wr.astype(BF16)
            wr_lo = (wr - wr_hi.astype(F32)).astype(BF16)
            h2, zp, route = _outproj(h, mix, mem_out, w_out.astype(BF16), _row(norm_ffn[i]),
                                     router=(wr_hi, wr_lo))
            plan = _route_plan(route, t)
            y = _experts(zp, plan, j, moe_wg, moe_wu, moe_wd)
            h = _combine(h2, y, route)
    return h.reshape(b, s, d)
```

```python
import functools

import jax
import jax.numpy as jnp
from jax import lax
from jax.experimental import pallas as pl
from jax.experimental.pallas import tpu as pltpu

F32 = jnp.float32
BF16 = jnp.bfloat16

D_MODEL = 1024
HEAD_DIM = 64
MEM_DIM = 256
MIX_DIM = D_MODEL - MEM_DIM
FOX_HEADS = MIX_DIM // HEAD_DIM
CONV_WIDTH = 31
N_EXPERTS = 8
TOP_K = 2
DEPTH = 4
EPS = 1e-6
NEG_INF = -1e30
LOG2E = 1.4426950408889634
BIAS_PARTS = 3

LANES = 128
SUBLANES = 8
VMEM_LIMIT_BYTES = 56 * 1024 * 1024

ROW_TILE = 512
ATTN_TILE = 512
CONV_TILE = 256
CONV_HALO = 32
CONV_ROW_CHUNK = 64
FFN_CHUNK = 256
EXPERT_ROWS = 1024
EXPERT_SUB = 256
EXPERT_FF_TILE = 1792
F_LANES = LANES
PACKED_SLABS = D_MODEL // 2 // LANES
OUT_SLABS = D_MODEL // LANES


def _params(*sem):
    return pltpu.CompilerParams(dimension_semantics=sem, vmem_limit_bytes=VMEM_LIMIT_BYTES)


def _resident(shape):
    nd = len(shape)
    return pl.BlockSpec(shape, lambda *_: (0,) * nd, pipeline_mode=pl.Buffered(1))


def _rms(x, g):
    ms = jnp.mean(x * x, axis=-1, keepdims=True)
    return x * lax.rsqrt(ms + EPS) * g


def _head_rms(x, ones_bd, g):
    ss = jnp.dot((x * x).astype(BF16), ones_bd, preferred_element_type=F32)
    return x * lax.rsqrt(ss * (1.0 / HEAD_DIM) + EPS) * g


def _silu(x):
    return x * (1.0 / (1.0 + jnp.exp(-x)))


def _log_sigmoid(x):
    return jnp.minimum(x, 0.0) - jnp.log(1.0 + jnp.exp(-jnp.abs(x)))


def _row_prefix_sum(x):
    n = x.shape[0]
    row = lax.broadcasted_iota(jnp.int32, x.shape, 0)
    s = 1
    while s < n:
        x = x + jnp.where(row >= s, pltpu.roll(x, s, 0), 0.0)
        s *= 2
    return x


def _split_bf16(x, parts):
    out = []
    for _ in range(parts - 1):
        hi = x.astype(BF16)
        out.append(hi)
        x = x - hi.astype(F32)
    out.append(x.astype(BF16))
    return out


def _fox_inproj_kernel(tiles_per_seq, h_ref, gn_ref, w_ref, bd_ref, gq_ref, gk_ref, gmq_ref,
                       bf_ref, place_ref, q_ref, k_ref, v_ref, mq_ref, kb_ref, carry_ref):
    xb = _rms(h_ref[...], gn_ref[...]).astype(BF16)
    m = MIX_DIM
    q = jnp.dot(xb, w_ref[:, 0:m], preferred_element_type=F32)
    q_ref[...] = _head_rms(q, bd_ref[...], gq_ref[...]).astype(BF16)
    k = jnp.dot(xb, w_ref[:, m:2 * m], preferred_element_type=F32)
    k_ref[...] = _head_rms(k, bd_ref[...], gk_ref[...]).astype(BF16)
    v_ref[...] = jnp.dot(xb, w_ref[:, 2 * m:3 * m], preferred_element_type=F32).astype(BF16)
    mq = jnp.dot(xb, w_ref[:, 3 * m:3 * m + MEM_DIM], preferred_element_type=F32)
    mq_ref[...] = _head_rms(mq, bd_ref[0:MEM_DIM, 0:MEM_DIM], gmq_ref[...]).astype(BF16)

    f = jnp.dot(xb, w_ref[:, 3 * m + MEM_DIM:], preferred_element_type=F32) + bf_ref[...]
    lane = lax.broadcasted_iota(jnp.int32, f.shape, 1)
    lf = jnp.where(lane < FOX_HEADS, _log_sigmoid(f), 0.0)

    @pl.when(pl.program_id(0) % tiles_per_seq == 0)
    def _():
        carry_ref[...] = jnp.zeros_like(carry_ref)

    c = _row_prefix_sum(lf) + carry_ref[0:1, :]
    carry_ref[0:1, :] = c[-1:, :]
    parts = _split_bf16(c * (-LOG2E), BIAS_PARTS)
    kb = jnp.dot(parts[0], place_ref[0], preferred_element_type=F32)
    for p in range(1, BIAS_PARTS):
        kb = kb + jnp.dot(parts[p], place_ref[p], preferred_element_type=F32)
    kb_ref[...] = kb.astype(BF16)


def _bias_placement():
    h = jnp.arange(F_LANES, dtype=jnp.int32)[:, None]
    col = jnp.arange(MIX_DIM, dtype=jnp.int32)[None, :]
    return jnp.stack([((h < FOX_HEADS)
                       & (col == (h // 2) * LANES + BIAS_PARTS * (h % 2) + p)).astype(BF16)
                      for p in range(BIAS_PARTS)])


def _fox_inproj(h, seq_len, gn, w, bd, gq, gk, gmq, bf):
    t = h.shape[0]
    tm = min(ROW_TILE, seq_len)
    assert seq_len % tm == 0
    row = lambda n: pl.BlockSpec((tm, n), lambda i: (i, 0))
    place = _bias_placement()
    return pl.pallas_call(
        functools.partial(_fox_inproj_kernel, seq_len // tm),
        grid=(t // tm,),
        in_specs=[row(D_MODEL), _resident(gn.shape), _resident(w.shape), _resident(bd.shape),
                  _resident(gq.shape), _resident(gk.shape), _resident(gmq.shape),
                  _resident(bf.shape), _resident(place.shape)],
        out_specs=[row(MIX_DIM), row(MIX_DIM), row(MIX_DIM), row(MEM_DIM), row(MIX_DIM)],
        out_shape=[jax.ShapeDtypeStruct((t, MIX_DIM), BF16)] * 3
        + [jax.ShapeDtypeStruct((t, MEM_DIM), BF16), jax.ShapeDtypeStruct((t, MIX_DIM), BF16)],
        scratch_shapes=[pltpu.VMEM((SUBLANES, F_LANES), F32)],
        compiler_params=_params("arbitrary"),
        name="fox_inproj",
    )(h, gn, w, bd, gq, gk, gmq, bf, place)


def _conv_inproj_kernel(h_ref, gn_ref, w_ref, b_ref, bd_ref, gmq_ref, u_ref, mq_ref):
    xb = _rms(h_ref[...], gn_ref[...]).astype(BF16)
    m = MIX_DIM
    a = jnp.dot(xb, w_ref[:, 0:m], preferred_element_type=F32) + b_ref[:, 0:m]
    g = jnp.dot(xb, w_ref[:, m:2 * m], preferred_element_type=F32) + b_ref[:, m:2 * m]
    u_ref[...] = a * (1.0 / (1.0 + jnp.exp(-g)))
    mq = jnp.dot(xb, w_ref[:, 2 * m:], preferred_element_type=F32)
    mq_ref[...] = _head_rms(mq, bd_ref[...], gmq_ref[...]).astype(BF16)


def _conv_inproj(h, seq_len, gn, w, b, bd, gmq):
    t = h.shape[0]
    tm = min(ROW_TILE, seq_len)
    assert t % tm == 0
    row = lambda n: pl.BlockSpec((tm, n), lambda i: (i, 0))
    return pl.pallas_call(
        _conv_inproj_kernel,
        grid=(t // tm,),
        in_specs=[row(D_MODEL), _resident(gn.shape), _resident(w.shape), _resident(b.shape),
                  _resident(bd.shape), _resident(gmq.shape)],
        out_specs=[row(MIX_DIM), row(MEM_DIM)],
        out_shape=[jax.ShapeDtypeStruct((t, MIX_DIM), F32),
                   jax.ShapeDtypeStruct((t, MEM_DIM), BF16)],
        compiler_params=_params("parallel"),
        name="conv_inproj",
    )(h, gn, w, b, bd, gmq)


def _mem_kv_kernel(m_ref, gn_ref, w_ref, bd_ref, gk_ref, mk_ref, mv_ref):
    xb = _rms(m_ref[...], gn_ref[...]).astype(BF16)
    mk = jnp.dot(xb, w_ref[:, 0:MEM_DIM], preferred_element_type=F32)
    mk_ref[...] = _head_rms(mk, bd_ref[...], gk_ref[...]).astype(BF16)
    mv_ref[...] = jnp.dot(xb, w_ref[:, MEM_DIM:], preferred_element_type=F32).astype(BF16)


def _mem_kv(mem2d, mem_len, gn, w, bd, gk):
    rows = mem2d.shape[0]
    row = lambda n: pl.BlockSpec((mem_len, n), lambda i: (i, 0))
    return pl.pallas_call(
        _mem_kv_kernel,
        grid=(rows // mem_len,),
        in_specs=[row(D_MODEL), _resident(gn.shape), _resident(w.shape), _resident(bd.shape),
                  _resident(gk.shape)],
        out_specs=[row(MEM_DIM), row(MEM_DIM)],
        out_shape=[jax.ShapeDtypeStruct((rows, MEM_DIM), BF16)] * 2,
        compiler_params=_params("parallel"),
        name="mem_kv",
    )(mem2d, gn, w, bd, gk)


def _head_lane_mask(shape, head_in_pair):
    lane = lax.broadcasted_iota(jnp.int32, shape, len(shape) - 1)
    return (lane // HEAD_DIM) == head_in_pair


def _fox_attn_kernel(tq, q_ref, k_ref, kb_ref, v_ref, o_ref, m_ref, acc_ref):
    i = pl.program_id(2)
    tk = tq
    q2 = q_ref[0]
    zero = jnp.zeros_like(q2)
    lane = lax.broadcasted_iota(jnp.int32, q2.shape, 1)
    q_stack = jnp.concatenate(
        [jnp.concatenate(
            [jnp.where(_head_lane_mask(q2.shape, hh), q2, zero),
             ((lane >= BIAS_PARTS * hh) & (lane < BIAS_PARTS * (hh + 1))).astype(BF16)], axis=1)
         for hh in range(2)], axis=0)
    m_ref[...] = jnp.full(m_ref.shape, NEG_INF, F32)
    acc_ref[...] = jnp.zeros(acc_ref.shape, F32)
    on_or_below_diag = (lax.broadcasted_iota(jnp.int32, (tq, tk), 1)
                        <= lax.broadcasted_iota(jnp.int32, (tq, tk), 0))

    def key_rows(j):
        return pl.ds(pl.multiple_of(j * tk, tk), tk)

    def block(j, masked):
        rows = key_rows(j)
        kb = jnp.concatenate([k_ref[0, rows, :], kb_ref[0, rows, :]], axis=1)
        s2 = lax.dot_general(q_stack, kb, (((1,), (1,)), ((), ())), preferred_element_type=F32)
        vb = v_ref[0, rows, :]
        for hh in range(2):
            s = s2[hh * tq:(hh + 1) * tq, :]
            if masked:
                s = jnp.where(on_or_below_diag, s, NEG_INF)
            m_old = m_ref[hh]
            m_new = jnp.maximum(m_old, jnp.max(s, axis=-1, keepdims=True))
            alpha = jnp.exp2(m_old - m_new)
            p = jnp.concatenate([jnp.exp2(s[:, t0:t0 + LANES] - m_new)
                                 for t0 in range(0, tk, LANES)], axis=1).astype(BF16)
            v_h = jnp.where(_head_lane_mask(vb.shape, hh), vb, jnp.ones_like(vb))
            acc_ref[hh] = alpha * acc_ref[hh] + jnp.dot(p, v_h, preferred_element_type=F32)
            m_ref[hh] = m_new

    def pair(j2, _):
        block(2 * j2, masked=False)
        block(2 * j2 + 1, masked=False)
        return 0

    lax.fori_loop(0, i // 2, pair, 0)

    @pl.when(i % 2 == 1)
    def _():
        block(i - 1, masked=False)

    block(i, masked=True)
    acc0, acc1 = acc_ref[0], acc_ref[1]
    out0 = acc0 * (1.0 / acc0[:, HEAD_DIM:HEAD_DIM + 1])
    out1 = acc1 * (1.0 / acc1[:, 0:1])
    o_ref[0] = jnp.where(_head_lane_mask(out0.shape, 0), out0, out1).astype(o_ref.dtype)


def _fox_attn(q, k, kbias, v):
    b, s, _ = q.shape
    tq = min(ATTN_TILE, s)
    assert s % tq == 0
    seq = pl.BlockSpec((1, s, LANES), lambda bi, hp, i: (bi, 0, hp))
    tile = pl.BlockSpec((1, tq, LANES), lambda bi, hp, i: (bi, i, hp))
    return pl.pallas_call(
        functools.partial(_fox_attn_kernel, tq),
        grid=(b, FOX_HEADS // 2, s // tq),
        in_specs=[tile, seq, seq, seq],
        out_specs=tile,
        out_shape=jax.ShapeDtypeStruct((b, s, MIX_DIM), BF16),
        scratch_shapes=[pltpu.VMEM((2, tq, LANES), F32)] * 2,
        compiler_params=_params("parallel", "parallel", "arbitrary"),
        name="fox_attn",
    )(q, k, kbias, v)


def _mem_attn_kernel(q_ref, k_ref, v_ref, o_ref):
    for pair in range(MEM_DIM // LANES):
        lanes = slice(pair * LANES, (pair + 1) * LANES)
        q2 = q_ref[:, lanes]
        k2 = k_ref[:, lanes]
        v2 = v_ref[:, lanes]
        outs = []
        for hh in range(2):
            qh = jnp.where(_head_lane_mask(q2.shape, hh), q2, jnp.zeros_like(q2))
            s = lax.dot_general(qh, k2, (((1,), (1,)), ((), ())), preferred_element_type=F32)
            p = jnp.exp(s - jnp.max(s, axis=-1, keepdims=True))
            l = jnp.sum(p, axis=-1, keepdims=True)
            outs.append(jnp.dot(p.astype(BF16), v2, preferred_element_type=F32) * (1.0 / l))
        o_ref[:, lanes] = jnp.where(_head_lane_mask(outs[0].shape, 0), outs[0],
                                    outs[1]).astype(o_ref.dtype)


def _mem_attn(mq, mk, mv, seq_len, mem_len):
    t = mq.shape[0]
    tm = min(ROW_TILE, seq_len)
    per_seq = seq_len // tm
    kv = pl.BlockSpec((mem_len, MEM_DIM), lambda i: (i // per_seq, 0))
    return pl.pallas_call(
        _mem_attn_kernel,
        grid=(t // tm,),
        in_specs=[pl.BlockSpec((tm, MEM_DIM), lambda i: (i, 0)), kv, kv],
        out_specs=pl.BlockSpec((tm, MEM_DIM), lambda i: (i, 0)),
        out_shape=jax.ShapeDtypeStruct((t, MEM_DIM), BF16),
        compiler_params=_params("parallel"),
        name="mem_attn",
    )(mq, mk, mv)


def _conv_kernel(ts, u_ref, dw_ref, dwb_ref, lng_ref, lnb_ref, o_ref, ext_ref, sh_ref, acc_ref):
    @pl.when(pl.program_id(1) == 0)
    def _():
        ext_ref[0:CONV_HALO, :] = jnp.zeros((CONV_HALO, MIX_DIM), F32)

    @pl.when(pl.program_id(1) > 0)
    def _():
        ext_ref[0:CONV_HALO, :] = ext_ref[ts:ts + CONV_HALO, :]

    ext_ref[CONV_HALO:CONV_HALO + ts, :] = u_ref[0]

    base = CONV_HALO - (CONV_WIDTH - 1)
    n_shift = SUBLANES
    for b in range(n_shift):
        n_rows = ts + ((CONV_WIDTH - 1 - b) // n_shift) * n_shift
        sh_ref[b, 0:n_rows, :] = ext_ref[base + b:base + b + n_rows, :]
    rc = min(CONV_ROW_CHUNK, ts)
    for r0 in range(0, ts, rc):
        for l0 in range(0, MIX_DIM, LANES):
            groups = (rc // SUBLANES, SUBLANES, LANES)
            acc = jnp.zeros(groups, F32) + dwb_ref[:, l0:l0 + LANES]
            for j in range(CONV_WIDTH):
                a, b = divmod(j, n_shift)
                x = sh_ref[b, r0 + a * n_shift:r0 + a * n_shift + rc, l0:l0 + LANES]
                w = jnp.broadcast_to(dw_ref[j, :, l0:l0 + LANES][None], groups)
                acc = acc + w * x.reshape(groups)
            acc_ref[r0:r0 + rc, l0:l0 + LANES] = acc.reshape(rc, LANES)

    y = acc_ref[...]
    mu = jnp.mean(y, axis=-1, keepdims=True)
    yc = y - mu
    var = jnp.mean(yc * yc, axis=-1, keepdims=True)
    z = yc * lax.rsqrt(var + EPS) * lng_ref[...] + lnb_ref[...]
    o_ref[0] = _silu(z).astype(o_ref.dtype)


def _conv_module(u, dw, dwb, lng, lnb):
    b, s, _ = u.shape
    ts = min(CONV_TILE, s)
    assert s % ts == 0 and ts >= CONV_HALO
    blk = pl.BlockSpec((1, ts, MIX_DIM), lambda bi, i: (bi, i, 0))
    return pl.pallas_call(
        functools.partial(_conv_kernel, ts),
        grid=(b, s // ts),
        in_specs=[blk, _resident(dw.shape), _resident(dwb.shape), _resident(lng.shape),
                  _resident(lnb.shape)],
        out_specs=blk,
        out_shape=jax.ShapeDtypeStruct((b, s, MIX_DIM), BF16),
        scratch_shapes=[pltpu.VMEM((ts + CONV_HALO, MIX_DIM), F32),
                        pltpu.VMEM((SUBLANES, ts + CONV_HALO - SUBLANES, MIX_DIM), F32),
                        pltpu.VMEM((ts, MIX_DIM), F32)],
        compiler_params=_params("parallel", "arbitrary"),
        name="conv_module",
    )(u, dw, dwb, lng, lnb)


def _outproj_math(h_ref, mix_ref, mem_ref, w_ref, gn_ref):
    h2 = (h_ref[...]
          + jnp.dot(mix_ref[...], w_ref[0:MIX_DIM, :], preferred_element_type=F32)
          + jnp.dot(mem_ref[...], w_ref[MIX_DIM:, :], preferred_element_type=F32))
    return h2, _rms(h2, gn_ref[...])


def _outproj_dense_kernel(h_ref, mix_ref, mem_ref, w_ref, gn_ref, h2_ref, z_ref):
    h2, z = _outproj_math(h_ref, mix_ref, mem_ref, w_ref, gn_ref)
    h2_ref[...] = h2
    z_ref[...] = z.astype(BF16)


def _outproj_router_kernel(h_ref, mix_ref, mem_ref, w_ref, gn_ref, wr_ref,
                           h2_ref, zp_ref, route_ref):
    h2, z = _outproj_math(h_ref, mix_ref, mem_ref, w_ref, gn_ref)
    h2_ref[...] = h2

    half = D_MODEL // 2
    lo_bits = pltpu.bitcast(z[:, :half].astype(BF16).astype(F32), jnp.uint32) >> 16
    hi_bits = pltpu.bitcast(z[:, half:].astype(BF16).astype(F32), jnp.uint32) & jnp.uint32(0xFFFF0000)
    packed = lo_bits | hi_bits
    for j in range(half // LANES):
        zp_ref[pl.ds(j, packed.shape[0], stride=PACKED_SLABS), :] = packed[:, j * LANES:(j + 1) * LANES]

    z_hi, z_lo = _split_bf16(z, 2)
    wr_hi, wr_lo = _split_bf16(wr_ref[...], 2)
    logits = (jnp.dot(z_hi, wr_hi, preferred_element_type=F32)
              + jnp.dot(z_lo, wr_hi, preferred_element_type=F32)
              + jnp.dot(z_hi, wr_lo, preferred_element_type=F32))
    lane = lax.broadcasted_iota(jnp.int32, logits.shape, 1)
    lane_f = lane.astype(F32)
    logits = jnp.where(lane < N_EXPERTS, logits, -jnp.inf)
    l1 = jnp.max(logits, axis=-1, keepdims=True)
    e1 = jnp.min(jnp.where(logits == l1, lane_f, float(LANES)), axis=-1, keepdims=True)
    rest = jnp.where(lane_f == e1, -jnp.inf, logits)
    l2 = jnp.max(rest, axis=-1, keepdims=True)
    e2 = jnp.min(jnp.where(rest == l2, lane_f, float(LANES)), axis=-1, keepdims=True)
    g2 = 1.0 / (1.0 + jnp.exp(l1 - l2))
    g1 = 1.0 - g2
    route = jnp.where(lane == 0, g1, 0.0) + jnp.where(lane == 1, g2, 0.0)
    route = route + jnp.where(lane == 2, e1, 0.0)
    route_ref[...] = route + jnp.where(lane == 3, e2, 0.0)


def _outproj(h, mix, mem, w, gn, router=None):
    t = h.shape[0]
    tm = min(ROW_TILE, t)
    assert t % tm == 0
    row = lambda n: pl.BlockSpec((tm, n), lambda i: (i, 0))
    in_specs = [row(D_MODEL), row(MIX_DIM), row(MEM_DIM), _resident(w.shape), _resident(gn.shape)]
    if router is None:
        return pl.pallas_call(
            _outproj_dense_kernel, grid=(t // tm,), in_specs=in_specs,
            out_specs=[row(D_MODEL), row(D_MODEL)],
            out_shape=[jax.ShapeDtypeStruct((t, D_MODEL), F32),
                       jax.ShapeDtypeStruct((t, D_MODEL), BF16)],
            compiler_params=_params("parallel"), name="outproj_dense",
        )(h, mix, mem, w, gn)
    return pl.pallas_call(
        _outproj_router_kernel, grid=(t // tm,),
        in_specs=in_specs + [_resident(router.shape)],
        out_specs=[row(D_MODEL), pl.BlockSpec((tm * PACKED_SLABS, LANES), lambda i: (i, 0)),
                   row(LANES)],
        out_shape=[jax.ShapeDtypeStruct((t, D_MODEL), F32),
                   jax.ShapeDtypeStruct((t * PACKED_SLABS, LANES), jnp.uint32),
                   jax.ShapeDtypeStruct((t, LANES), F32)],
        compiler_params=_params("parallel"), name="outproj_router",
    )(h, mix, mem, w, gn, router)


def _ffn_kernel(h_ref, z_ref, wg_ref, wu_ref, wd_ref, o_ref, acc_ref):
    z = z_ref[...]
    d_ff = wg_ref.shape[1]
    acc_ref[...] = h_ref[...]
    for f0 in range(0, d_ff, FFN_CHUNK):
        g = jnp.dot(z, wg_ref[:, f0:f0 + FFN_CHUNK], preferred_element_type=F32)
        u = jnp.dot(z, wu_ref[:, f0:f0 + FFN_CHUNK], preferred_element_type=F32)
        a = (_silu(g) * u).astype(BF16)
        acc_ref[...] += jnp.dot(a, wd_ref[f0:f0 + FFN_CHUNK, :], preferred_element_type=F32)
    o_ref[...] = acc_ref[...]


def _ffn(h2, z, wg, wu, wd):
    t = h2.shape[0]
    tm = min(ROW_TILE, t)
    assert t % tm == 0 and wg.shape[1] % FFN_CHUNK == 0
    row = pl.BlockSpec((tm, D_MODEL), lambda i: (i, 0))
    return pl.pallas_call(
        _ffn_kernel, grid=(t // tm,),
        in_specs=[row, row, _resident(wg.shape), _resident(wu.shape), _resident(wd.shape)],
        out_specs=row,
        out_shape=jax.ShapeDtypeStruct((t, D_MODEL), F32),
        scratch_shapes=[pltpu.VMEM((tm, D_MODEL), F32)],
        compiler_params=_params("parallel"), name="ffn_dense",
    )(h2, z, wg, wu, wd)


def _token_copy(src, src_row, dst, dst_row, slabs, sem):
    return pltpu.make_async_copy(src.at[pl.ds(src_row, slabs), :], dst.at[pl.ds(dst_row, slabs), :],
                                 sem)


def _block_copy(src, dst, slabs, sem):
    n = EXPERT_ROWS * slabs
    return pltpu.make_async_copy(src.at[pl.ds(0, n), :], dst.at[pl.ds(0, n), :], sem)


def _experts_kernel(n_f, be_ref, nact_ref, tok_cur_ref, tok_next_ref, dst_prev_ref, dst_cur_ref,
                    zp_hbm, wg_ref, wu_ref, wd_ref, y_hbm, xp_ref, x_ref, acc_ref, stage_ref,
                    sem_in, sem_out):
    del be_ref
    i = pl.program_id(0)
    f = pl.program_id(1)
    n_active = nact_ref[0]
    active = i < n_active
    has_next = i + 1 < n_active
    has_prev = i >= 1
    first_step = f == 0
    last_step = f == n_f - 1
    half = D_MODEL // 2

    def gather(idx_ref, r):
        src = pl.multiple_of(idx_ref[0, 0, r], PACKED_SLABS)
        _token_copy(zp_hbm, src, xp_ref, r * PACKED_SLABS, PACKED_SLABS, sem_in).start()

    def scatter(idx_ref, r):
        dst = pl.multiple_of(idx_ref[0, 0, r], OUT_SLABS)
        _token_copy(stage_ref, r * OUT_SLABS, y_hbm, dst, OUT_SLABS, sem_out).start()

    gather_next = functools.partial(gather, tok_next_ref)
    scatter_prev = functools.partial(scatter, dst_prev_ref)

    def rolled(fn):
        def body(r, _):
            fn(r)
            return 0
        lax.fori_loop(0, EXPERT_ROWS, body, 0)

    def compute(phase, inline_dma=None):
        def sub_block(s, _):
            r0 = pl.multiple_of(s * EXPERT_SUB, EXPERT_SUB)
            rows = pl.ds(r0, EXPERT_SUB)
            x = x_ref[rows, :]
            g = jnp.dot(x, wg_ref[0, 0], preferred_element_type=F32)
            u = jnp.dot(x, wu_ref[0, 0], preferred_element_type=F32)
            a = (_silu(g) * u).astype(BF16)
            c = jnp.dot(a, wd_ref[0, 0], preferred_element_type=F32)
            if phase == "first":
                acc_ref[rows, :] = c
            elif phase == "mid":
                acc_ref[rows, :] += c
            else:
                out = acc_ref[rows, :] + c
                for j in range(OUT_SLABS):
                    stage_ref[pl.ds(r0 * OUT_SLABS + j, EXPERT_SUB, stride=OUT_SLABS), :] = (
                        out[:, j * LANES:(j + 1) * LANES])
            if inline_dma is not None:
                for k in range(EXPERT_SUB):
                    inline_dma(r0 + k)
            return 0
        lax.fori_loop(0, EXPERT_ROWS // EXPERT_SUB, sub_block, 0)

    @pl.when((i == 0) & first_step)
    def _():
        stage_ref[...] = jnp.zeros(stage_ref.shape, F32)
        n_sink = EXPERT_ROWS * OUT_SLABS
        sink = pltpu.make_async_copy(
            stage_ref, y_hbm.at[pl.ds(y_hbm.shape[0] - n_sink, n_sink), :], sem_out)
        sink.start()
        sink.wait()
        rolled(functools.partial(gather, tok_cur_ref))

    @pl.when(active & first_step)
    def _():
        _block_copy(zp_hbm, xp_ref, PACKED_SLABS, sem_in).wait()
        for j in range(PACKED_SLABS):
            xp = xp_ref[pl.ds(j, EXPERT_ROWS, stride=PACKED_SLABS), :]
            lanes = slice(j * LANES, (j + 1) * LANES)
            x_ref[:, lanes] = pltpu.bitcast(xp << 16, F32).astype(BF16)
            x_ref[:, half + j * LANES:half + (j + 1) * LANES] = pltpu.bitcast(
                xp & jnp.uint32(0xFFFF0000), F32).astype(BF16)

    @pl.when(active & first_step & has_prev)
    def _():
        compute("first", scatter_prev)

    @pl.when(active & first_step & jnp.logical_not(has_prev))
    def _():
        compute("first")

    if n_f > 2:
        @pl.when(active & jnp.logical_not(first_step) & jnp.logical_not(last_step))
        def _():
            compute("mid")

    @pl.when(active & last_step & has_prev)
    def _():
        _block_copy(stage_ref, y_hbm, OUT_SLABS, sem_out).wait()

    @pl.when(active & last_step & has_next)
    def _():
        compute("last", gather_next)

    @pl.when(active & last_step & jnp.logical_not(has_next))
    def _():
        compute("last")
        rolled(functools.partial(scatter, dst_cur_ref))
        _block_copy(stage_ref, y_hbm, OUT_SLABS, sem_out).wait()


def _experts(zp, plan, layer, wg, wu, wd):
    t = zp.shape[0] // PACKED_SLABS
    blk_expert, n_active, tok_buf, dst_buf = plan
    nb = blk_expert.shape[0]
    d_ff = wg.shape[3]
    tf = EXPERT_FF_TILE
    n_f = d_ff // tf
    assert d_ff % tf == 0 and n_f >= 2
    last = nb - 1

    def idx(shift):
        return pl.BlockSpec((1, 1, EXPERT_ROWS),
                            lambda i, f, *_: (jnp.clip(i + shift, 0, last), 0, 0),
                            memory_space=pltpu.SMEM)

    grid_spec = pltpu.PrefetchScalarGridSpec(
        num_scalar_prefetch=2,
        grid=(nb, n_f),
        in_specs=[idx(0), idx(1), idx(-1), idx(0), pl.BlockSpec(memory_space=pl.ANY),
                  pl.BlockSpec((1, 1, D_MODEL, tf), lambda i, f, be, *_: (layer, be[i], 0, f)),
                  pl.BlockSpec((1, 1, D_MODEL, tf), lambda i, f, be, *_: (layer, be[i], 0, f)),
                  pl.BlockSpec((1, 1, tf, D_MODEL), lambda i, f, be, *_: (layer, be[i], f, 0))],
        out_specs=pl.BlockSpec(memory_space=pl.ANY),
        scratch_shapes=[pltpu.VMEM((EXPERT_ROWS * PACKED_SLABS, LANES), jnp.uint32),
                        pltpu.VMEM((EXPERT_ROWS, D_MODEL), BF16),
                        pltpu.VMEM((EXPERT_ROWS, D_MODEL), F32),
                        pltpu.VMEM((EXPERT_ROWS * OUT_SLABS, LANES), F32),
                        pltpu.SemaphoreType.DMA(()), pltpu.SemaphoreType.DMA(())],
    )
    return pl.pallas_call(
        functools.partial(_experts_kernel, n_f), grid_spec=grid_spec,
        out_shape=jax.ShapeDtypeStruct(((TOP_K * t + EXPERT_ROWS) * OUT_SLABS, LANES), F32),
        compiler_params=_params("arbitrary", "arbitrary"), name="moe_experts",
    )(blk_expert, n_active, tok_buf, tok_buf, dst_buf, dst_buf, zp, wg, wu, wd)


def _route_plan(route, t):
    a = t * TOP_K
    nb = a // EXPERT_ROWS + N_EXPERTS
    rows = nb * EXPERT_ROWS
    flat_e = route[:, 2:4].astype(jnp.int32).reshape(a)
    experts = jnp.arange(N_EXPERTS, dtype=jnp.int32)
    sizes = jnp.sum((flat_e[:, None] == experts[None, :]).astype(jnp.int32), axis=0)
    start = jnp.cumsum(sizes) - sizes
    padded = ((sizes + EXPERT_ROWS - 1) // EXPERT_ROWS) * EXPERT_ROWS
    pad_end = jnp.cumsum(padded)
    pad_start = pad_end - padded
    sorted_assign = jnp.sort(flat_e * a + jnp.arange(a, dtype=jnp.int32)) % a
    blk_start = jnp.arange(nb, dtype=jnp.int32) * EXPERT_ROWS
    blk_expert = jnp.minimum(jnp.searchsorted(pad_end, blk_start, side="right"),
                             N_EXPERTS - 1).astype(jnp.int32)
    row = jnp.arange(rows, dtype=jnp.int32)
    row_e = jnp.repeat(blk_expert, EXPERT_ROWS)
    rank = row - pad_start[row_e]
    real = (rank >= 0) & (rank < sizes[row_e])
    assign = sorted_assign[jnp.clip(start[row_e] + rank, 0, a - 1)]
    tok = assign // TOP_K
    tok_buf = jnp.where(real, tok, 0)
    dst_buf = jnp.where(real, tok + (assign % TOP_K) * t, TOP_K * t + row % EXPERT_ROWS)
    n_active = (pad_end[-1] // EXPERT_ROWS).reshape(1).astype(jnp.int32)
    return (blk_expert, n_active, (tok_buf * PACKED_SLABS).reshape(nb, 1, EXPERT_ROWS),
            (dst_buf * OUT_SLABS).reshape(nb, 1, EXPERT_ROWS))


def _combine_kernel(h_ref, y0_ref, y1_ref, route_ref, o_ref):
    g = route_ref[...]
    for j in range(OUT_SLABS):
        lanes = slice(j * LANES, (j + 1) * LANES)
        rows = pl.ds(j, h_ref.shape[0], stride=OUT_SLABS)
        o_ref[:, lanes] = h_ref[:, lanes] + (g[:, 0:1] * y0_ref[rows, :] + g[:, 1:2] * y1_ref[rows, :])


def _combine(h2, y, route):
    t = h2.shape[0]
    tm = min(ROW_TILE, t)
    nt = t // tm
    row = pl.BlockSpec((tm, D_MODEL), lambda i: (i, 0))
    return pl.pallas_call(
        _combine_kernel, grid=(nt,),
        in_specs=[row, pl.BlockSpec((tm * OUT_SLABS, LANES), lambda i: (i, 0)),
                  pl.BlockSpec((tm * OUT_SLABS, LANES), lambda i: (i + nt, 0)),
                  pl.BlockSpec((tm, LANES), lambda i: (i, 0))],
        out_specs=row,
        out_shape=jax.ShapeDtypeStruct((t, D_MODEL), F32),
        compiler_params=_params("parallel"), name="moe_combine",
    )(h2, y, y, route)


def _row(v):
    return v.reshape(1, -1).astype(F32)


def _tile_heads(g, n, scale=1.0):
    return _row(jnp.tile(g.astype(F32) * scale, n))


def kernel(x, mem, norm_mix, norm_mem, norm_ffn, w_mem_kv, g_mq, g_mk, fox_w_in, fox_b_f, fox_g_q, fox_g_k, fox_w_out, conv_w_in, conv_b_in, conv_dw, conv_dw_b, conv_ln_g, conv_ln_b, conv_w_out, ffn_w_gate, ffn_w_up, ffn_w_down, moe_router, moe_w_gate, moe_w_up, moe_w_down):
    b, s, d = x.shape
    mem_len = mem.shape[1]
    t = b * s
    assert d == D_MODEL
    scale = HEAD_DIM ** -0.5

    group = jnp.arange(MIX_DIM, dtype=jnp.int32) // HEAD_DIM
    ones_bd = (group[:, None] == group[None, :]).astype(BF16)
    ones_bd_mem = ones_bd[:MEM_DIM, :MEM_DIM]

    moe_wg, moe_wu, moe_wd = (w.astype(BF16) for w in (moe_w_gate, moe_w_up, moe_w_down))
    h = x.reshape(t, d)
    mem2d = mem.reshape(b * mem_len, d)
    for i in range(DEPTH):
        j = i // 2
        gmq = _tile_heads(g_mq[i], MEM_DIM // HEAD_DIM, scale)
        mk, mv = _mem_kv(mem2d, mem_len, _row(norm_mem[i]), w_mem_kv[i].astype(BF16),
                         ones_bd_mem, _tile_heads(g_mk[i], MEM_DIM // HEAD_DIM))
        if i % 2 == 0:
            w = fox_w_in[j]
            m3 = 3 * MIX_DIM
            w_cat = jnp.concatenate(
                [w[:, :m3], w[:, m3 + FOX_HEADS:], w[:, m3:m3 + FOX_HEADS],
                 jnp.zeros((d, F_LANES - FOX_HEADS), w.dtype)], axis=1).astype(BF16)
            bf = jnp.pad(fox_b_f[j].astype(F32), (0, F_LANES - FOX_HEADS)).reshape(1, F_LANES)
            q, k, v, mq, kbias = _fox_inproj(
                h, s, _row(norm_mix[i]), w_cat, ones_bd,
                _tile_heads(fox_g_q[j], FOX_HEADS, scale * LOG2E),
                _tile_heads(fox_g_k[j], FOX_HEADS), gmq, bf)
            seq = lambda a: a.reshape(b, s, MIX_DIM)
            mix = _fox_attn(seq(q), seq(k), seq(kbias), seq(v)).reshape(t, MIX_DIM)
            w_out = fox_w_out[j]
        else:
            u, mq = _conv_inproj(h, s, _row(norm_mix[i]), conv_w_in[j].astype(BF16),
                                 _row(conv_b_in[j]), ones_bd_mem, gmq)
            dw8 = jnp.broadcast_to(conv_dw[j].astype(F32)[:, None, :],
                                   (CONV_WIDTH, SUBLANES, MIX_DIM))
            mix = _conv_module(u.reshape(b, s, MIX_DIM), dw8,
                               _row(conv_dw_b[j]), _row(conv_ln_g[j]),
                               _row(conv_ln_b[j])).reshape(t, MIX_DIM)
            w_out = conv_w_out[j]
        mem_out = _mem_attn(mq, mk, mv, s, mem_len)
        if i % 2 == 0:
            h2, z = _outproj(h, mix, mem_out, w_out.astype(BF16), _row(norm_ffn[i]))
            h = _ffn(h2, z, ffn_w_gate[j].astype(BF16), ffn_w_up[j].astype(BF16),
                     ffn_w_down[j].astype(BF16))
        else:
            wr = jnp.pad(moe_router[j].astype(F32), ((0, 0), (0, LANES - N_EXPERTS)))
            h2, zp, route = _outproj(h, mix, mem_out, w_out.astype(BF16), _row(norm_ffn[i]),
                                     router=wr)
            plan = _route_plan(route, t)
            y = _experts(zp, plan, j, moe_wg, moe_wu, moe_wd)
            h = _combine(h2, y, route)
    return h.reshape(b, s, d)
```

```python
import functools

import jax
import jax.numpy as jnp
from jax import lax
from jax.experimental import pallas as pl
from jax.experimental.pallas import tpu as pltpu

F32 = jnp.float32
BF16 = jnp.bfloat16

D_MODEL = 1024
HEAD_DIM = 64
MEM_DIM = 256
MIX_DIM = D_MODEL - MEM_DIM
FOX_HEADS = MIX_DIM // HEAD_DIM
CONV_WIDTH = 31
N_EXPERTS = 8
TOP_K = 2
DEPTH = 4
EPS = 1e-6
NEG_INF = -1e30
LOG2E = 1.4426950408889634
BIAS_PARTS = 3
PART_STRIDE = 16

LANES = 128
SUBLANES = 8
MXU_WIDTH = 256
VMEM_LIMIT_BYTES = 56 * 1024 * 1024

ROW_TILE = 512
ATTN_TILE = 512
CONV_TILE = 256
CONV_HALO = 32
CONV_ROW_CHUNK = 64
FFN_CHUNK = 256
EXPERT_ROWS = 1024
EXPERT_SUB = 256
EXPERT_FF_TILE = 1792
F_LANES = LANES
PACKED_SLABS = D_MODEL // 2 // LANES
OUT_SLABS = D_MODEL // LANES


def _params(*sem):
    return pltpu.CompilerParams(dimension_semantics=sem, vmem_limit_bytes=VMEM_LIMIT_BYTES)


def _resident(shape):
    nd = len(shape)
    return pl.BlockSpec(shape, lambda *_: (0,) * nd, pipeline_mode=pl.Buffered(1))


def _rms(x, g):
    ms = jnp.mean(x * x, axis=-1, keepdims=True)
    return x * lax.rsqrt(ms + EPS) * g


def _head_rms(x, ones_bd, g):
    w = ones_bd.shape[0]
    sq = (x * x).astype(BF16)
    ss = jnp.concatenate([jnp.dot(sq[:, c:c + w], ones_bd, preferred_element_type=F32)
                          for c in range(0, x.shape[1], w)], axis=1)
    return x * lax.rsqrt(ss * (1.0 / HEAD_DIM) + EPS) * g


def _silu(x):
    return x * (1.0 / (1.0 + jnp.exp(-x)))


def _log_sigmoid(x):
    return jnp.minimum(x, 0.0) - jnp.log(1.0 + jnp.exp(-jnp.abs(x)))


def _row_prefix_sum(x):
    n = x.shape[0]
    row = lax.broadcasted_iota(jnp.int32, x.shape, 0)
    s = 1
    while s < n:
        x = x + jnp.where(row >= s, pltpu.roll(x, s, 0), 0.0)
        s *= 2
    return x


def _split_bf16(x, parts):
    out = []
    for _ in range(parts - 1):
        hi = x.astype(BF16)
        out.append(hi)
        x = x - hi.astype(F32)
    out.append(x.astype(BF16))
    return out


def _fox_inproj_kernel(tiles_per_seq, h_ref, gn_ref, w_ref, bd_ref, gq_ref, gk_ref, gmq_ref,
                       bf_ref, place_ref, q_ref, k_ref, v_ref, mq_ref, kb_ref, carry_ref):
    xb = _rms(h_ref[...], gn_ref[...]).astype(BF16)
    m = MIX_DIM
    q = jnp.dot(xb, w_ref[:, 0:m], preferred_element_type=F32)
    q_ref[...] = _head_rms(q, bd_ref[...], gq_ref[...]).astype(BF16)
    k = jnp.dot(xb, w_ref[:, m:2 * m], preferred_element_type=F32)
    k_ref[...] = _head_rms(k, bd_ref[...], gk_ref[...]).astype(BF16)
    v_ref[...] = jnp.dot(xb, w_ref[:, 2 * m:3 * m], preferred_element_type=F32).astype(BF16)
    mq = jnp.dot(xb, w_ref[:, 3 * m:3 * m + MEM_DIM], preferred_element_type=F32)
    mq_ref[...] = _head_rms(mq, bd_ref[...], gmq_ref[...]).astype(BF16)

    f = jnp.dot(xb, w_ref[:, 3 * m + MEM_DIM:], preferred_element_type=F32) + bf_ref[...]
    lane = lax.broadcasted_iota(jnp.int32, f.shape, 1)
    lf = jnp.where(lane < FOX_HEADS, _log_sigmoid(f), 0.0)

    @pl.when(pl.program_id(0) % tiles_per_seq == 0)
    def _():
        carry_ref[...] = jnp.zeros_like(carry_ref)

    c = _row_prefix_sum(lf) + carry_ref[0:1, :]
    carry_ref[0:1, :] = c[-1:, :]
    parts = _split_bf16(c * (-LOG2E), BIAS_PARTS)
    parked = parts[0].astype(F32)
    for p in range(1, BIAS_PARTS):
        parked = parked + pltpu.roll(parts[p].astype(F32), PART_STRIDE * p, 1)
    kb_ref[...] = jnp.dot(parked.astype(BF16), place_ref[...],
                          preferred_element_type=F32).astype(BF16)


def _bias_placement():
    row = jnp.arange(F_LANES, dtype=jnp.int32)[:, None]
    p, h = row // PART_STRIDE, row % PART_STRIDE
    col = jnp.arange(MIX_DIM, dtype=jnp.int32)[None, :]
    return ((p < BIAS_PARTS) & (h < FOX_HEADS)
            & (col == (h // 2) * LANES + BIAS_PARTS * (h % 2) + p)).astype(BF16)


def _fox_inproj(h, seq_len, gn, w, bd, gq, gk, gmq, bf):
    t = h.shape[0]
    tm = min(ROW_TILE, seq_len)
    assert seq_len % tm == 0
    row = lambda n: pl.BlockSpec((tm, n), lambda i: (i, 0))
    place = _bias_placement()
    return pl.pallas_call(
        functools.partial(_fox_inproj_kernel, seq_len // tm),
        grid=(t // tm,),
        in_specs=[row(D_MODEL), _resident(gn.shape), _resident(w.shape), _resident(bd.shape),
                  _resident(gq.shape), _resident(gk.shape), _resident(gmq.shape),
                  _resident(bf.shape), _resident(place.shape)],
        out_specs=[row(MIX_DIM), row(MIX_DIM), row(MIX_DIM), row(MEM_DIM), row(MIX_DIM)],
        out_shape=[jax.ShapeDtypeStruct((t, MIX_DIM), BF16)] * 3
        + [jax.ShapeDtypeStruct((t, MEM_DIM), BF16), jax.ShapeDtypeStruct((t, MIX_DIM), BF16)],
        scratch_shapes=[pltpu.VMEM((SUBLANES, F_LANES), F32)],
        compiler_params=_params("arbitrary"),
        name="fox_inproj",
    )(h, gn, w, bd, gq, gk, gmq, bf, place)


def _conv_inproj_kernel(h_ref, gn_ref, w_ref, b_ref, bd_ref, gmq_ref, u_ref, mq_ref):
    xb = _rms(h_ref[...], gn_ref[...]).astype(BF16)
    m = MIX_DIM
    a = jnp.dot(xb, w_ref[:, 0:m], preferred_element_type=F32) + b_ref[:, 0:m]
    g = jnp.dot(xb, w_ref[:, m:2 * m], preferred_element_type=F32) + b_ref[:, m:2 * m]
    u_ref[...] = a * (1.0 / (1.0 + jnp.exp(-g)))
    mq = jnp.dot(xb, w_ref[:, 2 * m:], preferred_element_type=F32)
    mq_ref[...] = _head_rms(mq, bd_ref[...], gmq_ref[...]).astype(BF16)


def _conv_inproj(h, seq_len, gn, w, b, bd, gmq):
    t = h.shape[0]
    tm = min(ROW_TILE, seq_len)
    assert t % tm == 0
    row = lambda n: pl.BlockSpec((tm, n), lambda i: (i, 0))
    return pl.pallas_call(
        _conv_inproj_kernel,
        grid=(t // tm,),
        in_specs=[row(D_MODEL), _resident(gn.shape), _resident(w.shape), _resident(b.shape),
                  _resident(bd.shape), _resident(gmq.shape)],
        out_specs=[row(MIX_DIM), row(MEM_DIM)],
        out_shape=[jax.ShapeDtypeStruct((t, MIX_DIM), F32),
                   jax.ShapeDtypeStruct((t, MEM_DIM), BF16)],
        compiler_params=_params("parallel"),
        name="conv_inproj",
    )(h, gn, w, b, bd, gmq)


def _mem_kv_kernel(m_ref, gn_ref, w_ref, bd_ref, gk_ref, mk_ref, mv_ref):
    xb = _rms(m_ref[...], gn_ref[...]).astype(BF16)
    mk = jnp.dot(xb, w_ref[:, 0:MEM_DIM], preferred_element_type=F32)
    mk_ref[...] = _head_rms(mk, bd_ref[...], gk_ref[...]).astype(BF16)
    mv_ref[...] = jnp.dot(xb, w_ref[:, MEM_DIM:], preferred_element_type=F32).astype(BF16)


def _mem_kv(mem2d, mem_len, gn, w, bd, gk):
    rows = mem2d.shape[0]
    row = lambda n: pl.BlockSpec((mem_len, n), lambda i: (i, 0))
    return pl.pallas_call(
        _mem_kv_kernel,
        grid=(rows // mem_len,),
        in_specs=[row(D_MODEL), _resident(gn.shape), _resident(w.shape), _resident(bd.shape),
                  _resident(gk.shape)],
        out_specs=[row(MEM_DIM), row(MEM_DIM)],
        out_shape=[jax.ShapeDtypeStruct((rows, MEM_DIM), BF16)] * 2,
        compiler_params=_params("parallel"),
        name="mem_kv",
    )(mem2d, gn, w, bd, gk)


def _head_lane_mask(shape, head_in_pair):
    lane = lax.broadcasted_iota(jnp.int32, shape, len(shape) - 1)
    return (lane // HEAD_DIM) == head_in_pair


def _fox_attn_kernel(tq, q_ref, k_ref, kb_ref, v_ref, o_ref, m_ref, acc_ref):
    i = pl.program_id(2)
    tk = tq
    q2 = q_ref[0]
    zero = jnp.zeros_like(q2)
    lane = lax.broadcasted_iota(jnp.int32, q2.shape, 1)
    q_stack = jnp.concatenate(
        [jnp.concatenate(
            [jnp.where(_head_lane_mask(q2.shape, hh), q2, zero),
             ((lane >= BIAS_PARTS * hh) & (lane < BIAS_PARTS * (hh + 1))).astype(BF16)], axis=1)
         for hh in range(2)], axis=0)
    m_ref[...] = jnp.full(m_ref.shape, NEG_INF, F32)
    acc_ref[...] = jnp.zeros(acc_ref.shape, F32)
    on_or_below_diag = (lax.broadcasted_iota(jnp.int32, (tq, tk), 1)
                        <= lax.broadcasted_iota(jnp.int32, (tq, tk), 0))

    def key_rows(j):
        return pl.ds(pl.multiple_of(j * tk, tk), tk)

    def block(j, masked):
        rows = key_rows(j)
        kb = jnp.concatenate([k_ref[0, rows, :], kb_ref[0, rows, :]], axis=1)
        s2 = lax.dot_general(q_stack, kb, (((1,), (1,)), ((), ())), preferred_element_type=F32)
        vb = v_ref[0, rows, :]
        for hh in range(2):
            s = s2[hh * tq:(hh + 1) * tq, :]
            if masked:
                s = jnp.where(on_or_below_diag, s, NEG_INF)
            m_old = m_ref[hh]
            m_new = jnp.maximum(m_old, jnp.max(s, axis=-1, keepdims=True))
            alpha = jnp.exp2(m_old - m_new)
            p = jnp.concatenate([jnp.exp2(s[:, t0:t0 + LANES] - m_new)
                                 for t0 in range(0, tk, LANES)], axis=1).astype(BF16)
            v_h = jnp.where(_head_lane_mask(vb.shape, hh), vb, jnp.ones_like(vb))
            acc_ref[hh] = alpha * acc_ref[hh] + jnp.dot(p, v_h, preferred_element_type=F32)
            m_ref[hh] = m_new

    def pair(j2, _):
        block(2 * j2, masked=False)
        block(2 * j2 + 1, masked=False)
        return 0

    lax.fori_loop(0, i // 2, pair, 0)

    @pl.when(i % 2 == 1)
    def _():
        block(i - 1, masked=False)

    block(i, masked=True)
    acc0, acc1 = acc_ref[0], acc_ref[1]
    out0 = acc0 * (1.0 / acc0[:, HEAD_DIM:HEAD_DIM + 1])
    out1 = acc1 * (1.0 / acc1[:, 0:1])
    o_ref[0] = jnp.where(_head_lane_mask(out0.shape, 0), out0, out1).astype(o_ref.dtype)


def _fox_attn(q, k, kbias, v):
    b, s, _ = q.shape
    tq = min(ATTN_TILE, s)
    assert s % tq == 0
    seq = pl.BlockSpec((1, s, LANES), lambda bi, hp, i: (bi, 0, hp))
    tile = pl.BlockSpec((1, tq, LANES), lambda bi, hp, i: (bi, i, hp))
    return pl.pallas_call(
        functools.partial(_fox_attn_kernel, tq),
        grid=(b, FOX_HEADS // 2, s // tq),
        in_specs=[tile, seq, seq, seq],
        out_specs=tile,
        out_shape=jax.ShapeDtypeStruct((b, s, MIX_DIM), BF16),
        scratch_shapes=[pltpu.VMEM((2, tq, LANES), F32)] * 2,
        compiler_params=_params("parallel", "parallel", "arbitrary"),
        name="fox_attn",
    )(q, k, kbias, v)


def _mem_attn_kernel(q_ref, k_ref, v_ref, o_ref):
    for pair in range(MEM_DIM // LANES):
        lanes = slice(pair * LANES, (pair + 1) * LANES)
        q2 = q_ref[:, lanes]
        k2 = k_ref[:, lanes]
        v2 = v_ref[:, lanes]
        outs = []
        for hh in range(2):
            qh = jnp.where(_head_lane_mask(q2.shape, hh), q2, jnp.zeros_like(q2))
            s = lax.dot_general(qh, k2, (((1,), (1,)), ((), ())), preferred_element_type=F32)
            p = jnp.exp(s - jnp.max(s, axis=-1, keepdims=True))
            l = jnp.sum(p, axis=-1, keepdims=True)
            outs.append(jnp.dot(p.astype(BF16), v2, preferred_element_type=F32) * (1.0 / l))
        o_ref[:, lanes] = jnp.where(_head_lane_mask(outs[0].shape, 0), outs[0],
                                    outs[1]).astype(o_ref.dtype)


def _mem_attn(mq, mk, mv, seq_len, mem_len):
    t = mq.shape[0]
    tm = min(ROW_TILE, seq_len)
    per_seq = seq_len // tm
    kv = pl.BlockSpec((mem_len, MEM_DIM), lambda i: (i // per_seq, 0))
    return pl.pallas_call(
        _mem_attn_kernel,
        grid=(t // tm,),
        in_specs=[pl.BlockSpec((tm, MEM_DIM), lambda i: (i, 0)), kv, kv],
        out_specs=pl.BlockSpec((tm, MEM_DIM), lambda i: (i, 0)),
        out_shape=jax.ShapeDtypeStruct((t, MEM_DIM), BF16),
        compiler_params=_params("parallel"),
        name="mem_attn",
    )(mq, mk, mv)


def _conv_kernel(ts, u_ref, dw_ref, dwb_ref, lng_ref, lnb_ref, o_ref, ext_ref, sh_ref, acc_ref):
    @pl.when(pl.program_id(1) == 0)
    def _():
        ext_ref[0:CONV_HALO, :] = jnp.zeros((CONV_HALO, MIX_DIM), F32)

    @pl.when(pl.program_id(1) > 0)
    def _():
        ext_ref[0:CONV_HALO, :] = ext_ref[ts:ts + CONV_HALO, :]

    ext_ref[CONV_HALO:CONV_HALO + ts, :] = u_ref[0]

    base = CONV_HALO - (CONV_WIDTH - 1)
    n_shift = SUBLANES
    for b in range(n_shift):
        n_rows = ts + ((CONV_WIDTH - 1 - b) // n_shift) * n_shift
        sh_ref[b, 0:n_rows, :] = ext_ref[base + b:base + b + n_rows, :]
    rc = min(CONV_ROW_CHUNK, ts)
    for r0 in range(0, ts, rc):
        for l0 in range(0, MIX_DIM, LANES):
            groups = (rc // SUBLANES, SUBLANES, LANES)
            acc = jnp.zeros(groups, F32) + dwb_ref[:, l0:l0 + LANES]
            for j in range(CONV_WIDTH):
                a, b = divmod(j, n_shift)
                x = sh_ref[b, r0 + a * n_shift:r0 + a * n_shift + rc, l0:l0 + LANES]
                w = jnp.broadcast_to(dw_ref[j, :, l0:l0 + LANES][None], groups)
                acc = acc + w * x.reshape(groups)
            acc_ref[r0:r0 + rc, l0:l0 + LANES] = acc.reshape(rc, LANES)

    y = acc_ref[...]
    mu = jnp.mean(y, axis=-1, keepdims=True)
    yc = y - mu
    var = jnp.mean(yc * yc, axis=-1, keepdims=True)
    z = yc * lax.rsqrt(var + EPS) * lng_ref[...] + lnb_ref[...]
    o_ref[0] = _silu(z).astype(o_ref.dtype)


def _conv_module(u, dw, dwb, lng, lnb):
    b, s, _ = u.shape
    ts = min(CONV_TILE, s)
    assert s % ts == 0 and ts >= CONV_HALO
    blk = pl.BlockSpec((1, ts, MIX_DIM), lambda bi, i: (bi, i, 0))
    return pl.pallas_call(
        functools.partial(_conv_kernel, ts),
        grid=(b, s // ts),
        in_specs=[blk, _resident(dw.shape), _resident(dwb.shape), _resident(lng.shape),
                  _resident(lnb.shape)],
        out_specs=blk,
        out_shape=jax.ShapeDtypeStruct((b, s, MIX_DIM), BF16),
        scratch_shapes=[pltpu.VMEM((ts + CONV_HALO, MIX_DIM), F32),
                        pltpu.VMEM((SUBLANES, ts + CONV_HALO - SUBLANES, MIX_DIM), F32),
                        pltpu.VMEM((ts, MIX_DIM), F32)],
        compiler_params=_params("parallel", "arbitrary"),
        name="conv_module",
    )(u, dw, dwb, lng, lnb)


def _outproj_math(h_ref, mix_ref, mem_ref, w_ref, gn_ref):
    h2 = (h_ref[...]
          + jnp.dot(mix_ref[...], w_ref[0:MIX_DIM, :], preferred_element_type=F32)
          + jnp.dot(mem_ref[...], w_ref[MIX_DIM:, :], preferred_element_type=F32))
    return h2, _rms(h2, gn_ref[...])


def _outproj_dense_kernel(h_ref, mix_ref, mem_ref, w_ref, gn_ref, h2_ref, z_ref):
    h2, z = _outproj_math(h_ref, mix_ref, mem_ref, w_ref, gn_ref)
    h2_ref[...] = h2
    z_ref[...] = z.astype(BF16)


def _outproj_router_kernel(h_ref, mix_ref, mem_ref, w_ref, gn_ref, wr_ref,
                           h2_ref, zp_ref, route_ref):
    h2, z = _outproj_math(h_ref, mix_ref, mem_ref, w_ref, gn_ref)
    h2_ref[...] = h2

    half = D_MODEL // 2
    lo_bits = pltpu.bitcast(z[:, :half].astype(BF16).astype(F32), jnp.uint32) >> 16
    hi_bits = pltpu.bitcast(z[:, half:].astype(BF16).astype(F32), jnp.uint32) & jnp.uint32(0xFFFF0000)
    packed = lo_bits | hi_bits
    for j in range(half // LANES):
        zp_ref[pl.ds(j, packed.shape[0], stride=PACKED_SLABS), :] = packed[:, j * LANES:(j + 1) * LANES]

    z_hi, z_lo = _split_bf16(z, 2)
    wr_hi, wr_lo = _split_bf16(wr_ref[...], 2)
    hi_terms = jnp.dot(z_hi, jnp.concatenate([wr_hi, wr_lo], axis=1),
                       preferred_element_type=F32)
    logits = (hi_terms[:, :LANES] + hi_terms[:, LANES:]
              + jnp.dot(z_lo, wr_hi, preferred_element_type=F32))
    lane = lax.broadcasted_iota(jnp.int32, logits.shape, 1)
    lane_f = lane.astype(F32)
    logits = jnp.where(lane < N_EXPERTS, logits, -jnp.inf)
    l1 = jnp.max(logits, axis=-1, keepdims=True)
    e1 = jnp.min(jnp.where(logits == l1, lane_f, float(LANES)), axis=-1, keepdims=True)
    rest = jnp.where(lane_f == e1, -jnp.inf, logits)
    l2 = jnp.max(rest, axis=-1, keepdims=True)
    e2 = jnp.min(jnp.where(rest == l2, lane_f, float(LANES)), axis=-1, keepdims=True)
    g2 = 1.0 / (1.0 + jnp.exp(l1 - l2))
    g1 = 1.0 - g2
    route = jnp.where(lane == 0, g1, 0.0) + jnp.where(lane == 1, g2, 0.0)
    route = route + jnp.where(lane == 2, e1, 0.0)
    route_ref[...] = route + jnp.where(lane == 3, e2, 0.0)


def _outproj(h, mix, mem, w, gn, router=None):
    t = h.shape[0]
    tm = min(ROW_TILE, t)
    assert t % tm == 0
    row = lambda n: pl.BlockSpec((tm, n), lambda i: (i, 0))
    in_specs = [row(D_MODEL), row(MIX_DIM), row(MEM_DIM), _resident(w.shape), _resident(gn.shape)]
    if router is None:
        return pl.pallas_call(
            _outproj_dense_kernel, grid=(t // tm,), in_specs=in_specs,
            out_specs=[row(D_MODEL), row(D_MODEL)],
            out_shape=[jax.ShapeDtypeStruct((t, D_MODEL), F32),
                       jax.ShapeDtypeStruct((t, D_MODEL), BF16)],
            compiler_params=_params("parallel"), name="outproj_dense",
        )(h, mix, mem, w, gn)
    return pl.pallas_call(
        _outproj_router_kernel, grid=(t // tm,),
        in_specs=in_specs + [_resident(router.shape)],
        out_specs=[row(D_MODEL), pl.BlockSpec((tm * PACKED_SLABS, LANES), lambda i: (i, 0)),
                   row(LANES)],
        out_shape=[jax.ShapeDtypeStruct((t, D_MODEL), F32),
                   jax.ShapeDtypeStruct((t * PACKED_SLABS, LANES), jnp.uint32),
                   jax.ShapeDtypeStruct((t, LANES), F32)],
        compiler_params=_params("parallel"), name="outproj_router",
    )(h, mix, mem, w, gn, router)


def _ffn_kernel(h_ref, z_ref, wg_ref, wu_ref, wd_ref, o_ref, acc_ref):
    z = z_ref[...]
    d_ff = wg_ref.shape[1]
    acc_ref[...] = h_ref[...]
    for f0 in range(0, d_ff, FFN_CHUNK):
        g = jnp.dot(z, wg_ref[:, f0:f0 + FFN_CHUNK], preferred_element_type=F32)
        u = jnp.dot(z, wu_ref[:, f0:f0 + FFN_CHUNK], preferred_element_type=F32)
        a = (_silu(g) * u).astype(BF16)
        acc_ref[...] += jnp.dot(a, wd_ref[f0:f0 + FFN_CHUNK, :], preferred_element_type=F32)
    o_ref[...] = acc_ref[...]


def _ffn(h2, z, wg, wu, wd):
    t = h2.shape[0]
    tm = min(ROW_TILE, t)
    assert t % tm == 0 and wg.shape[1] % FFN_CHUNK == 0
    row = pl.BlockSpec((tm, D_MODEL), lambda i: (i, 0))
    return pl.pallas_call(
        _ffn_kernel, grid=(t // tm,),
        in_specs=[row, row, _resident(wg.shape), _resident(wu.shape), _resident(wd.shape)],
        out_specs=row,
        out_shape=jax.ShapeDtypeStruct((t, D_MODEL), F32),
        scratch_shapes=[pltpu.VMEM((tm, D_MODEL), F32)],
        compiler_params=_params("parallel"), name="ffn_dense",
    )(h2, z, wg, wu, wd)


def _token_copy(src, src_row, dst, dst_row, slabs, sem):
    return pltpu.make_async_copy(src.at[pl.ds(src_row, slabs), :], dst.at[pl.ds(dst_row, slabs), :],
                                 sem)


def _block_copy(src, dst, slabs, sem):
    n = EXPERT_ROWS * slabs
    return pltpu.make_async_copy(src.at[pl.ds(0, n), :], dst.at[pl.ds(0, n), :], sem)


def _experts_kernel(n_f, be_ref, nact_ref, tok_cur_ref, tok_next_ref, dst_prev_ref, dst_cur_ref,
                    zp_hbm, wg_ref, wu_ref, wd_ref, y_hbm, xp_ref, x_ref, acc_ref, stage_ref,
                    sem_in, sem_out):
    del be_ref
    i = pl.program_id(0)
    f = pl.program_id(1)
    n_active = nact_ref[0]
    active = i < n_active
    has_next = i + 1 < n_active
    has_prev = i >= 1
    first_step = f == 0
    last_step = f == n_f - 1
    half = D_MODEL // 2

    def gather(idx_ref, r, priority=1):
        src = pl.multiple_of(idx_ref[0, 0, r], PACKED_SLABS)
        _token_copy(zp_hbm, src, xp_ref, r * PACKED_SLABS, PACKED_SLABS, sem_in).start(
            priority=priority)

    def scatter(idx_ref, r, priority=0):
        dst = pl.multiple_of(idx_ref[0, 0, r], OUT_SLABS)
        _token_copy(stage_ref, r * OUT_SLABS, y_hbm, dst, OUT_SLABS, sem_out).start(
            priority=priority)

    def gather_next(r, parity):
        del parity
        gather(tok_next_ref, r)

    def scatter_prev(r, parity):
        scatter(dst_prev_ref, r, priority=parity)

    def rolled(fn):
        def body(r, _):
            fn(r)
            return 0
        lax.fori_loop(0, EXPERT_ROWS, body, 0)

    def compute(phase, inline_dma=None):
        def sub_block(s, _):
            r0 = pl.multiple_of(s * EXPERT_SUB, EXPERT_SUB)
            rows = pl.ds(r0, EXPERT_SUB)
            x = x_ref[rows, :]
            g = jnp.dot(x, wg_ref[0, 0], preferred_element_type=F32)
            u = jnp.dot(x, wu_ref[0, 0], preferred_element_type=F32)
            a = (_silu(g) * u).astype(BF16)
            c = jnp.dot(a, wd_ref[0, 0], preferred_element_type=F32)
            if phase == "first":
                acc_ref[rows, :] = c
            elif phase == "mid":
                acc_ref[rows, :] += c
            else:
                out = acc_ref[rows, :] + c
                for j in range(OUT_SLABS):
                    stage_ref[pl.ds(r0 * OUT_SLABS + j, EXPERT_SUB, stride=OUT_SLABS), :] = (
                        out[:, j * LANES:(j + 1) * LANES])
            if inline_dma is not None:
                for k in range(EXPERT_SUB):
                    inline_dma(r0 + k, k % 2)
            return 0
        lax.fori_loop(0, EXPERT_ROWS // EXPERT_SUB, sub_block, 0)

    @pl.when((i == 0) & first_step)
    def _():
        stage_ref[...] = jnp.zeros(stage_ref.shape, F32)
        n_sink = EXPERT_ROWS * OUT_SLABS
        sink = pltpu.make_async_copy(
            stage_ref, y_hbm.at[pl.ds(y_hbm.shape[0] - n_sink, n_sink), :], sem_out)
        sink.start()
        sink.wait()
        rolled(functools.partial(gather, tok_cur_ref))

    @pl.when(active & first_step)
    def _():
        _block_copy(zp_hbm, xp_ref, PACKED_SLABS, sem_in).wait()
        for j in range(PACKED_SLABS):
            xp = xp_ref[pl.ds(j, EXPERT_ROWS, stride=PACKED_SLABS), :]
            lanes = slice(j * LANES, (j + 1) * LANES)
            x_ref[:, lanes] = pltpu.bitcast(xp << 16, F32).astype(BF16)
            x_ref[:, half + j * LANES:half + (j + 1) * LANES] = pltpu.bitcast(
                xp & jnp.uint32(0xFFFF0000), F32).astype(BF16)

    @pl.when(active & first_step & has_prev)
    def _():
        compute("first", scatter_prev)

    @pl.when(active & first_step & jnp.logical_not(has_prev))
    def _():
        compute("first")

    if n_f > 2:
        @pl.when(active & jnp.logical_not(first_step) & jnp.logical_not(last_step))
        def _():
            compute("mid")

    @pl.when(active & last_step & has_prev)
    def _():
        _block_copy(stage_ref, y_hbm, OUT_SLABS, sem_out).wait()

    @pl.when(active & last_step & has_next)
    def _():
        compute("last", gather_next)

    @pl.when(active & last_step & jnp.logical_not(has_next))
    def _():
        compute("last")
        rolled(functools.partial(scatter, dst_cur_ref))
        _block_copy(stage_ref, y_hbm, OUT_SLABS, sem_out).wait()


def _experts(zp, plan, layer, wg, wu, wd):
    t = zp.shape[0] // PACKED_SLABS
    blk_expert, n_active, tok_buf, dst_buf = plan
    nb = blk_expert.shape[0]
    d_ff = wg.shape[3]
    tf = EXPERT_FF_TILE
    n_f = d_ff // tf
    assert d_ff % tf == 0 and n_f >= 2
    last = nb - 1

    def idx(shift):
        return pl.BlockSpec((1, 1, EXPERT_ROWS),
                            lambda i, f, *_: (jnp.clip(i + shift, 0, last), 0, 0),
                            memory_space=pltpu.SMEM)

    grid_spec = pltpu.PrefetchScalarGridSpec(
        num_scalar_prefetch=2,
        grid=(nb, n_f),
        in_specs=[idx(0), idx(1), idx(-1), idx(0), pl.BlockSpec(memory_space=pl.ANY),
                  pl.BlockSpec((1, 1, D_MODEL, tf), lambda i, f, be, *_: (layer, be[i], 0, f)),
                  pl.BlockSpec((1, 1, D_MODEL, tf), lambda i, f, be, *_: (layer, be[i], 0, f)),
                  pl.BlockSpec((1, 1, tf, D_MODEL), lambda i, f, be, *_: (layer, be[i], f, 0))],
        out_specs=pl.BlockSpec(memory_space=pl.ANY),
        scratch_shapes=[pltpu.VMEM((EXPERT_ROWS * PACKED_SLABS, LANES), jnp.uint32),
                        pltpu.VMEM((EXPERT_ROWS, D_MODEL), BF16),
                        pltpu.VMEM((EXPERT_ROWS, D_MODEL), F32),
                        pltpu.VMEM((EXPERT_ROWS * OUT_SLABS, LANES), F32),
                        pltpu.SemaphoreType.DMA(()), pltpu.SemaphoreType.DMA(())],
    )
    return pl.pallas_call(
        functools.partial(_experts_kernel, n_f), grid_spec=grid_spec,
        out_shape=jax.ShapeDtypeStruct(((TOP_K * t + EXPERT_ROWS) * OUT_SLABS, LANES), F32),
        compiler_params=_params("arbitrary", "arbitrary"), name="moe_experts",
    )(blk_expert, n_active, tok_buf, tok_buf, dst_buf, dst_buf, zp, wg, wu, wd)


def _route_plan(route, t):
    a = t * TOP_K
    nb = a // EXPERT_ROWS + N_EXPERTS
    rows = nb * EXPERT_ROWS
    flat_e = route[:, 2:4].astype(jnp.int32).reshape(a)
    experts = jnp.arange(N_EXPERTS, dtype=jnp.int32)
    sizes = jnp.sum((flat_e[:, None] == experts[None, :]).astype(jnp.int32), axis=0)
    start = jnp.cumsum(sizes) - sizes
    padded = ((sizes + EXPERT_ROWS - 1) // EXPERT_ROWS) * EXPERT_ROWS
    pad_end = jnp.cumsum(padded)
    pad_start = pad_end - padded
    sorted_assign = jnp.sort(flat_e * a + jnp.arange(a, dtype=jnp.int32)) % a
    blk_start = jnp.arange(nb, dtype=jnp.int32) * EXPERT_ROWS
    blk_expert = jnp.minimum(jnp.searchsorted(pad_end, blk_start, side="right"),
                             N_EXPERTS - 1).astype(jnp.int32)
    row = jnp.arange(rows, dtype=jnp.int32)
    row_e = jnp.repeat(blk_expert, EXPERT_ROWS)
    rank = row - pad_start[row_e]
    real = (rank >= 0) & (rank < sizes[row_e])
    assign = sorted_assign[jnp.clip(start[row_e] + rank, 0, a - 1)]
    tok = assign // TOP_K
    tok_buf = jnp.where(real, tok, 0)
    dst_buf = jnp.where(real, tok + (assign % TOP_K) * t, TOP_K * t + row % EXPERT_ROWS)
    n_active = (pad_end[-1] // EXPERT_ROWS).reshape(1).astype(jnp.int32)
    return (blk_expert, n_active, (tok_buf * PACKED_SLABS).reshape(nb, 1, EXPERT_ROWS),
            (dst_buf * OUT_SLABS).reshape(nb, 1, EXPERT_ROWS))


def _combine_kernel(h_ref, y0_ref, y1_ref, route_ref, o_ref):
    g = route_ref[...]
    for j in range(OUT_SLABS):
        lanes = slice(j * LANES, (j + 1) * LANES)
        rows = pl.ds(j, h_ref.shape[0], stride=OUT_SLABS)
        o_ref[:, lanes] = h_ref[:, lanes] + (g[:, 0:1] * y0_ref[rows, :] + g[:, 1:2] * y1_ref[rows, :])


def _combine(h2, y, route):
    t = h2.shape[0]
    tm = min(ROW_TILE, t)
    nt = t // tm
    row = pl.BlockSpec((tm, D_MODEL), lambda i: (i, 0))
    return pl.pallas_call(
        _combine_kernel, grid=(nt,),
        in_specs=[row, pl.BlockSpec((tm * OUT_SLABS, LANES), lambda i: (i, 0)),
                  pl.BlockSpec((tm * OUT_SLABS, LANES), lambda i: (i + nt, 0)),
                  pl.BlockSpec((tm, LANES), lambda i: (i, 0))],
        out_specs=row,
        out_shape=jax.ShapeDtypeStruct((t, D_MODEL), F32),
        compiler_params=_params("parallel"), name="moe_combine",
    )(h2, y, y, route)


def _row(v):
    return v.reshape(1, -1).astype(F32)


def _tile_heads(g, n, scale=1.0):
    return _row(jnp.tile(g.astype(F32) * scale, n))


def kernel(x, mem, norm_mix, norm_mem, norm_ffn, w_mem_kv, g_mq, g_mk, fox_w_in, fox_b_f, fox_g_q, fox_g_k, fox_w_out, conv_w_in, conv_b_in, conv_dw, conv_dw_b, conv_ln_g, conv_ln_b, conv_w_out, ffn_w_gate, ffn_w_up, ffn_w_down, moe_router, moe_w_gate, moe_w_up, moe_w_down):
    b, s, d = x.shape
    mem_len = mem.shape[1]
    t = b * s
    assert d == D_MODEL
    scale = HEAD_DIM ** -0.5

    group = jnp.arange(MXU_WIDTH, dtype=jnp.int32) // HEAD_DIM
    ones_bd = (group[:, None] == group[None, :]).astype(BF16)

    moe_wg, moe_wu, moe_wd = (w.astype(BF16) for w in (moe_w_gate, moe_w_up, moe_w_down))
    h = x.reshape(t, d)
    mem2d = mem.reshape(b * mem_len, d)
    for i in range(DEPTH):
        j = i // 2
        gmq = _tile_heads(g_mq[i], MEM_DIM // HEAD_DIM, scale)
        mk, mv = _mem_kv(mem2d, mem_len, _row(norm_mem[i]), w_mem_kv[i].astype(BF16),
                         ones_bd, _tile_heads(g_mk[i], MEM_DIM // HEAD_DIM))
        if i % 2 == 0:
            w = fox_w_in[j]
            m3 = 3 * MIX_DIM
            w_cat = jnp.concatenate(
                [w[:, :m3], w[:, m3 + FOX_HEADS:], w[:, m3:m3 + FOX_HEADS],
                 jnp.zeros((d, F_LANES - FOX_HEADS), w.dtype)], axis=1).astype(BF16)
            bf = jnp.pad(fox_b_f[j].astype(F32), (0, F_LANES - FOX_HEADS)).reshape(1, F_LANES)
            q, k, v, mq, kbias = _fox_inproj(
                h, s, _row(norm_mix[i]), w_cat, ones_bd,
                _tile_heads(fox_g_q[j], FOX_HEADS, scale * LOG2E),
                _tile_heads(fox_g_k[j], FOX_HEADS), gmq, bf)
            seq = lambda a: a.reshape(b, s, MIX_DIM)
            mix = _fox_attn(seq(q), seq(k), seq(kbias), seq(v)).reshape(t, MIX_DIM)
            w_out = fox_w_out[j]
        else:
            u, mq = _conv_inproj(h, s, _row(norm_mix[i]), conv_w_in[j].astype(BF16),
                                 _row(conv_b_in[j]), ones_bd, gmq)
            dw8 = jnp.broadcast_to(conv_dw[j].astype(F32)[:, None, :],
                                   (CONV_WIDTH, SUBLANES, MIX_DIM))
            mix = _conv_module(u.reshape(b, s, MIX_DIM), dw8,
                               _row(conv_dw_b[j]), _row(conv_ln_g[j]),
                               _row(conv_ln_b[j])).reshape(t, MIX_DIM)
            w_out = conv_w_out[j]
        mem_out = _mem_attn(mq, mk, mv, s, mem_len)
        if i % 2 == 0:
            h2, z = _outproj(h, mix, mem_out, w_out.astype(BF16), _row(norm_ffn[i]))
            h = _ffn(h2, z, ffn_w_gate[j].astype(BF16), ffn_w_up[j].astype(BF16),
                     ffn_w_down[j].astype(BF16))
        else:
            wr = jnp.pad(moe_router[j].astype(F32), ((0, 0), (0, LANES - N_EXPERTS)))
            h2, zp, route = _outproj(h, mix, mem_out, w_out.astype(BF16), _row(norm_ffn[i]),
                                     router=wr)
            plan = _route_plan(route, t)
            y = _experts(zp, plan, j, moe_wg, moe_wu, moe_wd)
            h = _combine(h2, y, route)
    return h.reshape(b, s, d)
```

```python
import functools

import jax
import jax.numpy as jnp
from jax import lax
from jax.experimental import pallas as pl
from jax.experimental.pallas import tpu as pltpu

F32 = jnp.float32
BF16 = jnp.bfloat16

D_MODEL = 1024
HEAD_DIM = 64
MEM_DIM = 256
MIX_DIM = D_MODEL - MEM_DIM
FOX_HEADS = MIX_DIM // HEAD_DIM
CONV_WIDTH = 31
N_EXPERTS = 8
TOP_K = 2
DEPTH = 4
EPS = 1e-6
NEG_INF = -1e30
LOG2E = 1.4426950408889634
BIAS_PARTS = 3
PART_STRIDE = 16

LANES = 128
SUBLANES = 8
MXU_WIDTH = 256
VMEM_LIMIT_BYTES = 56 * 1024 * 1024

ROW_TILE = 512
ATTN_TILE = 512
CONV_TILE = 256
CONV_HALO = 32
CONV_ROW_CHUNK = 64
FFN_CHUNK = 256
EXPERT_ROWS = 1024
EXPERT_SUB = 256
EXPERT_FF_TILE = 1792
F_LANES = LANES
PACKED_SLABS = D_MODEL // 2 // LANES
OUT_SLABS = D_MODEL // LANES


def _params(*sem):
    return pltpu.CompilerParams(dimension_semantics=sem, vmem_limit_bytes=VMEM_LIMIT_BYTES)


def _resident(shape):
    nd = len(shape)
    return pl.BlockSpec(shape, lambda *_: (0,) * nd, pipeline_mode=pl.Buffered(1))


def _rms(x, g):
    ms = jnp.mean(x * x, axis=-1, keepdims=True)
    return x * lax.rsqrt(ms + EPS) * g


def _head_rms(x, ones_bd, g):
    w = ones_bd.shape[0]
    sq = (x * x).astype(BF16)
    ss = jnp.concatenate([jnp.dot(sq[:, c:c + w], ones_bd, preferred_element_type=F32)
                          for c in range(0, x.shape[1], w)], axis=1)
    return x * lax.rsqrt(ss * (1.0 / HEAD_DIM) + EPS) * g


def _silu(x):
    return x * (1.0 / (1.0 + jnp.exp(-x)))


def _log_sigmoid(x):
    return jnp.minimum(x, 0.0) - jnp.log(1.0 + jnp.exp(-jnp.abs(x)))


def _row_prefix_sum(x):
    n = x.shape[0]
    row = lax.broadcasted_iota(jnp.int32, x.shape, 0)
    s = 1
    while s < n:
        x = x + jnp.where(row >= s, pltpu.roll(x, s, 0), 0.0)
        s *= 2
    return x


def _split_bf16(x, parts):
    out = []
    for _ in range(parts - 1):
        hi = x.astype(BF16)
        out.append(hi)
        x = x - hi.astype(F32)
    out.append(x.astype(BF16))
    return out


def _moe_combined(h2_ref, y0_ref, y1_ref, route_ref):
    g = route_ref[...]
    cols = []
    for j in range(OUT_SLABS):
        rows = pl.ds(j, h2_ref.shape[0], stride=OUT_SLABS)
        cols.append(h2_ref[:, j * LANES:(j + 1) * LANES]
                    + (g[:, 0:1] * y0_ref[rows, :] + g[:, 1:2] * y1_ref[rows, :]))
    return jnp.concatenate(cols, axis=1)


def _fox_inproj_moe_kernel(tiles_per_seq, h2_ref, y0_ref, y1_ref, route_ref, *rest):
    ins, h_ref, tail = rest[:8], rest[8], rest[9:]
    h = _moe_combined(h2_ref, y0_ref, y1_ref, route_ref)
    h_ref[...] = h
    _fox_inproj_body(tiles_per_seq, h, *ins, *tail)


def _fox_inproj_kernel(tiles_per_seq, h_ref, *rest):
    _fox_inproj_body(tiles_per_seq, h_ref[...], *rest)


def _fox_inproj_body(tiles_per_seq, h, gn_ref, w_ref, bd_ref, gq_ref, gk_ref, gmq_ref,
                     bf_ref, place_ref, q_ref, k_ref, v_ref, mq_ref, kb_ref, carry_ref):
    xb = _rms(h, gn_ref[...]).astype(BF16)
    m = MIX_DIM
    q = jnp.dot(xb, w_ref[:, 0:m], preferred_element_type=F32)
    q_ref[...] = _head_rms(q, bd_ref[...], gq_ref[...]).astype(BF16)
    k = jnp.dot(xb, w_ref[:, m:2 * m], preferred_element_type=F32)
    k_ref[...] = _head_rms(k, bd_ref[...], gk_ref[...]).astype(BF16)
    v_ref[...] = jnp.dot(xb, w_ref[:, 2 * m:3 * m], preferred_element_type=F32).astype(BF16)
    mq = jnp.dot(xb, w_ref[:, 3 * m:3 * m + MEM_DIM], preferred_element_type=F32)
    mq_ref[...] = _head_rms(mq, bd_ref[...], gmq_ref[...]).astype(BF16)

    f = jnp.dot(xb, w_ref[:, 3 * m + MEM_DIM:], preferred_element_type=F32) + bf_ref[...]
    lane = lax.broadcasted_iota(jnp.int32, f.shape, 1)
    lf = jnp.where(lane < FOX_HEADS, _log_sigmoid(f), 0.0)

    @pl.when(pl.program_id(0) % tiles_per_seq == 0)
    def _():
        carry_ref[...] = jnp.zeros_like(carry_ref)

    c = _row_prefix_sum(lf) + carry_ref[0:1, :]
    carry_ref[0:1, :] = c[-1:, :]
    parts = _split_bf16(c * (-LOG2E), BIAS_PARTS)
    parked = parts[0].astype(F32)
    for p in range(1, BIAS_PARTS):
        parked = parked + pltpu.roll(parts[p].astype(F32), PART_STRIDE * p, 1)
    kb_ref[...] = jnp.dot(parked.astype(BF16), place_ref[...],
                          preferred_element_type=F32).astype(BF16)


def _bias_placement():
    row = jnp.arange(F_LANES, dtype=jnp.int32)[:, None]
    p, h = row // PART_STRIDE, row % PART_STRIDE
    col = jnp.arange(MIX_DIM, dtype=jnp.int32)[None, :]
    return ((p < BIAS_PARTS) & (h < FOX_HEADS)
            & (col == (h // 2) * LANES + BIAS_PARTS * (h % 2) + p)).astype(BF16)


def _fox_inproj(h, seq_len, gn, w, bd, gq, gk, gmq, bf, moe=None):
    t = h.shape[0]
    tm = min(ROW_TILE, seq_len)
    assert seq_len % tm == 0
    nt = t // tm
    row = lambda n: pl.BlockSpec((tm, n), lambda i: (i, 0))
    place = _bias_placement()
    params = (gn, w, bd, gq, gk, gmq, bf, place)
    in_specs = [row(D_MODEL)] + [_resident(p.shape) for p in params]
    out_specs = [row(MIX_DIM), row(MIX_DIM), row(MIX_DIM), row(MEM_DIM), row(MIX_DIM)]
    out_shape = ([jax.ShapeDtypeStruct((t, MIX_DIM), BF16)] * 3
                 + [jax.ShapeDtypeStruct((t, MEM_DIM), BF16),
                    jax.ShapeDtypeStruct((t, MIX_DIM), BF16)])
    common = dict(grid=(nt,), scratch_shapes=[pltpu.VMEM((SUBLANES, F_LANES), F32)],
                  compiler_params=_params("arbitrary"))
    if moe is None:
        outs = pl.pallas_call(
            functools.partial(_fox_inproj_kernel, seq_len // tm), in_specs=in_specs,
            out_specs=out_specs, out_shape=out_shape, name="fox_inproj", **common,
        )(h, *params)
        return (h, *outs)
    y, route = moe
    y_spec = lambda k: pl.BlockSpec((tm * OUT_SLABS, LANES), lambda i: (i + k * nt, 0))
    return pl.pallas_call(
        functools.partial(_fox_inproj_moe_kernel, seq_len // tm),
        in_specs=[row(D_MODEL), y_spec(0), y_spec(1), row(LANES)] + in_specs[1:],
        out_specs=[row(D_MODEL)] + out_specs,
        out_shape=[jax.ShapeDtypeStruct((t, D_MODEL), F32)] + out_shape,
        name="fox_inproj_moe", **common,
    )(h, y, y, route, *params)


def _conv_inproj_kernel(h_ref, gn_ref, w_ref, b_ref, bd_ref, gmq_ref, u_ref, mq_ref):
    xb = _rms(h_ref[...], gn_ref[...]).astype(BF16)
    m = MIX_DIM
    a = jnp.dot(xb, w_ref[:, 0:m], preferred_element_type=F32) + b_ref[:, 0:m]
    g = jnp.dot(xb, w_ref[:, m:2 * m], preferred_element_type=F32) + b_ref[:, m:2 * m]
    u_ref[...] = a * (1.0 / (1.0 + jnp.exp(-g)))
    mq = jnp.dot(xb, w_ref[:, 2 * m:], preferred_element_type=F32)
    mq_ref[...] = _head_rms(mq, bd_ref[...], gmq_ref[...]).astype(BF16)


def _conv_inproj(h, seq_len, gn, w, b, bd, gmq):
    t = h.shape[0]
    tm = min(ROW_TILE, seq_len)
    assert t % tm == 0
    row = lambda n: pl.BlockSpec((tm, n), lambda i: (i, 0))
    return pl.pallas_call(
        _conv_inproj_kernel,
        grid=(t // tm,),
        in_specs=[row(D_MODEL), _resident(gn.shape), _resident(w.shape), _resident(b.shape),
                  _resident(bd.shape), _resident(gmq.shape)],
        out_specs=[row(MIX_DIM), row(MEM_DIM)],
        out_shape=[jax.ShapeDtypeStruct((t, MIX_DIM), F32),
                   jax.ShapeDtypeStruct((t, MEM_DIM), BF16)],
        compiler_params=_params("parallel"),
        name="conv_inproj",
    )(h, gn, w, b, bd, gmq)


def _mem_kv_kernel(m_ref, gn_ref, w_ref, bd_ref, gk_ref, mk_ref, mv_ref):
    xb = _rms(m_ref[...], gn_ref[...]).astype(BF16)
    mk = jnp.dot(xb, w_ref[:, 0:MEM_DIM], preferred_element_type=F32)
    mk_ref[...] = _head_rms(mk, bd_ref[...], gk_ref[...]).astype(BF16)
    mv_ref[...] = jnp.dot(xb, w_ref[:, MEM_DIM:], preferred_element_type=F32).astype(BF16)


def _mem_kv(mem2d, mem_len, gn, w, bd, gk):
    rows = mem2d.shape[0]
    row = lambda n: pl.BlockSpec((mem_len, n), lambda i: (i, 0))
    return pl.pallas_call(
        _mem_kv_kernel,
        grid=(rows // mem_len,),
        in_specs=[row(D_MODEL), _resident(gn.shape), _resident(w.shape), _resident(bd.shape),
                  _resident(gk.shape)],
        out_specs=[row(MEM_DIM), row(MEM_DIM)],
        out_shape=[jax.ShapeDtypeStruct((rows, MEM_DIM), BF16)] * 2,
        compiler_params=_params("parallel"),
        name="mem_kv",
    )(mem2d, gn, w, bd, gk)


def _head_lane_mask(shape, head_in_pair):
    lane = lax.broadcasted_iota(jnp.int32, shape, len(shape) - 1)
    return (lane // HEAD_DIM) == head_in_pair


def _fox_attn_kernel(tq, q_ref, k_ref, kb_ref, v_ref, o_ref, m_ref, acc_ref):
    i = pl.program_id(2)
    tk = tq
    q2 = q_ref[0]
    zero = jnp.zeros_like(q2)
    lane = lax.broadcasted_iota(jnp.int32, q2.shape, 1)
    q_stack = jnp.concatenate(
        [jnp.concatenate(
            [jnp.where(_head_lane_mask(q2.shape, hh), q2, zero),
             ((lane >= BIAS_PARTS * hh) & (lane < BIAS_PARTS * (hh + 1))).astype(BF16)], axis=1)
         for hh in range(2)], axis=0)
    m_ref[...] = jnp.full(m_ref.shape, NEG_INF, F32)
    acc_ref[...] = jnp.zeros(acc_ref.shape, F32)
    on_or_below_diag = (lax.broadcasted_iota(jnp.int32, (tq, tk), 1)
                        <= lax.broadcasted_iota(jnp.int32, (tq, tk), 0))

    def key_rows(j):
        return pl.ds(pl.multiple_of(j * tk, tk), tk)

    def block(j, masked):
        rows = key_rows(j)
        kb = jnp.concatenate([k_ref[0, rows, :], kb_ref[0, rows, :]], axis=1)
        s2 = lax.dot_general(q_stack, kb, (((1,), (1,)), ((), ())), preferred_element_type=F32)
        vb = v_ref[0, rows, :]
        for hh in range(2):
            s = s2[hh * tq:(hh + 1) * tq, :]
            if masked:
                s = jnp.where(on_or_below_diag, s, NEG_INF)
            m_old = m_ref[hh]
            m_new = jnp.maximum(m_old, jnp.max(s, axis=-1, keepdims=True))
            alpha = jnp.exp2(m_old - m_new)
            p = jnp.concatenate([jnp.exp2(s[:, t0:t0 + LANES] - m_new)
                                 for t0 in range(0, tk, LANES)], axis=1).astype(BF16)
            v_h = jnp.where(_head_lane_mask(vb.shape, hh), vb, jnp.ones_like(vb))
            acc_ref[hh] = alpha * acc_ref[hh] + jnp.dot(p, v_h, preferred_element_type=F32)
            m_ref[hh] = m_new

    def pair(j2, _):
        block(2 * j2, masked=False)
        block(2 * j2 + 1, masked=False)
        return 0

    lax.fori_loop(0, i // 2, pair, 0)

    @pl.when(i % 2 == 1)
    def _():
        block(i - 1, masked=False)

    block(i, masked=True)
    acc0, acc1 = acc_ref[0], acc_ref[1]
    out0 = acc0 * (1.0 / acc0[:, HEAD_DIM:HEAD_DIM + 1])
    out1 = acc1 * (1.0 / acc1[:, 0:1])
    o_ref[0] = jnp.where(_head_lane_mask(out0.shape, 0), out0, out1).astype(o_ref.dtype)


def _fox_attn(q, k, kbias, v):
    b, s, _ = q.shape
    tq = min(ATTN_TILE, s)
    assert s % tq == 0
    seq = pl.BlockSpec((1, s, LANES), lambda bi, hp, i: (bi, 0, hp))
    tile = pl.BlockSpec((1, tq, LANES), lambda bi, hp, i: (bi, i, hp))
    return pl.pallas_call(
        functools.partial(_fox_attn_kernel, tq),
        grid=(b, FOX_HEADS // 2, s // tq),
        in_specs=[tile, seq, seq, seq],
        out_specs=tile,
        out_shape=jax.ShapeDtypeStruct((b, s, MIX_DIM), BF16),
        scratch_shapes=[pltpu.VMEM((2, tq, LANES), F32)] * 2,
        compiler_params=_params("parallel", "parallel", "arbitrary"),
        name="fox_attn",
    )(q, k, kbias, v)


def _mem_attn_kernel(q_ref, k_ref, v_ref, o_ref):
    for pair in range(MEM_DIM // LANES):
        lanes = slice(pair * LANES, (pair + 1) * LANES)
        q2 = q_ref[:, lanes]
        k2 = k_ref[:, lanes]
        v2 = v_ref[:, lanes]
        outs = []
        for hh in range(2):
            qh = jnp.where(_head_lane_mask(q2.shape, hh), q2, jnp.zeros_like(q2))
            s = lax.dot_general(qh, k2, (((1,), (1,)), ((), ())), preferred_element_type=F32)
            p = jnp.exp(s - jnp.max(s, axis=-1, keepdims=True))
            l = jnp.sum(p, axis=-1, keepdims=True)
            outs.append(jnp.dot(p.astype(BF16), v2, preferred_element_type=F32) * (1.0 / l))
        o_ref[:, lanes] = jnp.where(_head_lane_mask(outs[0].shape, 0), outs[0],
                                    outs[1]).astype(o_ref.dtype)


def _mem_attn(mq, mk, mv, seq_len, mem_len):
    t = mq.shape[0]
    tm = min(ROW_TILE, seq_len)
    per_seq = seq_len // tm
    kv = pl.BlockSpec((mem_len, MEM_DIM), lambda i: (i // per_seq, 0))
    return pl.pallas_call(
        _mem_attn_kernel,
        grid=(t // tm,),
        in_specs=[pl.BlockSpec((tm, MEM_DIM), lambda i: (i, 0)), kv, kv],
        out_specs=pl.BlockSpec((tm, MEM_DIM), lambda i: (i, 0)),
        out_shape=jax.ShapeDtypeStruct((t, MEM_DIM), BF16),
        compiler_params=_params("parallel"),
        name="mem_attn",
    )(mq, mk, mv)


def _conv_kernel(ts, u_ref, dw_ref, dwb_ref, lng_ref, lnb_ref, o_ref, ext_ref, sh_ref, acc_ref):
    @pl.when(pl.program_id(1) == 0)
    def _():
        ext_ref[0:CONV_HALO, :] = jnp.zeros((CONV_HALO, MIX_DIM), F32)

    @pl.when(pl.program_id(1) > 0)
    def _():
        ext_ref[0:CONV_HALO, :] = ext_ref[ts:ts + CONV_HALO, :]

    ext_ref[CONV_HALO:CONV_HALO + ts, :] = u_ref[0]

    base = CONV_HALO - (CONV_WIDTH - 1)
    n_shift = SUBLANES
    for b in range(n_shift):
        n_rows = ts + ((CONV_WIDTH - 1 - b) // n_shift) * n_shift
        sh_ref[b, 0:n_rows, :] = ext_ref[base + b:base + b + n_rows, :]
    rc = min(CONV_ROW_CHUNK, ts)
    for r0 in range(0, ts, rc):
        for l0 in range(0, MIX_DIM, LANES):
            groups = (rc // SUBLANES, SUBLANES, LANES)
            acc = jnp.zeros(groups, F32) + dwb_ref[:, l0:l0 + LANES]
            for j in range(CONV_WIDTH):
                a, b = divmod(j, n_shift)
                x = sh_ref[b, r0 + a * n_shift:r0 + a * n_shift + rc, l0:l0 + LANES]
                w = jnp.broadcast_to(dw_ref[j, :, l0:l0 + LANES][None], groups)
                acc = acc + w * x.reshape(groups)
            acc_ref[r0:r0 + rc, l0:l0 + LANES] = acc.reshape(rc, LANES)

    y = acc_ref[...]
    mu = jnp.mean(y, axis=-1, keepdims=True)
    yc = y - mu
    var = jnp.mean(yc * yc, axis=-1, keepdims=True)
    z = yc * lax.rsqrt(var + EPS) * lng_ref[...] + lnb_ref[...]
    o_ref[0] = _silu(z).astype(o_ref.dtype)


def _conv_module(u, dw, dwb, lng, lnb):
    b, s, _ = u.shape
    ts = min(CONV_TILE, s)
    assert s % ts == 0 and ts >= CONV_HALO
    blk = pl.BlockSpec((1, ts, MIX_DIM), lambda bi, i: (bi, i, 0))
    return pl.pallas_call(
        functools.partial(_conv_kernel, ts),
        grid=(b, s // ts),
        in_specs=[blk, _resident(dw.shape), _resident(dwb.shape), _resident(lng.shape),
                  _resident(lnb.shape)],
        out_specs=blk,
        out_shape=jax.ShapeDtypeStruct((b, s, MIX_DIM), BF16),
        scratch_shapes=[pltpu.VMEM((ts + CONV_HALO, MIX_DIM), F32),
                        pltpu.VMEM((SUBLANES, ts + CONV_HALO - SUBLANES, MIX_DIM), F32),
                        pltpu.VMEM((ts, MIX_DIM), F32)],
        compiler_params=_params("parallel", "arbitrary"),
        name="conv_module",
    )(u, dw, dwb, lng, lnb)


def _outproj_math(h_ref, mix_ref, mem_ref, w_ref, gn_ref):
    h2 = (h_ref[...]
          + jnp.dot(mix_ref[...], w_ref[0:MIX_DIM, :], preferred_element_type=F32)
          + jnp.dot(mem_ref[...], w_ref[MIX_DIM:, :], preferred_element_type=F32))
    return h2, _rms(h2, gn_ref[...])


def _outproj_dense_kernel(h_ref, mix_ref, mem_ref, w_ref, gn_ref, h2_ref, z_ref):
    h2, z = _outproj_math(h_ref, mix_ref, mem_ref, w_ref, gn_ref)
    h2_ref[...] = h2
    z_ref[...] = z.astype(BF16)


def _outproj_router_kernel(h_ref, mix_ref, mem_ref, w_ref, gn_ref, wr_ref,
                           h2_ref, zp_ref, route_ref):
    h2, z = _outproj_math(h_ref, mix_ref, mem_ref, w_ref, gn_ref)
    h2_ref[...] = h2

    half = D_MODEL // 2
    lo_bits = pltpu.bitcast(z[:, :half].astype(BF16).astype(F32), jnp.uint32) >> 16
    hi_bits = pltpu.bitcast(z[:, half:].astype(BF16).astype(F32), jnp.uint32) & jnp.uint32(0xFFFF0000)
    packed = lo_bits | hi_bits
    for j in range(half // LANES):
        zp_ref[pl.ds(j, packed.shape[0], stride=PACKED_SLABS), :] = packed[:, j * LANES:(j + 1) * LANES]

    z_hi, z_lo = _split_bf16(z, 2)
    wr_hi, wr_lo = _split_bf16(wr_ref[...], 2)
    hi_terms = jnp.dot(z_hi, jnp.concatenate([wr_hi, wr_lo], axis=1),
                       preferred_element_type=F32)
    logits = (hi_terms[:, :LANES] + hi_terms[:, LANES:]
              + jnp.dot(z_lo, wr_hi, preferred_element_type=F32))
    lane = lax.broadcasted_iota(jnp.int32, logits.shape, 1)
    lane_f = lane.astype(F32)
    logits = jnp.where(lane < N_EXPERTS, logits, -jnp.inf)
    l1 = jnp.max(logits, axis=-1, keepdims=True)
    e1 = jnp.min(jnp.where(logits == l1, lane_f, float(LANES)), axis=-1, keepdims=True)
    rest = jnp.where(lane_f == e1, -jnp.inf, logits)
    l2 = jnp.max(rest, axis=-1, keepdims=True)
    e2 = jnp.min(jnp.where(rest == l2, lane_f, float(LANES)), axis=-1, keepdims=True)
    g2 = 1.0 / (1.0 + jnp.exp(l1 - l2))
    g1 = 1.0 - g2
    route = jnp.where(lane == 0, g1, 0.0) + jnp.where(lane == 1, g2, 0.0)
    route = route + jnp.where(lane == 2, e1, 0.0)
    route_ref[...] = route + jnp.where(lane == 3, e2, 0.0)


def _outproj(h, mix, mem, w, gn, router=None):
    t = h.shape[0]
    tm = min(ROW_TILE, t)
    assert t % tm == 0
    row = lambda n: pl.BlockSpec((tm, n), lambda i: (i, 0))
    in_specs = [row(D_MODEL), row(MIX_DIM), row(MEM_DIM), _resident(w.shape), _resident(gn.shape)]
    if router is None:
        return pl.pallas_call(
            _outproj_dense_kernel, grid=(t // tm,), in_specs=in_specs,
            out_specs=[row(D_MODEL), row(D_MODEL)],
            out_shape=[jax.ShapeDtypeStruct((t, D_MODEL), F32),
                       jax.ShapeDtypeStruct((t, D_MODEL), BF16)],
            compiler_params=_params("parallel"), name="outproj_dense",
        )(h, mix, mem, w, gn)
    return pl.pallas_call(
        _outproj_router_kernel, grid=(t // tm,),
        in_specs=in_specs + [_resident(router.shape)],
        out_specs=[row(D_MODEL), pl.BlockSpec((tm * PACKED_SLABS, LANES), lambda i: (i, 0)),
                   row(LANES)],
        out_shape=[jax.ShapeDtypeStruct((t, D_MODEL), F32),
                   jax.ShapeDtypeStruct((t * PACKED_SLABS, LANES), jnp.uint32),
                   jax.ShapeDtypeStruct((t, LANES), F32)],
        compiler_params=_params("parallel"), name="outproj_router",
    )(h, mix, mem, w, gn, router)


def _ffn_kernel(h_ref, z_ref, wg_ref, wu_ref, wd_ref, o_ref, acc_ref):
    z = z_ref[...]
    d_ff = wg_ref.shape[1]
    acc_ref[...] = h_ref[...]
    for f0 in range(0, d_ff, FFN_CHUNK):
        g = jnp.dot(z, wg_ref[:, f0:f0 + FFN_CHUNK], preferred_element_type=F32)
        u = jnp.dot(z, wu_ref[:, f0:f0 + FFN_CHUNK], preferred_element_type=F32)
        a = (_silu(g) * u).astype(BF16)
        acc_ref[...] += jnp.dot(a, wd_ref[f0:f0 + FFN_CHUNK, :], preferred_element_type=F32)
    o_ref[...] = acc_ref[...]


def _ffn(h2, z, wg, wu, wd):
    t = h2.shape[0]
    tm = min(ROW_TILE, t)
    assert t % tm == 0 and wg.shape[1] % FFN_CHUNK == 0
    row = pl.BlockSpec((tm, D_MODEL), lambda i: (i, 0))
    return pl.pallas_call(
        _ffn_kernel, grid=(t // tm,),
        in_specs=[row, row, _resident(wg.shape), _resident(wu.shape), _resident(wd.shape)],
        out_specs=row,
        out_shape=jax.ShapeDtypeStruct((t, D_MODEL), F32),
        scratch_shapes=[pltpu.VMEM((tm, D_MODEL), F32)],
        compiler_params=_params("parallel"), name="ffn_dense",
    )(h2, z, wg, wu, wd)


def _token_copy(src, src_row, dst, dst_row, slabs, sem):
    return pltpu.make_async_copy(src.at[pl.ds(src_row, slabs), :], dst.at[pl.ds(dst_row, slabs), :],
                                 sem)


def _block_copy(src, dst, slabs, sem):
    n = EXPERT_ROWS * slabs
    return pltpu.make_async_copy(src.at[pl.ds(0, n), :], dst.at[pl.ds(0, n), :], sem)


def _experts_kernel(n_f, be_ref, nact_ref, tok_cur_ref, tok_next_ref, dst_prev_ref, dst_cur_ref,
                    zp_hbm, wg_ref, wu_ref, wd_ref, y_hbm, xp_ref, x_ref, acc_ref, stage_ref,
                    sem_in, sem_out):
    del be_ref
    i = pl.program_id(0)
    f = pl.program_id(1)
    n_active = nact_ref[0]
    active = i < n_active
    has_next = i + 1 < n_active
    has_prev = i >= 1
    first_step = f == 0
    last_step = f == n_f - 1
    half = D_MODEL // 2

    def gather(idx_ref, r, priority=0):
        src = pl.multiple_of(idx_ref[0, 0, r], PACKED_SLABS)
        _token_copy(zp_hbm, src, xp_ref, r * PACKED_SLABS, PACKED_SLABS, sem_in).start(
            priority=priority)

    def scatter(idx_ref, r, priority=0):
        dst = pl.multiple_of(idx_ref[0, 0, r], OUT_SLABS)
        _token_copy(stage_ref, r * OUT_SLABS, y_hbm, dst, OUT_SLABS, sem_out).start(
            priority=priority)

    def gather_next(r, parity):
        del parity
        gather(tok_next_ref, r)

    def scatter_prev(r, parity):
        scatter(dst_prev_ref, r, priority=parity)

    def rolled(fn):
        def body(r, _):
            fn(r)
            return 0
        lax.fori_loop(0, EXPERT_ROWS, body, 0)

    def compute(phase, inline_dma=None):
        def sub_block(s, _):
            r0 = pl.multiple_of(s * EXPERT_SUB, EXPERT_SUB)
            rows = pl.ds(r0, EXPERT_SUB)
            x = x_ref[rows, :]
            g = jnp.dot(x, wg_ref[0, 0], preferred_element_type=F32)
            u = jnp.dot(x, wu_ref[0, 0], preferred_element_type=F32)
            a = (_silu(g) * u).astype(BF16)
            c = jnp.dot(a, wd_ref[0, 0], preferred_element_type=F32)
            if phase == "first":
                acc_ref[rows, :] = c
            elif phase == "mid":
                acc_ref[rows, :] += c
            else:
                out = acc_ref[rows, :] + c
                for j in range(OUT_SLABS):
                    stage_ref[pl.ds(r0 * OUT_SLABS + j, EXPERT_SUB, stride=OUT_SLABS), :] = (
                        out[:, j * LANES:(j + 1) * LANES])
            if inline_dma is not None:
                for k in range(EXPERT_SUB):
                    inline_dma(r0 + k, k % 2)
            return 0
        lax.fori_loop(0, EXPERT_ROWS // EXPERT_SUB, sub_block, 0)

    @pl.when((i == 0) & first_step)
    def _():
        stage_ref[...] = jnp.zeros(stage_ref.shape, F32)
        n_sink = EXPERT_ROWS * OUT_SLABS
        sink = pltpu.make_async_copy(
            stage_ref, y_hbm.at[pl.ds(y_hbm.shape[0] - n_sink, n_sink), :], sem_out)
        sink.start()
        sink.wait()
        rolled(functools.partial(gather, tok_cur_ref))

    @pl.when(active & first_step)
    def _():
        _block_copy(zp_hbm, xp_ref, PACKED_SLABS, sem_in).wait()
        for j in range(PACKED_SLABS):
            xp = xp_ref[pl.ds(j, EXPERT_ROWS, stride=PACKED_SLABS), :]
            lanes = slice(j * LANES, (j + 1) * LANES)
            x_ref[:, lanes] = pltpu.bitcast(xp << 16, F32).astype(BF16)
            x_ref[:, half + j * LANES:half + (j + 1) * LANES] = pltpu.bitcast(
                xp & jnp.uint32(0xFFFF0000), F32).astype(BF16)

    @pl.when(active & first_step & has_prev)
    def _():
        compute("first", scatter_prev)

    @pl.when(active & first_step & jnp.logical_not(has_prev))
    def _():
        compute("first")

    if n_f > 2:
        @pl.when(active & jnp.logical_not(first_step) & jnp.logical_not(last_step))
        def _():
            compute("mid")

    @pl.when(active & last_step & has_prev)
    def _():
        _block_copy(stage_ref, y_hbm, OUT_SLABS, sem_out).wait()

    @pl.when(active & last_step & has_next)
    def _():
        compute("last", gather_next)

    @pl.when(active & last_step & jnp.logical_not(has_next))
    def _():
        compute("last")
        rolled(functools.partial(scatter, dst_cur_ref))
        _block_copy(stage_ref, y_hbm, OUT_SLABS, sem_out).wait()


def _experts(zp, plan, layer, wg, wu, wd):
    t = zp.shape[0] // PACKED_SLABS
    blk_expert, n_active, tok_buf, dst_buf = plan
    nb = blk_expert.shape[0]
    d_ff = wg.shape[3]
    tf = EXPERT_FF_TILE
    n_f = d_ff // tf
    assert d_ff % tf == 0 and n_f >= 2
    last = nb - 1

    def idx(shift):
        return pl.BlockSpec((1, 1, EXPERT_ROWS),
                            lambda i, f, *_: (jnp.clip(i + shift, 0, last), 0, 0),
                            memory_space=pltpu.SMEM)

    grid_spec = pltpu.PrefetchScalarGridSpec(
        num_scalar_prefetch=2,
        grid=(nb, n_f),
        in_specs=[idx(0), idx(1), idx(-1), idx(0), pl.BlockSpec(memory_space=pl.ANY),
                  pl.BlockSpec((1, 1, D_MODEL, tf), lambda i, f, be, *_: (layer, be[i], 0, f)),
                  pl.BlockSpec((1, 1, D_MODEL, tf), lambda i, f, be, *_: (layer, be[i], 0, f)),
                  pl.BlockSpec((1, 1, tf, D_MODEL), lambda i, f, be, *_: (layer, be[i], f, 0))],
        out_specs=pl.BlockSpec(memory_space=pl.ANY),
        scratch_shapes=[pltpu.VMEM((EXPERT_ROWS * PACKED_SLABS, LANES), jnp.uint32),
                        pltpu.VMEM((EXPERT_ROWS, D_MODEL), BF16),
                        pltpu.VMEM((EXPERT_ROWS, D_MODEL), F32),
                        pltpu.VMEM((EXPERT_ROWS * OUT_SLABS, LANES), F32),
                        pltpu.SemaphoreType.DMA(()), pltpu.SemaphoreType.DMA(())],
    )
    return pl.pallas_call(
        functools.partial(_experts_kernel, n_f), grid_spec=grid_spec,
        out_shape=jax.ShapeDtypeStruct(((TOP_K * t + EXPERT_ROWS) * OUT_SLABS, LANES), F32),
        compiler_params=_params("arbitrary", "arbitrary"), name="moe_experts",
    )(blk_expert, n_active, tok_buf, tok_buf, dst_buf, dst_buf, zp, wg, wu, wd)


def _route_plan(route, t):
    a = t * TOP_K
    nb = a // EXPERT_ROWS + N_EXPERTS
    rows = nb * EXPERT_ROWS
    flat_e = route[:, 2:4].astype(jnp.int32).reshape(a)
    experts = jnp.arange(N_EXPERTS, dtype=jnp.int32)
    sizes = jnp.sum((flat_e[:, None] == experts[None, :]).astype(jnp.int32), axis=0)
    start = jnp.cumsum(sizes) - sizes
    padded = ((sizes + EXPERT_ROWS - 1) // EXPERT_ROWS) * EXPERT_ROWS
    pad_end = jnp.cumsum(padded)
    pad_start = pad_end - padded
    sorted_assign = jnp.sort(flat_e * a + jnp.arange(a, dtype=jnp.int32)) % a
    blk_start = jnp.arange(nb, dtype=jnp.int32) * EXPERT_ROWS
    blk_expert = jnp.minimum(jnp.searchsorted(pad_end, blk_start, side="right"),
                             N_EXPERTS - 1).astype(jnp.int32)
    row = jnp.arange(rows, dtype=jnp.int32)
    row_e = jnp.repeat(blk_expert, EXPERT_ROWS)
    rank = row - pad_start[row_e]
    real = (rank >= 0) & (rank < sizes[row_e])
    assign = sorted_assign[jnp.clip(start[row_e] + rank, 0, a - 1)]
    tok = assign // TOP_K
    tok_buf = jnp.where(real, tok, 0)
    dst_buf = jnp.where(real, tok + (assign % TOP_K) * t, TOP_K * t + row % EXPERT_ROWS)
    n_active = (pad_end[-1] // EXPERT_ROWS).reshape(1).astype(jnp.int32)
    return (blk_expert, n_active, (tok_buf * PACKED_SLABS).reshape(nb, 1, EXPERT_ROWS),
            (dst_buf * OUT_SLABS).reshape(nb, 1, EXPERT_ROWS))


def _combine_kernel(h_ref, y0_ref, y1_ref, route_ref, o_ref):
    o_ref[...] = _moe_combined(h_ref, y0_ref, y1_ref, route_ref)


def _combine(h2, y, route):
    t = h2.shape[0]
    tm = min(ROW_TILE, t)
    nt = t // tm
    row = pl.BlockSpec((tm, D_MODEL), lambda i: (i, 0))
    return pl.pallas_call(
        _combine_kernel, grid=(nt,),
        in_specs=[row, pl.BlockSpec((tm * OUT_SLABS, LANES), lambda i: (i, 0)),
                  pl.BlockSpec((tm * OUT_SLABS, LANES), lambda i: (i + nt, 0)),
                  pl.BlockSpec((tm, LANES), lambda i: (i, 0))],
        out_specs=row,
        out_shape=jax.ShapeDtypeStruct((t, D_MODEL), F32),
        compiler_params=_params("parallel"), name="moe_combine",
    )(h2, y, y, route)


def _row(v):
    return v.reshape(1, -1).astype(F32)


def _tile_heads(g, n, scale=1.0):
    return _row(jnp.tile(g.astype(F32) * scale, n))


def kernel(x, mem, norm_mix, norm_mem, norm_ffn, w_mem_kv, g_mq, g_mk, fox_w_in, fox_b_f, fox_g_q, fox_g_k, fox_w_out, conv_w_in, conv_b_in, conv_dw, conv_dw_b, conv_ln_g, conv_ln_b, conv_w_out, ffn_w_gate, ffn_w_up, ffn_w_down, moe_router, moe_w_gate, moe_w_up, moe_w_down):
    b, s, d = x.shape
    mem_len = mem.shape[1]
    t = b * s
    assert d == D_MODEL
    scale = HEAD_DIM ** -0.5

    group = jnp.arange(MXU_WIDTH, dtype=jnp.int32) // HEAD_DIM
    ones_bd = (group[:, None] == group[None, :]).astype(BF16)

    moe_wg, moe_wu, moe_wd = (w.astype(BF16) for w in (moe_w_gate, moe_w_up, moe_w_down))
    h = x.reshape(t, d)
    pending = None
    mem2d = mem.reshape(b * mem_len, d)
    for i in range(DEPTH):
        j = i // 2
        gmq = _tile_heads(g_mq[i], MEM_DIM // HEAD_DIM, scale)
        mk, mv = _mem_kv(mem2d, mem_len, _row(norm_mem[i]), w_mem_kv[i].astype(BF16),
                         ones_bd, _tile_heads(g_mk[i], MEM_DIM // HEAD_DIM))
        if i % 2 == 0:
            w = fox_w_in[j]
            m3 = 3 * MIX_DIM
            w_cat = jnp.concatenate(
                [w[:, :m3], w[:, m3 + FOX_HEADS:], w[:, m3:m3 + FOX_HEADS],
                 jnp.zeros((d, F_LANES - FOX_HEADS), w.dtype)], axis=1).astype(BF16)
            bf = jnp.pad(fox_b_f[j].astype(F32), (0, F_LANES - FOX_HEADS)).reshape(1, F_LANES)
            h, q, k, v, mq, kbias = _fox_inproj(
                h, s, _row(norm_mix[i]), w_cat, ones_bd,
                _tile_heads(fox_g_q[j], FOX_HEADS, scale * LOG2E),
                _tile_heads(fox_g_k[j], FOX_HEADS), gmq, bf, moe=pending)
            pending = None
            seq = lambda a: a.reshape(b, s, MIX_DIM)
            mix = _fox_attn(seq(q), seq(k), seq(kbias), seq(v)).reshape(t, MIX_DIM)
            w_out = fox_w_out[j]
        else:
            u, mq = _conv_inproj(h, s, _row(norm_mix[i]), conv_w_in[j].astype(BF16),
                                 _row(conv_b_in[j]), ones_bd, gmq)
            dw8 = jnp.broadcast_to(conv_dw[j].astype(F32)[:, None, :],
                                   (CONV_WIDTH, SUBLANES, MIX_DIM))
            mix = _conv_module(u.reshape(b, s, MIX_DIM), dw8,
                               _row(conv_dw_b[j]), _row(conv_ln_g[j]),
                               _row(conv_ln_b[j])).reshape(t, MIX_DIM)
            w_out = conv_w_out[j]
        mem_out = _mem_attn(mq, mk, mv, s, mem_len)
        if i % 2 == 0:
            h2, z = _outproj(h, mix, mem_out, w_out.astype(BF16), _row(norm_ffn[i]))
            h = _ffn(h2, z, ffn_w_gate[j].astype(BF16), ffn_w_up[j].astype(BF16),
                     ffn_w_down[j].astype(BF16))
        else:
            wr = jnp.pad(moe_router[j].astype(F32), ((0, 0), (0, LANES - N_EXPERTS)))
            h2, zp, route = _outproj(h, mix, mem_out, w_out.astype(BF16), _row(norm_ffn[i]),
                                     router=wr)
            plan = _route_plan(route, t)
            y = _experts(zp, plan, j, moe_wg, moe_wu, moe_wd)
            if i + 1 < DEPTH and (i + 1) % 2 == 0:
                h, pending = h2, (y, route)
            else:
                h = _combine(h2, y, route)
    return h.reshape(b, s, d)
```

```python
import functools

import jax
import jax.numpy as jnp
from jax import lax
from jax.experimental import pallas as pl
from jax.experimental.pallas import tpu as pltpu

F32 = jnp.float32
BF16 = jnp.bfloat16

D_MODEL = 1024
HEAD_DIM = 64
MEM_DIM = 256
MIX_DIM = D_MODEL - MEM_DIM
FOX_HEADS = MIX_DIM // HEAD_DIM
CONV_WIDTH = 31
N_EXPERTS = 8
TOP_K = 2
DEPTH = 4
EPS = 1e-6
NEG_INF = -1e30
LOG2E = 1.4426950408889634
BIAS_PARTS = 3
PART_STRIDE = 16

LANES = 128
SUBLANES = 8
MXU_WIDTH = 256
VMEM_LIMIT_BYTES = 56 * 1024 * 1024

ROW_TILE = 512
ATTN_TILE = 512
CONV_TILE = 256
CONV_HALO = 32
CONV_ROW_CHUNK = 64
FFN_CHUNK = 256
EXPERT_ROWS = 1024
EXPERT_SUB = 128
EXPERT_FF_TILE = 1792
F_LANES = LANES
PACKED_SLABS = D_MODEL // 2 // LANES
OUT_SLABS = D_MODEL // LANES


def _params(*sem):
    return pltpu.CompilerParams(dimension_semantics=sem, vmem_limit_bytes=VMEM_LIMIT_BYTES)


def _resident(shape):
    nd = len(shape)
    return pl.BlockSpec(shape, lambda *_: (0,) * nd, pipeline_mode=pl.Buffered(1))


def _rms(x, g):
    ms = jnp.mean(x * x, axis=-1, keepdims=True)
    return x * lax.rsqrt(ms + EPS) * g


def _head_rms(x, ones_bd, g):
    w = ones_bd.shape[0]
    sq = (x * x).astype(BF16)
    ss = jnp.concatenate([jnp.dot(sq[:, c:c + w], ones_bd, preferred_element_type=F32)
                          for c in range(0, x.shape[1], w)], axis=1)
    return x * lax.rsqrt(ss * (1.0 / HEAD_DIM) + EPS) * g


def _silu(x):
    return x * (1.0 / (1.0 + jnp.exp(-x)))


def _log_sigmoid(x):
    return jnp.minimum(x, 0.0) - jnp.log(1.0 + jnp.exp(-jnp.abs(x)))


def _row_prefix_sum(x):
    n = x.shape[0]
    row = lax.broadcasted_iota(jnp.int32, x.shape, 0)
    s = 1
    while s < n:
        x = x + jnp.where(row >= s, pltpu.roll(x, s, 0), 0.0)
        s *= 2
    return x


def _split_bf16(x, parts):
    out = []
    for _ in range(parts - 1):
        hi = x.astype(BF16)
        out.append(hi)
        x = x - hi.astype(F32)
    out.append(x.astype(BF16))
    return out


def _moe_combined(h2_ref, y0_ref, y1_ref, route_ref):
    g = route_ref[...]
    cols = []
    for j in range(OUT_SLABS):
        rows = pl.ds(j, h2_ref.shape[0], stride=OUT_SLABS)
        cols.append(h2_ref[:, j * LANES:(j + 1) * LANES]
                    + (g[:, 0:1] * y0_ref[rows, :] + g[:, 1:2] * y1_ref[rows, :]))
    return jnp.concatenate(cols, axis=1)


def _fox_inproj_moe_kernel(tiles_per_seq, h2_ref, y0_ref, y1_ref, route_ref, *rest):
    ins, h_ref, tail = rest[:8], rest[8], rest[9:]
    h = _moe_combined(h2_ref, y0_ref, y1_ref, route_ref)
    h_ref[...] = h
    _fox_inproj_body(tiles_per_seq, h, *ins, *tail)


def _fox_inproj_kernel(tiles_per_seq, h_ref, *rest):
    _fox_inproj_body(tiles_per_seq, h_ref[...], *rest)


def _fox_inproj_body(tiles_per_seq, h, gn_ref, w_ref, bd_ref, gq_ref, gk_ref, gmq_ref,
                     bf_ref, place_ref, q_ref, k_ref, v_ref, mq_ref, kb_ref, carry_ref):
    xb = _rms(h, gn_ref[...]).astype(BF16)
    m = MIX_DIM
    q = jnp.dot(xb, w_ref[:, 0:m], preferred_element_type=F32)
    q_ref[...] = _head_rms(q, bd_ref[...], gq_ref[...]).astype(BF16)
    k = jnp.dot(xb, w_ref[:, m:2 * m], preferred_element_type=F32)
    k_ref[...] = _head_rms(k, bd_ref[...], gk_ref[...]).astype(BF16)
    v_ref[...] = jnp.dot(xb, w_ref[:, 2 * m:3 * m], preferred_element_type=F32).astype(BF16)
    mq = jnp.dot(xb, w_ref[:, 3 * m:3 * m + MEM_DIM], preferred_element_type=F32)
    mq_ref[...] = _head_rms(mq, bd_ref[...], gmq_ref[...]).astype(BF16)

    f = jnp.dot(xb, w_ref[:, 3 * m + MEM_DIM:], preferred_element_type=F32) + bf_ref[...]
    lane = lax.broadcasted_iota(jnp.int32, f.shape, 1)
    lf = jnp.where(lane < FOX_HEADS, _log_sigmoid(f), 0.0)

    @pl.when(pl.program_id(0) % tiles_per_seq == 0)
    def _():
        carry_ref[...] = jnp.zeros_like(carry_ref)

    c = _row_prefix_sum(lf) + carry_ref[0:1, :]
    carry_ref[0:1, :] = c[-1:, :]
    parts = _split_bf16(c * (-LOG2E), BIAS_PARTS)
    parked = parts[0].astype(F32)
    for p in range(1, BIAS_PARTS):
        parked = parked + pltpu.roll(parts[p].astype(F32), PART_STRIDE * p, 1)
    kb_ref[...] = jnp.dot(parked.astype(BF16), place_ref[...],
                          preferred_element_type=F32).astype(BF16)


def _bias_placement():
    row = jnp.arange(F_LANES, dtype=jnp.int32)[:, None]
    p, h = row // PART_STRIDE, row % PART_STRIDE
    col = jnp.arange(MIX_DIM, dtype=jnp.int32)[None, :]
    return ((p < BIAS_PARTS) & (h < FOX_HEADS)
            & (col == (h // 2) * LANES + BIAS_PARTS * (h % 2) + p)).astype(BF16)


def _fox_inproj(h, seq_len, gn, w, bd, gq, gk, gmq, bf, moe=None):
    t = h.shape[0]
    tm = min(ROW_TILE, seq_len)
    assert seq_len % tm == 0
    nt = t // tm
    row = lambda n: pl.BlockSpec((tm, n), lambda i: (i, 0))
    place = _bias_placement()
    params = (gn, w, bd, gq, gk, gmq, bf, place)
    in_specs = [row(D_MODEL)] + [_resident(p.shape) for p in params]
    out_specs = [row(MIX_DIM), row(MIX_DIM), row(MIX_DIM), row(MEM_DIM), row(MIX_DIM)]
    out_shape = ([jax.ShapeDtypeStruct((t, MIX_DIM), BF16)] * 3
                 + [jax.ShapeDtypeStruct((t, MEM_DIM), BF16),
                    jax.ShapeDtypeStruct((t, MIX_DIM), BF16)])
    common = dict(grid=(nt,), scratch_shapes=[pltpu.VMEM((SUBLANES, F_LANES), F32)],
                  compiler_params=_params("arbitrary"))
    if moe is None:
        outs = pl.pallas_call(
            functools.partial(_fox_inproj_kernel, seq_len // tm), in_specs=in_specs,
            out_specs=out_specs, out_shape=out_shape, name="fox_inproj", **common,
        )(h, *params)
        return (h, *outs)
    y, route = moe
    y_spec = lambda k: pl.BlockSpec((tm * OUT_SLABS, LANES), lambda i: (i + k * nt, 0))
    return pl.pallas_call(
        functools.partial(_fox_inproj_moe_kernel, seq_len // tm),
        in_specs=[row(D_MODEL), y_spec(0), y_spec(1), row(LANES)] + in_specs[1:],
        out_specs=[row(D_MODEL)] + out_specs,
        out_shape=[jax.ShapeDtypeStruct((t, D_MODEL), F32)] + out_shape,
        name="fox_inproj_moe", **common,
    )(h, y, y, route, *params)


def _conv_inproj_kernel(h_ref, gn_ref, w_ref, b_ref, bd_ref, gmq_ref, u_ref, mq_ref):
    xb = _rms(h_ref[...], gn_ref[...]).astype(BF16)
    m = MIX_DIM
    a = jnp.dot(xb, w_ref[:, 0:m], preferred_element_type=F32) + b_ref[:, 0:m]
    g = jnp.dot(xb, w_ref[:, m:2 * m], preferred_element_type=F32) + b_ref[:, m:2 * m]
    u_ref[...] = a * (1.0 / (1.0 + jnp.exp(-g)))
    mq = jnp.dot(xb, w_ref[:, 2 * m:], preferred_element_type=F32)
    mq_ref[...] = _head_rms(mq, bd_ref[...], gmq_ref[...]).astype(BF16)


def _conv_inproj(h, seq_len, gn, w, b, bd, gmq):
    t = h.shape[0]
    tm = min(ROW_TILE, seq_len)
    assert t % tm == 0
    row = lambda n: pl.BlockSpec((tm, n), lambda i: (i, 0))
    return pl.pallas_call(
        _conv_inproj_kernel,
        grid=(t // tm,),
        in_specs=[row(D_MODEL), _resident(gn.shape), _resident(w.shape), _resident(b.shape),
                  _resident(bd.shape), _resident(gmq.shape)],
        out_specs=[row(MIX_DIM), row(MEM_DIM)],
        out_shape=[jax.ShapeDtypeStruct((t, MIX_DIM), F32),
                   jax.ShapeDtypeStruct((t, MEM_DIM), BF16)],
        compiler_params=_params("parallel"),
        name="conv_inproj",
    )(h, gn, w, b, bd, gmq)


def _mem_kv_kernel(m_ref, gn_ref, w_ref, bd_ref, gk_ref, mk_ref, mv_ref):
    xb = _rms(m_ref[...], gn_ref[...]).astype(BF16)
    mk = jnp.dot(xb, w_ref[:, 0:MEM_DIM], preferred_element_type=F32)
    mk_ref[...] = _head_rms(mk, bd_ref[...], gk_ref[...]).astype(BF16)
    mv_ref[...] = jnp.dot(xb, w_ref[:, MEM_DIM:], preferred_element_type=F32).astype(BF16)


def _mem_kv(mem2d, mem_len, gn, w, bd, gk):
    rows = mem2d.shape[0]
    row = lambda n: pl.BlockSpec((mem_len, n), lambda i: (i, 0))
    return pl.pallas_call(
        _mem_kv_kernel,
        grid=(rows // mem_len,),
        in_specs=[row(D_MODEL), _resident(gn.shape), _resident(w.shape), _resident(bd.shape),
                  _resident(gk.shape)],
        out_specs=[row(MEM_DIM), row(MEM_DIM)],
        out_shape=[jax.ShapeDtypeStruct((rows, MEM_DIM), BF16)] * 2,
        compiler_params=_params("parallel"),
        name="mem_kv",
    )(mem2d, gn, w, bd, gk)


def _head_lane_mask(shape, head_in_pair):
    lane = lax.broadcasted_iota(jnp.int32, shape, len(shape) - 1)
    return (lane // HEAD_DIM) == head_in_pair


def _fox_attn_kernel(tq, q_ref, k_ref, kb_ref, v_ref, o_ref, m_ref, acc_ref):
    i = pl.program_id(2)
    tk = tq
    q2 = q_ref[0]
    zero = jnp.zeros_like(q2)
    lane = lax.broadcasted_iota(jnp.int32, q2.shape, 1)
    q_stack = jnp.concatenate(
        [jnp.concatenate(
            [jnp.where(_head_lane_mask(q2.shape, hh), q2, zero),
             ((lane >= BIAS_PARTS * hh) & (lane < BIAS_PARTS * (hh + 1))).astype(BF16)], axis=1)
         for hh in range(2)], axis=0)
    m_ref[...] = jnp.full(m_ref.shape, NEG_INF, F32)
    acc_ref[...] = jnp.zeros(acc_ref.shape, F32)
    on_or_below_diag = (lax.broadcasted_iota(jnp.int32, (tq, tk), 1)
                        <= lax.broadcasted_iota(jnp.int32, (tq, tk), 0))

    def key_rows(j):
        return pl.ds(pl.multiple_of(j * tk, tk), tk)

    def block(j, masked):
        rows = key_rows(j)
        kb = jnp.concatenate([k_ref[0, rows, :], kb_ref[0, rows, :]], axis=1)
        s2 = lax.dot_general(q_stack, kb, (((1,), (1,)), ((), ())), preferred_element_type=F32)
        vb = v_ref[0, rows, :]
        for hh in range(2):
            s = s2[hh * tq:(hh + 1) * tq, :]
            if masked:
                s = jnp.where(on_or_below_diag, s, NEG_INF)
            m_old = m_ref[hh]
            m_new = jnp.maximum(m_old, jnp.max(s, axis=-1, keepdims=True))
            alpha = jnp.exp2(m_old - m_new)
            p = jnp.concatenate([jnp.exp2(s[:, t0:t0 + LANES] - m_new)
                                 for t0 in range(0, tk, LANES)], axis=1).astype(BF16)
            v_h = jnp.where(_head_lane_mask(vb.shape, hh), vb, jnp.ones_like(vb))
            acc_ref[hh] = alpha * acc_ref[hh] + jnp.dot(p, v_h, preferred_element_type=F32)
            m_ref[hh] = m_new

    def pair(j2, _):
        block(2 * j2, masked=False)
        block(2 * j2 + 1, masked=False)
        return 0

    lax.fori_loop(0, i // 2, pair, 0)

    @pl.when(i % 2 == 1)
    def _():
        block(i - 1, masked=False)

    block(i, masked=True)
    acc0, acc1 = acc_ref[0], acc_ref[1]
    out0 = acc0 * (1.0 / acc0[:, HEAD_DIM:HEAD_DIM + 1])
    out1 = acc1 * (1.0 / acc1[:, 0:1])
    o_ref[0] = jnp.where(_head_lane_mask(out0.shape, 0), out0, out1).astype(o_ref.dtype)


def _fox_attn(q, k, kbias, v):
    b, s, _ = q.shape
    tq = min(ATTN_TILE, s)
    assert s % tq == 0
    seq = pl.BlockSpec((1, s, LANES), lambda bi, hp, i: (bi, 0, hp))
    tile = pl.BlockSpec((1, tq, LANES), lambda bi, hp, i: (bi, i, hp))
    return pl.pallas_call(
        functools.partial(_fox_attn_kernel, tq),
        grid=(b, FOX_HEADS // 2, s // tq),
        in_specs=[tile, seq, seq, seq],
        out_specs=tile,
        out_shape=jax.ShapeDtypeStruct((b, s, MIX_DIM), BF16),
        scratch_shapes=[pltpu.VMEM((2, tq, LANES), F32)] * 2,
        compiler_params=_params("parallel", "parallel", "arbitrary"),
        name="fox_attn",
    )(q, k, kbias, v)


def _mem_attn_kernel(q_ref, k_ref, v_ref, o_ref):
    for pair in range(MEM_DIM // LANES):
        lanes = slice(pair * LANES, (pair + 1) * LANES)
        q2 = q_ref[:, lanes]
        k2 = k_ref[:, lanes]
        v2 = v_ref[:, lanes]
        outs = []
        for hh in range(2):
            qh = jnp.where(_head_lane_mask(q2.shape, hh), q2, jnp.zeros_like(q2))
            s = lax.dot_general(qh, k2, (((1,), (1,)), ((), ())), preferred_element_type=F32)
            p = jnp.exp(s - jnp.max(s, axis=-1, keepdims=True))
            l = jnp.sum(p, axis=-1, keepdims=True)
            outs.append(jnp.dot(p.astype(BF16), v2, preferred_element_type=F32) * (1.0 / l))
        o_ref[:, lanes] = jnp.where(_head_lane_mask(outs[0].shape, 0), outs[0],
                                    outs[1]).astype(o_ref.dtype)


def _mem_attn(mq, mk, mv, seq_len, mem_len):
    t = mq.shape[0]
    tm = min(ROW_TILE, seq_len)
    per_seq = seq_len // tm
    kv = pl.BlockSpec((mem_len, MEM_DIM), lambda i: (i // per_seq, 0))
    return pl.pallas_call(
        _mem_attn_kernel,
        grid=(t // tm,),
        in_specs=[pl.BlockSpec((tm, MEM_DIM), lambda i: (i, 0)), kv, kv],
        out_specs=pl.BlockSpec((tm, MEM_DIM), lambda i: (i, 0)),
        out_shape=jax.ShapeDtypeStruct((t, MEM_DIM), BF16),
        compiler_params=_params("parallel"),
        name="mem_attn",
    )(mq, mk, mv)


def _conv_kernel(ts, u_ref, dw_ref, dwb_ref, lng_ref, lnb_ref, o_ref, ext_ref, sh_ref, acc_ref):
    @pl.when(pl.program_id(1) == 0)
    def _():
        ext_ref[0:CONV_HALO, :] = jnp.zeros((CONV_HALO, MIX_DIM), F32)

    @pl.when(pl.program_id(1) > 0)
    def _():
        ext_ref[0:CONV_HALO, :] = ext_ref[ts:ts + CONV_HALO, :]

    ext_ref[CONV_HALO:CONV_HALO + ts, :] = u_ref[0]

    base = CONV_HALO - (CONV_WIDTH - 1)
    n_shift = SUBLANES
    for b in range(n_shift):
        n_rows = ts + ((CONV_WIDTH - 1 - b) // n_shift) * n_shift
        sh_ref[b, 0:n_rows, :] = ext_ref[base + b:base + b + n_rows, :]
    rc = min(CONV_ROW_CHUNK, ts)
    groups = (rc // SUBLANES, SUBLANES, LANES)
    for r0 in range(0, ts, rc):
        for l0 in range(0, MIX_DIM, LANES):
            acc = jnp.zeros(groups, F32) + dwb_ref[:, l0:l0 + LANES]
            for j in range(CONV_WIDTH):
                a, b = divmod(j, n_shift)
                x = sh_ref[b, r0 + a * n_shift:r0 + a * n_shift + rc, l0:l0 + LANES]
                acc = acc + dw_ref[j, :, l0:l0 + LANES] * x.reshape(groups)
            acc_ref[r0:r0 + rc, l0:l0 + LANES] = acc.reshape(rc, LANES)

    y = acc_ref[...]
    mu = jnp.mean(y, axis=-1, keepdims=True)
    yc = y - mu
    var = jnp.mean(yc * yc, axis=-1, keepdims=True)
    z = yc * lax.rsqrt(var + EPS) * lng_ref[...] + lnb_ref[...]
    o_ref[0] = _silu(z).astype(o_ref.dtype)


def _conv_module(u, dw, dwb, lng, lnb):
    b, s, _ = u.shape
    ts = min(CONV_TILE, s)
    assert s % ts == 0 and ts >= CONV_HALO
    blk = pl.BlockSpec((1, ts, MIX_DIM), lambda bi, i: (bi, i, 0))
    return pl.pallas_call(
        functools.partial(_conv_kernel, ts),
        grid=(b, s // ts),
        in_specs=[blk, _resident(dw.shape), _resident(dwb.shape), _resident(lng.shape),
                  _resident(lnb.shape)],
        out_specs=blk,
        out_shape=jax.ShapeDtypeStruct((b, s, MIX_DIM), BF16),
        scratch_shapes=[pltpu.VMEM((ts + CONV_HALO, MIX_DIM), F32),
                        pltpu.VMEM((SUBLANES, ts + CONV_HALO - SUBLANES, MIX_DIM), F32),
                        pltpu.VMEM((ts, MIX_DIM), F32)],
        compiler_params=_params("parallel", "arbitrary"),
        name="conv_module",
    )(u, dw, dwb, lng, lnb)


def _outproj_math(h_ref, mix_ref, mem_ref, w_ref, gn_ref):
    h2 = (h_ref[...]
          + jnp.dot(mix_ref[...], w_ref[0:MIX_DIM, :], preferred_element_type=F32)
          + jnp.dot(mem_ref[...], w_ref[MIX_DIM:, :], preferred_element_type=F32))
    return h2, _rms(h2, gn_ref[...])


def _outproj_dense_kernel(h_ref, mix_ref, mem_ref, w_ref, gn_ref, h2_ref, z_ref):
    h2, z = _outproj_math(h_ref, mix_ref, mem_ref, w_ref, gn_ref)
    h2_ref[...] = h2
    z_ref[...] = z.astype(BF16)


def _outproj_router_kernel(h_ref, mix_ref, mem_ref, w_ref, gn_ref, wr_ref,
                           h2_ref, zp_ref, route_ref):
    h2, z = _outproj_math(h_ref, mix_ref, mem_ref, w_ref, gn_ref)
    h2_ref[...] = h2

    half = D_MODEL // 2
    lo_bits = pltpu.bitcast(z[:, :half].astype(BF16).astype(F32), jnp.uint32) >> 16
    hi_bits = pltpu.bitcast(z[:, half:].astype(BF16).astype(F32), jnp.uint32) & jnp.uint32(0xFFFF0000)
    packed = lo_bits | hi_bits
    for j in range(half // LANES):
        zp_ref[pl.ds(j, packed.shape[0], stride=PACKED_SLABS), :] = packed[:, j * LANES:(j + 1) * LANES]

    z_hi, z_lo = _split_bf16(z, 2)
    wr_hi, wr_lo = _split_bf16(wr_ref[...], 2)
    hi_terms = jnp.dot(z_hi, jnp.concatenate([wr_hi, wr_lo], axis=1),
                       preferred_element_type=F32)
    logits = (hi_terms[:, :LANES] + hi_terms[:, LANES:]
              + jnp.dot(z_lo, wr_hi, preferred_element_type=F32))
    lane = lax.broadcasted_iota(jnp.int32, logits.shape, 1)
    lane_f = lane.astype(F32)
    logits = jnp.where(lane < N_EXPERTS, logits, -jnp.inf)
    l1 = jnp.max(logits, axis=-1, keepdims=True)
    e1 = jnp.min(jnp.where(logits == l1, lane_f, float(LANES)), axis=-1, keepdims=True)
    rest = jnp.where(lane_f == e1, -jnp.inf, logits)
    l2 = jnp.max(rest, axis=-1, keepdims=True)
    e2 = jnp.min(jnp.where(rest == l2, lane_f, float(LANES)), axis=-1, keepdims=True)
    g2 = 1.0 / (1.0 + jnp.exp(l1 - l2))
    g1 = 1.0 - g2
    route = jnp.where(lane == 0, g1, 0.0) + jnp.where(lane == 1, g2, 0.0)
    route = route + jnp.where(lane == 2, e1, 0.0)
    route_ref[...] = route + jnp.where(lane == 3, e2, 0.0)


def _outproj(h, mix, mem, w, gn, router=None):
    t = h.shape[0]
    tm = min(ROW_TILE, t)
    assert t % tm == 0
    row = lambda n: pl.BlockSpec((tm, n), lambda i: (i, 0))
    in_specs = [row(D_MODEL), row(MIX_DIM), row(MEM_DIM), _resident(w.shape), _resident(gn.shape)]
    if router is None:
        return pl.pallas_call(
            _outproj_dense_kernel, grid=(t // tm,), in_specs=in_specs,
            out_specs=[row(D_MODEL), row(D_MODEL)],
            out_shape=[jax.ShapeDtypeStruct((t, D_MODEL), F32),
                       jax.ShapeDtypeStruct((t, D_MODEL), BF16)],
            compiler_params=_params("parallel"), name="outproj_dense",
        )(h, mix, mem, w, gn)
    return pl.pallas_call(
        _outproj_router_kernel, grid=(t // tm,),
        in_specs=in_specs + [_resident(router.shape)],
        out_specs=[row(D_MODEL), pl.BlockSpec((tm * PACKED_SLABS, LANES), lambda i: (i, 0)),
                   row(LANES)],
        out_shape=[jax.ShapeDtypeStruct((t, D_MODEL), F32),
                   jax.ShapeDtypeStruct((t * PACKED_SLABS, LANES), jnp.uint32),
                   jax.ShapeDtypeStruct((t, LANES), F32)],
        compiler_params=_params("parallel"), name="outproj_router",
    )(h, mix, mem, w, gn, router)


def _ffn_kernel(h_ref, z_ref, wg_ref, wu_ref, wd_ref, o_ref, acc_ref):
    z = z_ref[...]
    d_ff = wg_ref.shape[1]
    acc_ref[...] = h_ref[...]
    for f0 in range(0, d_ff, FFN_CHUNK):
        g = jnp.dot(z, wg_ref[:, f0:f0 + FFN_CHUNK], preferred_element_type=F32)
        u = jnp.dot(z, wu_ref[:, f0:f0 + FFN_CHUNK], preferred_element_type=F32)
        a = (_silu(g) * u).astype(BF16)
        acc_ref[...] += jnp.dot(a, wd_ref[f0:f0 + FFN_CHUNK, :], preferred_element_type=F32)
    o_ref[...] = acc_ref[...]


def _ffn(h2, z, wg, wu, wd):
    t = h2.shape[0]
    tm = min(ROW_TILE, t)
    assert t % tm == 0 and wg.shape[1] % FFN_CHUNK == 0
    row = pl.BlockSpec((tm, D_MODEL), lambda i: (i, 0))
    return pl.pallas_call(
        _ffn_kernel, grid=(t // tm,),
        in_specs=[row, row, _resident(wg.shape), _resident(wu.shape), _resident(wd.shape)],
        out_specs=row,
        out_shape=jax.ShapeDtypeStruct((t, D_MODEL), F32),
        scratch_shapes=[pltpu.VMEM((tm, D_MODEL), F32)],
        compiler_params=_params("parallel"), name="ffn_dense",
    )(h2, z, wg, wu, wd)


def _token_copy(src, src_row, dst, dst_row, slabs, sem):
    return pltpu.make_async_copy(src.at[pl.ds(src_row, slabs), :], dst.at[pl.ds(dst_row, slabs), :],
                                 sem)


def _block_copy(src, dst, slabs, sem):
    n = EXPERT_ROWS * slabs
    return pltpu.make_async_copy(src.at[pl.ds(0, n), :], dst.at[pl.ds(0, n), :], sem)


def _experts_kernel(n_f, be_ref, nact_ref, tok_cur_ref, tok_next_ref, dst_prev_ref, dst_cur_ref,
                    zp_hbm, wg_ref, wu_ref, wd_ref, y_hbm, xp_ref, x_ref, acc_ref, stage_ref,
                    sem_in, sem_out):
    del be_ref
    i = pl.program_id(0)
    f = pl.program_id(1)
    n_active = nact_ref[0]
    active = i < n_active
    has_next = i + 1 < n_active
    has_prev = i >= 1
    first_step = f == 0
    last_step = f == n_f - 1
    half = D_MODEL // 2

    def gather(idx_ref, r, priority=0):
        src = pl.multiple_of(idx_ref[0, 0, r], PACKED_SLABS)
        _token_copy(zp_hbm, src, xp_ref, r * PACKED_SLABS, PACKED_SLABS, sem_in).start(
            priority=priority)

    def scatter(idx_ref, r, priority=0):
        dst = pl.multiple_of(idx_ref[0, 0, r], OUT_SLABS)
        _token_copy(stage_ref, r * OUT_SLABS, y_hbm, dst, OUT_SLABS, sem_out).start(
            priority=priority)

    def gather_next(r, parity):
        del parity
        gather(tok_next_ref, r)

    def scatter_prev(r, parity):
        scatter(dst_prev_ref, r, priority=parity)

    def rolled(fn):
        def body(r, _):
            fn(r)
            return 0
        lax.fori_loop(0, EXPERT_ROWS, body, 0)

    def compute(phase, inline_dma=None):
        def sub_block(s, _):
            r0 = pl.multiple_of(s * EXPERT_SUB, EXPERT_SUB)
            rows = pl.ds(r0, EXPERT_SUB)
            x = x_ref[rows, :]
            g = jnp.dot(x, wg_ref[0, 0], preferred_element_type=F32)
            u = jnp.dot(x, wu_ref[0, 0], preferred_element_type=F32)
            a = (_silu(g) * u).astype(BF16)
            c = jnp.dot(a, wd_ref[0, 0], preferred_element_type=F32)
            if phase == "first":
                acc_ref[rows, :] = c
            elif phase == "mid":
                acc_ref[rows, :] += c
            else:
                out = acc_ref[rows, :] + c
                for j in range(OUT_SLABS):
                    stage_ref[pl.ds(r0 * OUT_SLABS + j, EXPERT_SUB, stride=OUT_SLABS), :] = (
                        out[:, j * LANES:(j + 1) * LANES])
            if inline_dma is not None:
                for k in range(EXPERT_SUB):
                    inline_dma(r0 + k, k % 2)
            return 0
        lax.fori_loop(0, EXPERT_ROWS // EXPERT_SUB, sub_block, 0)

    @pl.when((i == 0) & first_step)
    def _():
        stage_ref[...] = jnp.zeros(stage_ref.shape, F32)
        n_sink = EXPERT_ROWS * OUT_SLABS
        sink = pltpu.make_async_copy(
            stage_ref, y_hbm.at[pl.ds(y_hbm.shape[0] - n_sink, n_sink), :], sem_out)
        sink.start()
        sink.wait()
        rolled(functools.partial(gather, tok_cur_ref))

    @pl.when(active & first_step)
    def _():
        _block_copy(zp_hbm, xp_ref, PACKED_SLABS, sem_in).wait()
        for j in range(PACKED_SLABS):
            xp = xp_ref[pl.ds(j, EXPERT_ROWS, stride=PACKED_SLABS), :]
            lanes = slice(j * LANES, (j + 1) * LANES)
            x_ref[:, lanes] = pltpu.bitcast(xp << 16, F32).astype(BF16)
            x_ref[:, half + j * LANES:half + (j + 1) * LANES] = pltpu.bitcast(
                xp & jnp.uint32(0xFFFF0000), F32).astype(BF16)

    @pl.when(active & first_step & has_prev)
    def _():
        compute("first", scatter_prev)

    @pl.when(active & first_step & jnp.logical_not(has_prev))
    def _():
        compute("first")

    if n_f > 2:
        @pl.when(active & jnp.logical_not(first_step) & jnp.logical_not(last_step))
        def _():
            compute("mid")

    @pl.when(active & last_step & has_prev)
    def _():
        _block_copy(stage_ref, y_hbm, OUT_SLABS, sem_out).wait()

    @pl.when(active & last_step & has_next)
    def _():
        compute("last", gather_next)

    @pl.when(active & last_step & jnp.logical_not(has_next))
    def _():
        compute("last")
        rolled(functools.partial(scatter, dst_cur_ref))
        _block_copy(stage_ref, y_hbm, OUT_SLABS, sem_out).wait()


def _experts(zp, plan, layer, wg, wu, wd):
    t = zp.shape[0] // PACKED_SLABS
    blk_expert, n_active, tok_buf, dst_buf = plan
    nb = blk_expert.shape[0]
    d_ff = wg.shape[3]
    tf = EXPERT_FF_TILE
    n_f = d_ff // tf
    assert d_ff % tf == 0 and n_f >= 2
    last = nb - 1

    def idx(shift):
        return pl.BlockSpec((1, 1, EXPERT_ROWS),
                            lambda i, f, *_: (jnp.clip(i + shift, 0, last), 0, 0),
                            memory_space=pltpu.SMEM)

    grid_spec = pltpu.PrefetchScalarGridSpec(
        num_scalar_prefetch=2,
        grid=(nb, n_f),
        in_specs=[idx(0), idx(1), idx(-1), idx(0), pl.BlockSpec(memory_space=pl.ANY),
                  pl.BlockSpec((1, 1, D_MODEL, tf), lambda i, f, be, *_: (layer, be[i], 0, f)),
                  pl.BlockSpec((1, 1, D_MODEL, tf), lambda i, f, be, *_: (layer, be[i], 0, f)),
                  pl.BlockSpec((1, 1, tf, D_MODEL), lambda i, f, be, *_: (layer, be[i], f, 0))],
        out_specs=pl.BlockSpec(memory_space=pl.ANY),
        scratch_shapes=[pltpu.VMEM((EXPERT_ROWS * PACKED_SLABS, LANES), jnp.uint32),
                        pltpu.VMEM((EXPERT_ROWS, D_MODEL), BF16),
                        pltpu.VMEM((EXPERT_ROWS, D_MODEL), F32),
                        pltpu.VMEM((EXPERT_ROWS * OUT_SLABS, LANES), F32),
                        pltpu.SemaphoreType.DMA(()), pltpu.SemaphoreType.DMA(())],
    )
    return pl.pallas_call(
        functools.partial(_experts_kernel, n_f), grid_spec=grid_spec,
        out_shape=jax.ShapeDtypeStruct(((TOP_K * t + EXPERT_ROWS) * OUT_SLABS, LANES), F32),
        compiler_params=_params("arbitrary", "arbitrary"), name="moe_experts",
    )(blk_expert, n_active, tok_buf, tok_buf, dst_buf, dst_buf, zp, wg, wu, wd)


def _route_plan(route, t):
    a = t * TOP_K
    nb = a // EXPERT_ROWS + N_EXPERTS
    rows = nb * EXPERT_ROWS
    flat_e = route[:, 2:4].astype(jnp.int32).reshape(a)
    experts = jnp.arange(N_EXPERTS, dtype=jnp.int32)
    sizes = jnp.sum((flat_e[:, None] == experts[None, :]).astype(jnp.int32), axis=0)
    start = jnp.cumsum(sizes) - sizes
    padded = ((sizes + EXPERT_ROWS - 1) // EXPERT_ROWS) * EXPERT_ROWS
    pad_end = jnp.cumsum(padded)
    pad_start = pad_end - padded
    sorted_assign = jnp.sort(flat_e * a + jnp.arange(a, dtype=jnp.int32)) % a
    blk_start = jnp.arange(nb, dtype=jnp.int32) * EXPERT_ROWS
    blk_expert = jnp.minimum(jnp.searchsorted(pad_end, blk_start, side="right"),
                             N_EXPERTS - 1).astype(jnp.int32)
    row = jnp.arange(rows, dtype=jnp.int32)
    row_e = jnp.repeat(blk_expert, EXPERT_ROWS)
    rank = row - pad_start[row_e]
    real = (rank >= 0) & (rank < sizes[row_e])
    assign = sorted_assign[jnp.clip(start[row_e] + rank, 0, a - 1)]
    tok = assign // TOP_K
    tok_buf = jnp.where(real, tok, 0)
    dst_buf = jnp.where(real, tok + (assign % TOP_K) * t, TOP_K * t + row % EXPERT_ROWS)
    n_active = (pad_end[-1] // EXPERT_ROWS).reshape(1).astype(jnp.int32)
    return (blk_expert, n_active, (tok_buf * PACKED_SLABS).reshape(nb, 1, EXPERT_ROWS),
            (dst_buf * OUT_SLABS).reshape(nb, 1, EXPERT_ROWS))


def _combine_kernel(h_ref, y0_ref, y1_ref, route_ref, o_ref):
    o_ref[...] = _moe_combined(h_ref, y0_ref, y1_ref, route_ref)


def _combine(h2, y, route):
    t = h2.shape[0]
    tm = min(ROW_TILE, t)
    nt = t // tm
    row = pl.BlockSpec((tm, D_MODEL), lambda i: (i, 0))
    return pl.pallas_call(
        _combine_kernel, grid=(nt,),
        in_specs=[row, pl.BlockSpec((tm * OUT_SLABS, LANES), lambda i: (i, 0)),
                  pl.BlockSpec((tm * OUT_SLABS, LANES), lambda i: (i + nt, 0)),
                  pl.BlockSpec((tm, LANES), lambda i: (i, 0))],
        out_specs=row,
        out_shape=jax.ShapeDtypeStruct((t, D_MODEL), F32),
        compiler_params=_params("parallel"), name="moe_combine",
    )(h2, y, y, route)


def _row(v):
    return v.reshape(1, -1).astype(F32)


def _tile_heads(g, n, scale=1.0):
    return _row(jnp.tile(g.astype(F32) * scale, n))


def kernel(x, mem, norm_mix, norm_mem, norm_ffn, w_mem_kv, g_mq, g_mk, fox_w_in, fox_b_f, fox_g_q, fox_g_k, fox_w_out, conv_w_in, conv_b_in, conv_dw, conv_dw_b, conv_ln_g, conv_ln_b, conv_w_out, ffn_w_gate, ffn_w_up, ffn_w_down, moe_router, moe_w_gate, moe_w_up, moe_w_down):
    b, s, d = x.shape
    mem_len = mem.shape[1]
    t = b * s
    assert d == D_MODEL
    scale = HEAD_DIM ** -0.5

    group = jnp.arange(MXU_WIDTH, dtype=jnp.int32) // HEAD_DIM
    ones_bd = (group[:, None] == group[None, :]).astype(BF16)

    moe_wg, moe_wu, moe_wd = (w.astype(BF16) for w in (moe_w_gate, moe_w_up, moe_w_down))
    h = x.reshape(t, d)
    pending = None
    mem2d = mem.reshape(b * mem_len, d)
    for i in range(DEPTH):
        j = i // 2
        gmq = _tile_heads(g_mq[i], MEM_DIM // HEAD_DIM, scale)
        mk, mv = _mem_kv(mem2d, mem_len, _row(norm_mem[i]), w_mem_kv[i].astype(BF16),
                         ones_bd, _tile_heads(g_mk[i], MEM_DIM // HEAD_DIM))
        if i % 2 == 0:
            w = fox_w_in[j]
            m3 = 3 * MIX_DIM
            w_cat = jnp.concatenate(
                [w[:, :m3], w[:, m3 + FOX_HEADS:], w[:, m3:m3 + FOX_HEADS],
                 jnp.zeros((d, F_LANES - FOX_HEADS), w.dtype)], axis=1).astype(BF16)
            bf = jnp.pad(fox_b_f[j].astype(F32), (0, F_LANES - FOX_HEADS)).reshape(1, F_LANES)
            h, q, k, v, mq, kbias = _fox_inproj(
                h, s, _row(norm_mix[i]), w_cat, ones_bd,
                _tile_heads(fox_g_q[j], FOX_HEADS, scale * LOG2E),
                _tile_heads(fox_g_k[j], FOX_HEADS), gmq, bf, moe=pending)
            pending = None
            seq = lambda a: a.reshape(b, s, MIX_DIM)
            mix = _fox_attn(seq(q), seq(k), seq(kbias), seq(v)).reshape(t, MIX_DIM)
            w_out = fox_w_out[j]
        else:
            u, mq = _conv_inproj(h, s, _row(norm_mix[i]), conv_w_in[j].astype(BF16),
                                 _row(conv_b_in[j]), ones_bd, gmq)
            dw8 = jnp.broadcast_to(conv_dw[j].astype(F32)[:, None, :],
                                   (CONV_WIDTH, SUBLANES, MIX_DIM))
            mix = _conv_module(u.reshape(b, s, MIX_DIM), dw8,
                               _row(conv_dw_b[j]), _row(conv_ln_g[j]),
                               _row(conv_ln_b[j])).reshape(t, MIX_DIM)
            w_out = conv_w_out[j]
        mem_out = _mem_attn(mq, mk, mv, s, mem_len)
        if i % 2 == 0:
            h2, z = _outproj(h, mix, mem_out, w_out.astype(BF16), _row(norm_ffn[i]))
            h = _ffn(h2, z, ffn_w_gate[j].astype(BF16), ffn_w_up[j].astype(BF16),
                     ffn_w_down[j].astype(BF16))
        else:
            wr = jnp.pad(moe_router[j].astype(F32), ((0, 0), (0, LANES - N_EXPERTS)))
            h2, zp, route = _outproj(h, mix, mem_out, w_out.astype(BF16), _row(norm_ffn[i]),
                                     router=wr)
            plan = _route_plan(route, t)
            y = _experts(zp, plan, j, moe_wg, moe_wu, moe_wd)
            if i + 1 < DEPTH and (i + 1) % 2 == 0:
                h, pending = h2, (y, route)
            else:
                h = _combine(h2, y, route)
    return h.reshape(b, s, d)
```

```python
import functools

import jax
import jax.numpy as jnp
from jax import lax
from jax.experimental import pallas as pl
from jax.experimental.pallas import tpu as pltpu

F32 = jnp.float32
BF16 = jnp.bfloat16

D_MODEL = 1024
HEAD_DIM = 64
MEM_DIM = 256
MIX_DIM = D_MODEL - MEM_DIM
FOX_HEADS = MIX_DIM // HEAD_DIM
CONV_WIDTH = 31
N_EXPERTS = 8
TOP_K = 2
DEPTH = 4
EPS = 1e-6
NEG_INF = -1e30
LOG2E = 1.4426950408889634
BIAS_PARTS = 3
PART_STRIDE = 16

LANES = 128
SUBLANES = 8
MXU_WIDTH = 256
VMEM_LIMIT_BYTES = 56 * 1024 * 1024

ROW_TILE = 512
ATTN_TILE = 512
CONV_TILE = 256
CONV_HALO = 32
CONV_ROW_CHUNK = 64
FFN_CHUNK = 256
EXPERT_ROWS = 1024
EXPERT_SUB = 256
EXPERT_FF_TILE = 1792
F_LANES = LANES
PACKED_SLABS = D_MODEL // 2 // LANES
OUT_SLABS = D_MODEL // LANES


def _params(*sem):
    return pltpu.CompilerParams(dimension_semantics=sem, vmem_limit_bytes=VMEM_LIMIT_BYTES)


def _resident(shape):
    nd = len(shape)
    return pl.BlockSpec(shape, lambda *_: (0,) * nd, pipeline_mode=pl.Buffered(1))


def _rms(x, g):
    ms = jnp.mean(x * x, axis=-1, keepdims=True)
    return x * lax.rsqrt(ms + EPS) * g


def _head_rms(x, ones_bd, g):
    w = ones_bd.shape[0]
    sq = (x * x).astype(BF16)
    ss = jnp.concatenate([jnp.dot(sq[:, c:c + w], ones_bd, preferred_element_type=F32)
                          for c in range(0, x.shape[1], w)], axis=1)
    return x * lax.rsqrt(ss * (1.0 / HEAD_DIM) + EPS) * g


def _silu(x):
    return x * (1.0 / (1.0 + jnp.exp(-x)))


def _log_sigmoid(x):
    return jnp.minimum(x, 0.0) - jnp.log(1.0 + jnp.exp(-jnp.abs(x)))


def _row_prefix_sum(x):
    n = x.shape[0]
    row = lax.broadcasted_iota(jnp.int32, x.shape, 0)
    s = 1
    while s < n:
        x = x + jnp.where(row >= s, pltpu.roll(x, s, 0), 0.0)
        s *= 2
    return x


def _split_bf16(x, parts):
    out = []
    for _ in range(parts - 1):
        hi = x.astype(BF16)
        out.append(hi)
        x = x - hi.astype(F32)
    out.append(x.astype(BF16))
    return out


def _moe_combined(h2_ref, y0_ref, y1_ref, route_ref):
    g = route_ref[...]
    cols = []
    for j in range(OUT_SLABS):
        rows = pl.ds(j, h2_ref.shape[0], stride=OUT_SLABS)
        cols.append(h2_ref[:, j * LANES:(j + 1) * LANES]
                    + (g[:, 0:1] * y0_ref[rows, :] + g[:, 1:2] * y1_ref[rows, :]))
    return jnp.concatenate(cols, axis=1)


def _fox_inproj_moe_kernel(tiles_per_seq, h2_ref, y0_ref, y1_ref, route_ref, *rest):
    ins, h_ref, tail = rest[:8], rest[8], rest[9:]
    h = _moe_combined(h2_ref, y0_ref, y1_ref, route_ref)
    h_ref[...] = h
    _fox_inproj_body(tiles_per_seq, h, *ins, *tail)


def _fox_inproj_kernel(tiles_per_seq, h_ref, *rest):
    _fox_inproj_body(tiles_per_seq, h_ref[...], *rest)


def _fox_inproj_body(tiles_per_seq, h, gn_ref, w_ref, bd_ref, gq_ref, gk_ref, gmq_ref,
                     bf_ref, place_ref, q_ref, k_ref, v_ref, mq_ref, kb_ref, carry_ref):
    xb = _rms(h, gn_ref[...]).astype(BF16)
    m = MIX_DIM
    q = jnp.dot(xb, w_ref[:, 0:m], preferred_element_type=F32)
    q_ref[...] = _head_rms(q, bd_ref[...], gq_ref[...]).astype(BF16)
    k = jnp.dot(xb, w_ref[:, m:2 * m], preferred_element_type=F32)
    k_ref[...] = _head_rms(k, bd_ref[...], gk_ref[...]).astype(BF16)
    v_ref[...] = jnp.dot(xb, w_ref[:, 2 * m:3 * m], preferred_element_type=F32).astype(BF16)
    mq = jnp.dot(xb, w_ref[:, 3 * m:3 * m + MEM_DIM], preferred_element_type=F32)
    mq_ref[...] = _head_rms(mq, bd_ref[...], gmq_ref[...]).astype(BF16)

    f = jnp.dot(xb, w_ref[:, 3 * m + MEM_DIM:], preferred_element_type=F32) + bf_ref[...]
    lane = lax.broadcasted_iota(jnp.int32, f.shape, 1)
    lf = jnp.where(lane < FOX_HEADS, _log_sigmoid(f), 0.0)

    @pl.when(pl.program_id(0) % tiles_per_seq == 0)
    def _():
        carry_ref[...] = jnp.zeros_like(carry_ref)

    c = _row_prefix_sum(lf) + carry_ref[0:1, :]
    carry_ref[0:1, :] = c[-1:, :]
    parts = _split_bf16(c * (-LOG2E), BIAS_PARTS)
    parked = parts[0].astype(F32)
    for p in range(1, BIAS_PARTS):
        parked = parked + pltpu.roll(parts[p].astype(F32), PART_STRIDE * p, 1)
    kb_ref[...] = jnp.dot(parked.astype(BF16), place_ref[...],
                          preferred_element_type=F32).astype(BF16)


def _bias_placement():
    row = jnp.arange(F_LANES, dtype=jnp.int32)[:, None]
    p, h = row // PART_STRIDE, row % PART_STRIDE
    col = jnp.arange(MIX_DIM, dtype=jnp.int32)[None, :]
    return ((p < BIAS_PARTS) & (h < FOX_HEADS)
            & (col == (h // 2) * LANES + BIAS_PARTS * (h % 2) + p)).astype(BF16)


def _fox_inproj(h, seq_len, gn, w, bd, gq, gk, gmq, bf, moe=None):
    t = h.shape[0]
    tm = min(ROW_TILE, seq_len)
    assert seq_len % tm == 0
    nt = t // tm
    row = lambda n: pl.BlockSpec((tm, n), lambda i: (i, 0))
    place = _bias_placement()
    params = (gn, w, bd, gq, gk, gmq, bf, place)
    in_specs = [row(D_MODEL)] + [_resident(p.shape) for p in params]
    out_specs = [row(MIX_DIM), row(MIX_DIM), row(MIX_DIM), row(MEM_DIM), row(MIX_DIM)]
    out_shape = ([jax.ShapeDtypeStruct((t, MIX_DIM), BF16)] * 3
                 + [jax.ShapeDtypeStruct((t, MEM_DIM), BF16),
                    jax.ShapeDtypeStruct((t, MIX_DIM), BF16)])
    common = dict(grid=(nt,), scratch_shapes=[pltpu.VMEM((SUBLANES, F_LANES), F32)],
                  compiler_params=_params("arbitrary"))
    if moe is None:
        outs = pl.pallas_call(
            functools.partial(_fox_inproj_kernel, seq_len // tm), in_specs=in_specs,
            out_specs=out_specs, out_shape=out_shape, name="fox_inproj", **common,
        )(h, *params)
        return (h, *outs)
    y, route = moe
    y_spec = lambda k: pl.BlockSpec((tm * OUT_SLABS, LANES), lambda i: (i + k * nt, 0))
    return pl.pallas_call(
        functools.partial(_fox_inproj_moe_kernel, seq_len // tm),
        in_specs=[row(D_MODEL), y_spec(0), y_spec(1), row(LANES)] + in_specs[1:],
        out_specs=[row(D_MODEL)] + out_specs,
        out_shape=[jax.ShapeDtypeStruct((t, D_MODEL), F32)] + out_shape,
        name="fox_inproj_moe", **common,
    )(h, y, y, route, *params)


def _conv_inproj_kernel(h_ref, gn_ref, w_ref, b_ref, bd_ref, gmq_ref, u_ref, mq_ref):
    xb = _rms(h_ref[...], gn_ref[...]).astype(BF16)
    m = MIX_DIM
    a = jnp.dot(xb, w_ref[:, 0:m], preferred_element_type=F32) + b_ref[:, 0:m]
    g = jnp.dot(xb, w_ref[:, m:2 * m], preferred_element_type=F32) + b_ref[:, m:2 * m]
    u_ref[...] = a * (1.0 / (1.0 + jnp.exp(-g)))
    mq = jnp.dot(xb, w_ref[:, 2 * m:], preferred_element_type=F32)
    mq_ref[...] = _head_rms(mq, bd_ref[...], gmq_ref[...]).astype(BF16)


def _conv_inproj(h, seq_len, gn, w, b, bd, gmq):
    t = h.shape[0]
    tm = min(ROW_TILE, seq_len)
    assert t % tm == 0
    row = lambda n: pl.BlockSpec((tm, n), lambda i: (i, 0))
    return pl.pallas_call(
        _conv_inproj_kernel,
        grid=(t // tm,),
        in_specs=[row(D_MODEL), _resident(gn.shape), _resident(w.shape), _resident(b.shape),
                  _resident(bd.shape), _resident(gmq.shape)],
        out_specs=[row(MIX_DIM), row(MEM_DIM)],
        out_shape=[jax.ShapeDtypeStruct((t, MIX_DIM), F32),
                   jax.ShapeDtypeStruct((t, MEM_DIM), BF16)],
        compiler_params=_params("parallel"),
        name="conv_inproj",
    )(h, gn, w, b, bd, gmq)


def _mem_kv_kernel(m_ref, gn_ref, w_ref, bd_ref, gk_ref, mk_ref, mv_ref):
    xb = _rms(m_ref[...], gn_ref[...]).astype(BF16)
    mk = jnp.dot(xb, w_ref[:, 0:MEM_DIM], preferred_element_type=F32)
    mk_ref[...] = _head_rms(mk, bd_ref[...], gk_ref[...]).astype(BF16)
    mv_ref[...] = jnp.dot(xb, w_ref[:, MEM_DIM:], preferred_element_type=F32).astype(BF16)


def _mem_kv(mem2d, mem_len, gn, w, bd, gk):
    rows = mem2d.shape[0]
    row = lambda n: pl.BlockSpec((mem_len, n), lambda i: (i, 0))
    return pl.pallas_call(
        _mem_kv_kernel,
        grid=(rows // mem_len,),
        in_specs=[row(D_MODEL), _resident(gn.shape), _resident(w.shape), _resident(bd.shape),
                  _resident(gk.shape)],
        out_specs=[row(MEM_DIM), row(MEM_DIM)],
        out_shape=[jax.ShapeDtypeStruct((rows, MEM_DIM), BF16)] * 2,
        compiler_params=_params("parallel"),
        name="mem_kv",
    )(mem2d, gn, w, bd, gk)


def _head_lane_mask(shape, head_in_pair):
    lane = lax.broadcasted_iota(jnp.int32, shape, len(shape) - 1)
    return (lane // HEAD_DIM) == head_in_pair


def _fox_attn_kernel(tq, q_ref, k_ref, kb_ref, v_ref, o_ref, m_ref, acc_ref):
    i = pl.program_id(2)
    tk = tq
    q2 = q_ref[0]
    zero = jnp.zeros_like(q2)
    lane = lax.broadcasted_iota(jnp.int32, q2.shape, 1)
    q_stack = jnp.concatenate(
        [jnp.concatenate(
            [jnp.where(_head_lane_mask(q2.shape, hh), q2, zero),
             ((lane >= BIAS_PARTS * hh) & (lane < BIAS_PARTS * (hh + 1))).astype(BF16)], axis=1)
         for hh in range(2)], axis=0)
    m_ref[...] = jnp.full(m_ref.shape, NEG_INF, F32)
    acc_ref[...] = jnp.zeros(acc_ref.shape, F32)
    on_or_below_diag = (lax.broadcasted_iota(jnp.int32, (tq, tk), 1)
                        <= lax.broadcasted_iota(jnp.int32, (tq, tk), 0))

    def key_rows(j):
        return pl.ds(pl.multiple_of(j * tk, tk), tk)

    def block(j, masked):
        rows = key_rows(j)
        kb = jnp.concatenate([k_ref[0, rows, :], kb_ref[0, rows, :]], axis=1)
        s2 = lax.dot_general(q_stack, kb, (((1,), (1,)), ((), ())), preferred_element_type=F32)
        vb = v_ref[0, rows, :]
        for hh in range(2):
            s = s2[hh * tq:(hh + 1) * tq, :]
            if masked:
                s = jnp.where(on_or_below_diag, s, NEG_INF)
            m_old = m_ref[hh]
            m_new = jnp.maximum(m_old, jnp.max(s, axis=-1, keepdims=True))
            alpha = jnp.exp2(m_old - m_new)
            p = jnp.concatenate([jnp.exp2(s[:, t0:t0 + LANES] - m_new)
                                 for t0 in range(0, tk, LANES)], axis=1).astype(BF16)
            v_h = jnp.where(_head_lane_mask(vb.shape, hh), vb, jnp.ones_like(vb))
            acc_ref[hh] = alpha * acc_ref[hh] + jnp.dot(p, v_h, preferred_element_type=F32)
            m_ref[hh] = m_new

    def pair(j2, _):
        block(2 * j2, masked=False)
        block(2 * j2 + 1, masked=False)
        return 0

    lax.fori_loop(0, i // 2, pair, 0)

    @pl.when(i % 2 == 1)
    def _():
        block(i - 1, masked=False)
        block(i, masked=True)

    @pl.when(i % 2 == 0)
    def _():
        block(i, masked=True)

    acc0, acc1 = acc_ref[0], acc_ref[1]
    out0 = acc0 * (1.0 / acc0[:, HEAD_DIM:HEAD_DIM + 1])
    out1 = acc1 * (1.0 / acc1[:, 0:1])
    o_ref[0] = jnp.where(_head_lane_mask(out0.shape, 0), out0, out1).astype(o_ref.dtype)


def _fox_attn(q, k, kbias, v):
    b, s, _ = q.shape
    tq = min(ATTN_TILE, s)
    assert s % tq == 0
    seq = pl.BlockSpec((1, s, LANES), lambda bi, hp, i: (bi, 0, hp))
    tile = pl.BlockSpec((1, tq, LANES), lambda bi, hp, i: (bi, i, hp))
    return pl.pallas_call(
        functools.partial(_fox_attn_kernel, tq),
        grid=(b, FOX_HEADS // 2, s // tq),
        in_specs=[tile, seq, seq, seq],
        out_specs=tile,
        out_shape=jax.ShapeDtypeStruct((b, s, MIX_DIM), BF16),
        scratch_shapes=[pltpu.VMEM((2, tq, LANES), F32)] * 2,
        compiler_params=_params("parallel", "parallel", "arbitrary"),
        name="fox_attn",
    )(q, k, kbias, v)


def _mem_attn_kernel(q_ref, k_ref, v_ref, o_ref):
    for pair in range(MEM_DIM // LANES):
        lanes = slice(pair * LANES, (pair + 1) * LANES)
        q2 = q_ref[:, lanes]
        k2 = k_ref[:, lanes]
        v2 = v_ref[:, lanes]
        outs = []
        for hh in range(2):
            qh = jnp.where(_head_lane_mask(q2.shape, hh), q2, jnp.zeros_like(q2))
            s = lax.dot_general(qh, k2, (((1,), (1,)), ((), ())), preferred_element_type=F32)
            p = jnp.exp(s - jnp.max(s, axis=-1, keepdims=True))
            l = jnp.sum(p, axis=-1, keepdims=True)
            outs.append(jnp.dot(p.astype(BF16), v2, preferred_element_type=F32) * (1.0 / l))
        o_ref[:, lanes] = jnp.where(_head_lane_mask(outs[0].shape, 0), outs[0],
                                    outs[1]).astype(o_ref.dtype)


def _mem_attn(mq, mk, mv, seq_len, mem_len):
    t = mq.shape[0]
    tm = min(ROW_TILE, seq_len)
    per_seq = seq_len // tm
    kv = pl.BlockSpec((mem_len, MEM_DIM), lambda i: (i // per_seq, 0))
    return pl.pallas_call(
        _mem_attn_kernel,
        grid=(t // tm,),
        in_specs=[pl.BlockSpec((tm, MEM_DIM), lambda i: (i, 0)), kv, kv],
        out_specs=pl.BlockSpec((tm, MEM_DIM), lambda i: (i, 0)),
        out_shape=jax.ShapeDtypeStruct((t, MEM_DIM), BF16),
        compiler_params=_params("parallel"),
        name="mem_attn",
    )(mq, mk, mv)


def _conv_kernel(ts, u_ref, dw_ref, dwb_ref, lng_ref, lnb_ref, o_ref, ext_ref, sh_ref, acc_ref):
    @pl.when(pl.program_id(1) == 0)
    def _():
        ext_ref[0:CONV_HALO, :] = jnp.zeros((CONV_HALO, MIX_DIM), F32)

    @pl.when(pl.program_id(1) > 0)
    def _():
        ext_ref[0:CONV_HALO, :] = ext_ref[ts:ts + CONV_HALO, :]

    ext_ref[CONV_HALO:CONV_HALO + ts, :] = u_ref[0]

    base = CONV_HALO - (CONV_WIDTH - 1)
    n_shift = SUBLANES
    for b in range(n_shift):
        n_rows = ts + ((CONV_WIDTH - 1 - b) // n_shift) * n_shift
        sh_ref[b, 0:n_rows, :] = ext_ref[base + b:base + b + n_rows, :]
    rc = min(CONV_ROW_CHUNK, ts)
    for r0 in range(0, ts, rc):
        for l0 in range(0, MIX_DIM, LANES):
            groups = (rc // SUBLANES, SUBLANES, LANES)
            acc = jnp.zeros(groups, F32) + dwb_ref[:, l0:l0 + LANES]
            for j in range(CONV_WIDTH):
                a, b = divmod(j, n_shift)
                x = sh_ref[b, r0 + a * n_shift:r0 + a * n_shift + rc, l0:l0 + LANES]
                w = jnp.broadcast_to(dw_ref[j, :, l0:l0 + LANES][None], groups)
                acc = acc + w * x.reshape(groups)
            acc_ref[r0:r0 + rc, l0:l0 + LANES] = acc.reshape(rc, LANES)

    y = acc_ref[...]
    mu = jnp.mean(y, axis=-1, keepdims=True)
    yc = y - mu
    var = jnp.mean(yc * yc, axis=-1, keepdims=True)
    z = yc * lax.rsqrt(var + EPS) * lng_ref[...] + lnb_ref[...]
    o_ref[0] = _silu(z).astype(o_ref.dtype)


def _conv_module(u, dw, dwb, lng, lnb):
    b, s, _ = u.shape
    ts = min(CONV_TILE, s)
    assert s % ts == 0 and ts >= CONV_HALO
    blk = pl.BlockSpec((1, ts, MIX_DIM), lambda bi, i: (bi, i, 0))
    return pl.pallas_call(
        functools.partial(_conv_kernel, ts),
        grid=(b, s // ts),
        in_specs=[blk, _resident(dw.shape), _resident(dwb.shape), _resident(lng.shape),
                  _resident(lnb.shape)],
        out_specs=blk,
        out_shape=jax.ShapeDtypeStruct((b, s, MIX_DIM), BF16),
        scratch_shapes=[pltpu.VMEM((ts + CONV_HALO, MIX_DIM), F32),
                        pltpu.VMEM((SUBLANES, ts + CONV_HALO - SUBLANES, MIX_DIM), F32),
                        pltpu.VMEM((ts, MIX_DIM), F32)],
        compiler_params=_params("parallel", "arbitrary"),
        name="conv_module",
    )(u, dw, dwb, lng, lnb)


def _outproj_math(h_ref, mix_ref, mem_ref, w_ref, gn_ref):
    h2 = (h_ref[...]
          + jnp.dot(mix_ref[...], w_ref[0:MIX_DIM, :], preferred_element_type=F32)
          + jnp.dot(mem_ref[...], w_ref[MIX_DIM:, :], preferred_element_type=F32))
    return h2, _rms(h2, gn_ref[...])


def _outproj_dense_kernel(h_ref, mix_ref, mem_ref, w_ref, gn_ref, h2_ref, z_ref):
    h2, z = _outproj_math(h_ref, mix_ref, mem_ref, w_ref, gn_ref)
    h2_ref[...] = h2
    z_ref[...] = z.astype(BF16)


def _outproj_router_kernel(h_ref, mix_ref, mem_ref, w_ref, gn_ref, wr_ref,
                           h2_ref, zp_ref, route_ref):
    h2, z = _outproj_math(h_ref, mix_ref, mem_ref, w_ref, gn_ref)
    h2_ref[...] = h2

    half = D_MODEL // 2
    lo_bits = pltpu.bitcast(z[:, :half].astype(BF16).astype(F32), jnp.uint32) >> 16
    hi_bits = pltpu.bitcast(z[:, half:].astype(BF16).astype(F32), jnp.uint32) & jnp.uint32(0xFFFF0000)
    packed = lo_bits | hi_bits
    for j in range(half // LANES):
        zp_ref[pl.ds(j, packed.shape[0], stride=PACKED_SLABS), :] = packed[:, j * LANES:(j + 1) * LANES]

    z_hi, z_lo = _split_bf16(z, 2)
    wr_hi, wr_lo = _split_bf16(wr_ref[...], 2)
    hi_terms = jnp.dot(z_hi, jnp.concatenate([wr_hi, wr_lo], axis=1),
                       preferred_element_type=F32)
    logits = (hi_terms[:, :LANES] + hi_terms[:, LANES:]
              + jnp.dot(z_lo, wr_hi, preferred_element_type=F32))
    lane = lax.broadcasted_iota(jnp.int32, logits.shape, 1)
    lane_f = lane.astype(F32)
    logits = jnp.where(lane < N_EXPERTS, logits, -jnp.inf)
    l1 = jnp.max(logits, axis=-1, keepdims=True)
    e1 = jnp.min(jnp.where(logits == l1, lane_f, float(LANES)), axis=-1, keepdims=True)
    rest = jnp.where(lane_f == e1, -jnp.inf, logits)
    l2 = jnp.max(rest, axis=-1, keepdims=True)
    e2 = jnp.min(jnp.where(rest == l2, lane_f, float(LANES)), axis=-1, keepdims=True)
    g2 = 1.0 / (1.0 + jnp.exp(l1 - l2))
    g1 = 1.0 - g2
    route = jnp.where(lane == 0, g1, 0.0) + jnp.where(lane == 1, g2, 0.0)
    route = route + jnp.where(lane == 2, e1, 0.0)
    route_ref[...] = route + jnp.where(lane == 3, e2, 0.0)


def _outproj(h, mix, mem, w, gn, router=None):
    t = h.shape[0]
    tm = min(ROW_TILE, t)
    assert t % tm == 0
    row = lambda n: pl.BlockSpec((tm, n), lambda i: (i, 0))
    in_specs = [row(D_MODEL), row(MIX_DIM), row(MEM_DIM), _resident(w.shape), _resident(gn.shape)]
    if router is None:
        return pl.pallas_call(
            _outproj_dense_kernel, grid=(t // tm,), in_specs=in_specs,
            out_specs=[row(D_MODEL), row(D_MODEL)],
            out_shape=[jax.ShapeDtypeStruct((t, D_MODEL), F32),
                       jax.ShapeDtypeStruct((t, D_MODEL), BF16)],
            compiler_params=_params("parallel"), name="outproj_dense",
        )(h, mix, mem, w, gn)
    return pl.pallas_call(
        _outproj_router_kernel, grid=(t // tm,),
        in_specs=in_specs + [_resident(router.shape)],
        out_specs=[row(D_MODEL), pl.BlockSpec((tm * PACKED_SLABS, LANES), lambda i: (i, 0)),
                   row(LANES)],
        out_shape=[jax.ShapeDtypeStruct((t, D_MODEL), F32),
                   jax.ShapeDtypeStruct((t * PACKED_SLABS, LANES), jnp.uint32),
                   jax.ShapeDtypeStruct((t, LANES), F32)],
        compiler_params=_params("parallel"), name="outproj_router",
    )(h, mix, mem, w, gn, router)


def _ffn_kernel(h_ref, z_ref, wg_ref, wu_ref, wd_ref, o_ref, acc_ref):
    z = z_ref[...]
    d_ff = wg_ref.shape[1]
    acc_ref[...] = h_ref[...]
    for f0 in range(0, d_ff, FFN_CHUNK):
        g = jnp.dot(z, wg_ref[:, f0:f0 + FFN_CHUNK], preferred_element_type=F32)
        u = jnp.dot(z, wu_ref[:, f0:f0 + FFN_CHUNK], preferred_element_type=F32)
        a = (_silu(g) * u).astype(BF16)
        acc_ref[...] += jnp.dot(a, wd_ref[f0:f0 + FFN_CHUNK, :], preferred_element_type=F32)
    o_ref[...] = acc_ref[...]


def _ffn(h2, z, wg, wu, wd):
    t = h2.shape[0]
    tm = min(ROW_TILE, t)
    assert t % tm == 0 and wg.shape[1] % FFN_CHUNK == 0
    row = pl.BlockSpec((tm, D_MODEL), lambda i: (i, 0))
    return pl.pallas_call(
        _ffn_kernel, grid=(t // tm,),
        in_specs=[row, row, _resident(wg.shape), _resident(wu.shape), _resident(wd.shape)],
        out_specs=row,
        out_shape=jax.ShapeDtypeStruct((t, D_MODEL), F32),
        scratch_shapes=[pltpu.VMEM((tm, D_MODEL), F32)],
        compiler_params=_params("parallel"), name="ffn_dense",
    )(h2, z, wg, wu, wd)


def _token_copy(src, src_row, dst, dst_row, slabs, sem):
    return pltpu.make_async_copy(src.at[pl.ds(src_row, slabs), :], dst.at[pl.ds(dst_row, slabs), :],
                                 sem)


def _block_copy(src, dst, slabs, sem):
    n = EXPERT_ROWS * slabs
    return pltpu.make_async_copy(src.at[pl.ds(0, n), :], dst.at[pl.ds(0, n), :], sem)


def _experts_kernel(n_f, be_ref, nact_ref, tok_cur_ref, tok_next_ref, dst_prev_ref, dst_cur_ref,
                    zp_hbm, wg_ref, wu_ref, wd_ref, y_hbm, xp_ref, x_ref, acc_ref, stage_ref,
                    sem_in, sem_out):
    del be_ref
    i = pl.program_id(0)
    f = pl.program_id(1)
    n_active = nact_ref[0]
    active = i < n_active
    has_next = i + 1 < n_active
    has_prev = i >= 1
    first_step = f == 0
    last_step = f == n_f - 1
    half = D_MODEL // 2

    def gather(idx_ref, r, priority=0):
        src = pl.multiple_of(idx_ref[0, 0, r], PACKED_SLABS)
        _token_copy(zp_hbm, src, xp_ref, r * PACKED_SLABS, PACKED_SLABS, sem_in).start(
            priority=priority)

    def scatter(idx_ref, r, priority=0):
        dst = pl.multiple_of(idx_ref[0, 0, r], OUT_SLABS)
        _token_copy(stage_ref, r * OUT_SLABS, y_hbm, dst, OUT_SLABS, sem_out).start(
            priority=priority)

    def gather_next(r, parity):
        del parity
        gather(tok_next_ref, r)

    def scatter_prev(r, parity):
        scatter(dst_prev_ref, r, priority=parity)

    def rolled(fn):
        def body(r, _):
            fn(r)
            return 0
        lax.fori_loop(0, EXPERT_ROWS, body, 0)

    def compute(phase, inline_dma=None):
        def sub_block(s, _):
            r0 = pl.multiple_of(s * EXPERT_SUB, EXPERT_SUB)
            rows = pl.ds(r0, EXPERT_SUB)
            x = x_ref[rows, :]
            g = jnp.dot(x, wg_ref[0, 0], preferred_element_type=F32)
            u = jnp.dot(x, wu_ref[0, 0], preferred_element_type=F32)
            a = (_silu(g) * u).astype(BF16)
            c = jnp.dot(a, wd_ref[0, 0], preferred_element_type=F32)
            if phase == "first":
                acc_ref[rows, :] = c
            elif phase == "mid":
                acc_ref[rows, :] += c
            else:
                out = acc_ref[rows, :] + c
                for j in range(OUT_SLABS):
                    stage_ref[pl.ds(r0 * OUT_SLABS + j, EXPERT_SUB, stride=OUT_SLABS), :] = (
                        out[:, j * LANES:(j + 1) * LANES])
            if inline_dma is not None:
                for k in range(EXPERT_SUB):
                    inline_dma(r0 + k, k % 2)
            return 0
        lax.fori_loop(0, EXPERT_ROWS // EXPERT_SUB, sub_block, 0)

    @pl.when((i == 0) & first_step)
    def _():
        stage_ref[...] = jnp.zeros(stage_ref.shape, F32)
        n_sink = EXPERT_ROWS * OUT_SLABS
        sink = pltpu.make_async_copy(
            stage_ref, y_hbm.at[pl.ds(y_hbm.shape[0] - n_sink, n_sink), :], sem_out)
        sink.start()
        sink.wait()
        rolled(functools.partial(gather, tok_cur_ref))

    @pl.when(active & first_step)
    def _():
        _block_copy(zp_hbm, xp_ref, PACKED_SLABS, sem_in).wait()
        for j in range(PACKED_SLABS):
            xp = xp_ref[pl.ds(j, EXPERT_ROWS, stride=PACKED_SLABS), :]
            lanes = slice(j * LANES, (j + 1) * LANES)
            x_ref[:, lanes] = pltpu.bitcast(xp << 16, F32).astype(BF16)
            x_ref[:, half + j * LANES:half + (j + 1) * LANES] = pltpu.bitcast(
                xp & jnp.uint32(0xFFFF0000), F32).astype(BF16)

    @pl.when(active & first_step & has_prev)
    def _():
        compute("first", scatter_prev)

    @pl.when(active & first_step & jnp.logical_not(has_prev))
    def _():
        compute("first")

    if n_f > 2:
        @pl.when(active & jnp.logical_not(first_step) & jnp.logical_not(last_step))
        def _():
            compute("mid")

    @pl.when(active & last_step & has_prev)
    def _():
        _block_copy(stage_ref, y_hbm, OUT_SLABS, sem_out).wait()

    @pl.when(active & last_step & has_next)
    def _():
        compute("last", gather_next)

    @pl.when(active & last_step & jnp.logical_not(has_next))
    def _():
        compute("last")
        rolled(functools.partial(scatter, dst_cur_ref))
        _block_copy(stage_ref, y_hbm, OUT_SLABS, sem_out).wait()


def _experts(zp, plan, layer, wg, wu, wd):
    t = zp.shape[0] // PACKED_SLABS
    blk_expert, n_active, tok_buf, dst_buf = plan
    nb = blk_expert.shape[0]
    d_ff = wg.shape[3]
    tf = EXPERT_FF_TILE
    n_f = d_ff // tf
    assert d_ff % tf == 0 and n_f >= 2
    last = nb - 1

    def idx(shift):
        return pl.BlockSpec((1, 1, EXPERT_ROWS),
                            lambda i, f, *_: (jnp.clip(i + shift, 0, last), 0, 0),
                            memory_space=pltpu.SMEM)

    grid_spec = pltpu.PrefetchScalarGridSpec(
        num_scalar_prefetch=2,
        grid=(nb, n_f),
        in_specs=[idx(0), idx(1), idx(-1), idx(0), pl.BlockSpec(memory_space=pl.ANY),
                  pl.BlockSpec((1, 1, D_MODEL, tf), lambda i, f, be, *_: (layer, be[i], 0, f)),
                  pl.BlockSpec((1, 1, D_MODEL, tf), lambda i, f, be, *_: (layer, be[i], 0, f)),
                  pl.BlockSpec((1, 1, tf, D_MODEL), lambda i, f, be, *_: (layer, be[i], f, 0))],
        out_specs=pl.BlockSpec(memory_space=pl.ANY),
        scratch_shapes=[pltpu.VMEM((EXPERT_ROWS * PACKED_SLABS, LANES), jnp.uint32),
                        pltpu.VMEM((EXPERT_ROWS, D_MODEL), BF16),
                        pltpu.VMEM((EXPERT_ROWS, D_MODEL), F32),
                        pltpu.VMEM((EXPERT_ROWS * OUT_SLABS, LANES), F32),
                        pltpu.SemaphoreType.DMA(()), pltpu.SemaphoreType.DMA(())],
    )
    return pl.pallas_call(
        functools.partial(_experts_kernel, n_f), grid_spec=grid_spec,
        out_shape=jax.ShapeDtypeStruct(((TOP_K * t + EXPERT_ROWS) * OUT_SLABS, LANES), F32),
        compiler_params=_params("arbitrary", "arbitrary"), name="moe_experts",
    )(blk_expert, n_active, tok_buf, tok_buf, dst_buf, dst_buf, zp, wg, wu, wd)


def _route_plan(route, t):
    a = t * TOP_K
    nb = a // EXPERT_ROWS + N_EXPERTS
    rows = nb * EXPERT_ROWS
    flat_e = route[:, 2:4].astype(jnp.int32).reshape(a)
    experts = jnp.arange(N_EXPERTS, dtype=jnp.int32)
    sizes = jnp.sum((flat_e[:, None] == experts[None, :]).astype(jnp.int32), axis=0)
    start = jnp.cumsum(sizes) - sizes
    padded = ((sizes + EXPERT_ROWS - 1) // EXPERT_ROWS) * EXPERT_ROWS
    pad_end = jnp.cumsum(padded)
    pad_start = pad_end - padded
    sorted_assign = jnp.sort(flat_e * a + jnp.arange(a, dtype=jnp.int32)) % a
    blk_start = jnp.arange(nb, dtype=jnp.int32) * EXPERT_ROWS
    blk_expert = jnp.minimum(
        jnp.sum((pad_end[None, :] <= blk_start[:, None]).astype(jnp.int32), axis=1),
        N_EXPERTS - 1)
    row = jnp.arange(rows, dtype=jnp.int32)
    row_e = jnp.repeat(blk_expert, EXPERT_ROWS)
    rank = row - pad_start[row_e]
    real = (rank >= 0) & (rank < sizes[row_e])
    assign = sorted_assign[jnp.clip(start[row_e] + rank, 0, a - 1)]
    tok = assign // TOP_K
    tok_buf = jnp.where(real, tok, 0)
    dst_buf = jnp.where(real, tok + (assign % TOP_K) * t, TOP_K * t + row % EXPERT_ROWS)
    n_active = (pad_end[-1] // EXPERT_ROWS).reshape(1).astype(jnp.int32)
    return (blk_expert, n_active, (tok_buf * PACKED_SLABS).reshape(nb, 1, EXPERT_ROWS),
            (dst_buf * OUT_SLABS).reshape(nb, 1, EXPERT_ROWS))


def _combine_kernel(h_ref, y0_ref, y1_ref, route_ref, o_ref):
    o_ref[...] = _moe_combined(h_ref, y0_ref, y1_ref, route_ref)


def _combine(h2, y, route):
    t = h2.shape[0]
    tm = min(ROW_TILE, t)
    nt = t // tm
    row = pl.BlockSpec((tm, D_MODEL), lambda i: (i, 0))
    return pl.pallas_call(
        _combine_kernel, grid=(nt,),
        in_specs=[row, pl.BlockSpec((tm * OUT_SLABS, LANES), lambda i: (i, 0)),
                  pl.BlockSpec((tm * OUT_SLABS, LANES), lambda i: (i + nt, 0)),
                  pl.BlockSpec((tm, LANES), lambda i: (i, 0))],
        out_specs=row,
        out_shape=jax.ShapeDtypeStruct((t, D_MODEL), F32),
        compiler_params=_params("parallel"), name="moe_combine",
    )(h2, y, y, route)


def _row(v):
    return v.reshape(1, -1).astype(F32)


def _tile_heads(g, n, scale=1.0):
    return _row(jnp.tile(g.astype(F32) * scale, n))


def kernel(x, mem, norm_mix, norm_mem, norm_ffn, w_mem_kv, g_mq, g_mk, fox_w_in, fox_b_f, fox_g_q, fox_g_k, fox_w_out, conv_w_in, conv_b_in, conv_dw, conv_dw_b, conv_ln_g, conv_ln_b, conv_w_out, ffn_w_gate, ffn_w_up, ffn_w_down, moe_router, moe_w_gate, moe_w_up, moe_w_down):
    b, s, d = x.shape
    mem_len = mem.shape[1]
    t = b * s
    assert d == D_MODEL
    scale = HEAD_DIM ** -0.5

    group = jnp.arange(MXU_WIDTH, dtype=jnp.int32) // HEAD_DIM
    ones_bd = (group[:, None] == group[None, :]).astype(BF16)

    moe_wg, moe_wu, moe_wd = (w.astype(BF16) for w in (moe_w_gate, moe_w_up, moe_w_down))
    h = x.reshape(t, d)
    pending = None
    mem2d = mem.reshape(b * mem_len, d)
    for i in range(DEPTH):
        j = i // 2
        gmq = _tile_heads(g_mq[i], MEM_DIM // HEAD_DIM, scale)
        mk, mv = _mem_kv(mem2d, mem_len, _row(norm_mem[i]), w_mem_kv[i].astype(BF16),
                         ones_bd, _tile_heads(g_mk[i], MEM_DIM // HEAD_DIM))
        if i % 2 == 0:
            w = fox_w_in[j]
            m3 = 3 * MIX_DIM
            w_cat = jnp.concatenate(
                [w[:, :m3], w[:, m3 + FOX_HEADS:], w[:, m3:m3 + FOX_HEADS],
                 jnp.zeros((d, F_LANES - FOX_HEADS), w.dtype)], axis=1).astype(BF16)
            bf = jnp.pad(fox_b_f[j].astype(F32), (0, F_LANES - FOX_HEADS)).reshape(1, F_LANES)
            h, q, k, v, mq, kbias = _fox_inproj(
                h, s, _row(norm_mix[i]), w_cat, ones_bd,
                _tile_heads(fox_g_q[j], FOX_HEADS, scale * LOG2E),
                _tile_heads(fox_g_k[j], FOX_HEADS), gmq, bf, moe=pending)
            pending = None
            seq = lambda a: a.reshape(b, s, MIX_DIM)
            mix = _fox_attn(seq(q), seq(k), seq(kbias), seq(v)).reshape(t, MIX_DIM)
            w_out = fox_w_out[j]
        else:
            u, mq = _conv_inproj(h, s, _row(norm_mix[i]), conv_w_in[j].astype(BF16),
                                 _row(conv_b_in[j]), ones_bd, gmq)
            dw8 = jnp.broadcast_to(conv_dw[j].astype(F32)[:, None, :],
                                   (CONV_WIDTH, SUBLANES, MIX_DIM))
            mix = _conv_module(u.reshape(b, s, MIX_DIM), dw8,
                               _row(conv_dw_b[j]), _row(conv_ln_g[j]),
                               _row(conv_ln_b[j])).reshape(t, MIX_DIM)
            w_out = conv_w_out[j]
        mem_out = _mem_attn(mq, mk, mv, s, mem_len)
        if i % 2 == 0:
            h2, z = _outproj(h, mix, mem_out, w_out.astype(BF16), _row(norm_ffn[i]))
            h = _ffn(h2, z, ffn_w_gate[j].astype(BF16), ffn_w_up[j].astype(BF16),
                     ffn_w_down[j].astype(BF16))
        else:
            wr = jnp.pad(moe_router[j].astype(F32), ((0, 0), (0, LANES - N_EXPERTS)))
            h2, zp, route = _outproj(h, mix, mem_out, w_out.astype(BF16), _row(norm_ffn[i]),
                                     router=wr)
            plan = _route_plan(route, t)
            y = _experts(zp, plan, j, moe_wg, moe_wu, moe_wd)
            if i + 1 < DEPTH and (i + 1) % 2 == 0:
                h, pending = h2, (y, route)
            else:
                h = _combine(h2, y, route)
    return h.reshape(b, s, d)
```

```python
import functools

import jax
import jax.numpy as jnp
from jax import lax
from jax.experimental import pallas as pl
from jax.experimental.pallas import tpu as pltpu

F32 = jnp.float32
BF16 = jnp.bfloat16

D_MODEL = 1024
HEAD_DIM = 64
MEM_DIM = 256
MIX_DIM = D_MODEL - MEM_DIM
FOX_HEADS = MIX_DIM // HEAD_DIM
CONV_WIDTH = 31
N_EXPERTS = 8
TOP_K = 2
DEPTH = 4
EPS = 1e-6
NEG_INF = -1e30
LOG2E = 1.4426950408889634
BIAS_PARTS = 3
PART_STRIDE = 16

LANES = 128
SUBLANES = 8
MXU_WIDTH = 256
VMEM_LIMIT_BYTES = 56 * 1024 * 1024

ROW_TILE = 512
ATTN_TILE = 512
CONV_TILE = 256
CONV_HALO = 32
CONV_ROW_CHUNK = 64
FFN_CHUNK = 256
EXPERT_ROWS = 1024
EXPERT_SUB = 256
EXPERT_FF_TILE = 1792
F_LANES = LANES
PACKED_SLABS = D_MODEL // 2 // LANES
OUT_SLABS = D_MODEL // LANES


def _params(*sem):
    return pltpu.CompilerParams(dimension_semantics=sem, vmem_limit_bytes=VMEM_LIMIT_BYTES)


def _resident(shape):
    nd = len(shape)
    return pl.BlockSpec(shape, lambda *_: (0,) * nd, pipeline_mode=pl.Buffered(1))


def _rms(x, g):
    ms = jnp.mean(x * x, axis=-1, keepdims=True)
    return x * lax.rsqrt(ms + EPS) * g


def _head_rms(x, ones_bd, g):
    w = ones_bd.shape[0]
    sq = (x * x).astype(BF16)
    ss = jnp.concatenate([jnp.dot(sq[:, c:c + w], ones_bd, preferred_element_type=F32)
                          for c in range(0, x.shape[1], w)], axis=1)
    return x * lax.rsqrt(ss * (1.0 / HEAD_DIM) + EPS) * g


def _silu(x):
    return x * (1.0 / (1.0 + jnp.exp(-x)))


def _log_sigmoid(x):
    return jnp.minimum(x, 0.0) - jnp.log(1.0 + jnp.exp(-jnp.abs(x)))


def _row_prefix_sum(x):
    n = x.shape[0]
    row = lax.broadcasted_iota(jnp.int32, x.shape, 0)
    s = 1
    while s < n:
        x = x + jnp.where(row >= s, pltpu.roll(x, s, 0), 0.0)
        s *= 2
    return x


def _split_bf16(x, parts):
    out = []
    for _ in range(parts - 1):
        hi = x.astype(BF16)
        out.append(hi)
        x = x - hi.astype(F32)
    out.append(x.astype(BF16))
    return out


def _moe_combined(h2_ref, y0_ref, y1_ref, route_ref):
    g = route_ref[...]
    cols = []
    for j in range(OUT_SLABS):
        rows = pl.ds(j, h2_ref.shape[0], stride=OUT_SLABS)
        cols.append(h2_ref[:, j * LANES:(j + 1) * LANES]
                    + (g[:, 0:1] * y0_ref[rows, :] + g[:, 1:2] * y1_ref[rows, :]))
    return jnp.concatenate(cols, axis=1)


def _fox_inproj_moe_kernel(tiles_per_seq, h2_ref, y0_ref, y1_ref, route_ref, *rest):
    ins, h_ref, tail = rest[:8], rest[8], rest[9:]
    h = _moe_combined(h2_ref, y0_ref, y1_ref, route_ref)
    h_ref[...] = h
    _fox_inproj_body(tiles_per_seq, h, *ins, *tail)


def _fox_inproj_kernel(tiles_per_seq, h_ref, *rest):
    _fox_inproj_body(tiles_per_seq, h_ref[...], *rest)


def _fox_inproj_body(tiles_per_seq, h, gn_ref, w_ref, bd_ref, gq_ref, gk_ref, gmq_ref,
                     bf_ref, place_ref, q_ref, k_ref, v_ref, mq_ref, kb_ref, carry_ref):
    xb = _rms(h, gn_ref[...]).astype(BF16)
    m = MIX_DIM
    q = jnp.dot(xb, w_ref[:, 0:m], preferred_element_type=F32)
    q_ref[...] = _head_rms(q, bd_ref[...], gq_ref[...]).astype(BF16)
    k = jnp.dot(xb, w_ref[:, m:2 * m], preferred_element_type=F32)
    k_ref[...] = _head_rms(k, bd_ref[...], gk_ref[...]).astype(BF16)
    v_ref[...] = jnp.dot(xb, w_ref[:, 2 * m:3 * m], preferred_element_type=F32).astype(BF16)
    mq = jnp.dot(xb, w_ref[:, 3 * m:3 * m + MEM_DIM], preferred_element_type=F32)
    mq_ref[...] = _head_rms(mq, bd_ref[...], gmq_ref[...]).astype(BF16)

    f = jnp.dot(xb, w_ref[:, 3 * m + MEM_DIM:], preferred_element_type=F32) + bf_ref[...]
    lane = lax.broadcasted_iota(jnp.int32, f.shape, 1)
    lf = jnp.where(lane < FOX_HEADS, _log_sigmoid(f), 0.0)

    @pl.when(pl.program_id(0) % tiles_per_seq == 0)
    def _():
        carry_ref[...] = jnp.zeros_like(carry_ref)

    c = _row_prefix_sum(lf) + carry_ref[0:1, :]
    carry_ref[0:1, :] = c[-1:, :]
    parts = _split_bf16(c * (-LOG2E), BIAS_PARTS)
    parked = parts[0].astype(F32)
    for p in range(1, BIAS_PARTS):
        parked = parked + pltpu.roll(parts[p].astype(F32), PART_STRIDE * p, 1)
    kb_ref[...] = jnp.dot(parked.astype(BF16), place_ref[...],
                          preferred_element_type=F32).astype(BF16)


def _bias_placement():
    row = jnp.arange(F_LANES, dtype=jnp.int32)[:, None]
    p, h = row // PART_STRIDE, row % PART_STRIDE
    col = jnp.arange(MIX_DIM, dtype=jnp.int32)[None, :]
    return ((p < BIAS_PARTS) & (h < FOX_HEADS)
            & (col == (h // 2) * LANES + BIAS_PARTS * (h % 2) + p)).astype(BF16)


def _fox_inproj(h, seq_len, gn, w, bd, gq, gk, gmq, bf, moe=None):
    t = h.shape[0]
    tm = min(ROW_TILE, seq_len)
    assert seq_len % tm == 0
    nt = t // tm
    row = lambda n: pl.BlockSpec((tm, n), lambda i: (i, 0))
    place = _bias_placement()
    params = (gn, w, bd, gq, gk, gmq, bf, place)
    in_specs = [row(D_MODEL)] + [_resident(p.shape) for p in params]
    out_specs = [row(MIX_DIM), row(MIX_DIM), row(MIX_DIM), row(MEM_DIM), row(MIX_DIM)]
    out_shape = ([jax.ShapeDtypeStruct((t, MIX_DIM), BF16)] * 3
                 + [jax.ShapeDtypeStruct((t, MEM_DIM), BF16),
                    jax.ShapeDtypeStruct((t, MIX_DIM), BF16)])
    common = dict(grid=(nt,), scratch_shapes=[pltpu.VMEM((SUBLANES, F_LANES), F32)],
                  compiler_params=_params("arbitrary"))
    if moe is None:
        outs = pl.pallas_call(
            functools.partial(_fox_inproj_kernel, seq_len // tm), in_specs=in_specs,
            out_specs=out_specs, out_shape=out_shape, name="fox_inproj", **common,
        )(h, *params)
        return (h, *outs)
    y, route = moe
    y_spec = lambda k: pl.BlockSpec((tm * OUT_SLABS, LANES), lambda i: (i + k * nt, 0))
    return pl.pallas_call(
        functools.partial(_fox_inproj_moe_kernel, seq_len // tm),
        in_specs=[row(D_MODEL), y_spec(0), y_spec(1), row(LANES)] + in_specs[1:],
        out_specs=[row(D_MODEL)] + out_specs,
        out_shape=[jax.ShapeDtypeStruct((t, D_MODEL), F32)] + out_shape,
        name="fox_inproj_moe", **common,
    )(h, y, y, route, *params)


def _conv_inproj_kernel(h_ref, gn_ref, w_ref, b_ref, bd_ref, gmq_ref, u_ref, mq_ref):
    xb = _rms(h_ref[...], gn_ref[...]).astype(BF16)
    m = MIX_DIM
    a = jnp.dot(xb, w_ref[:, 0:m], preferred_element_type=F32) + b_ref[:, 0:m]
    g = jnp.dot(xb, w_ref[:, m:2 * m], preferred_element_type=F32) + b_ref[:, m:2 * m]
    u_ref[...] = a * (1.0 / (1.0 + jnp.exp(-g)))
    mq = jnp.dot(xb, w_ref[:, 2 * m:], preferred_element_type=F32)
    mq_ref[...] = _head_rms(mq, bd_ref[...], gmq_ref[...]).astype(BF16)


def _conv_inproj(h, seq_len, gn, w, b, bd, gmq):
    t = h.shape[0]
    tm = min(ROW_TILE, seq_len)
    assert t % tm == 0
    row = lambda n: pl.BlockSpec((tm, n), lambda i: (i, 0))
    return pl.pallas_call(
        _conv_inproj_kernel,
        grid=(t // tm,),
        in_specs=[row(D_MODEL), _resident(gn.shape), _resident(w.shape), _resident(b.shape),
                  _resident(bd.shape), _resident(gmq.shape)],
        out_specs=[row(MIX_DIM), row(MEM_DIM)],
        out_shape=[jax.ShapeDtypeStruct((t, MIX_DIM), F32),
                   jax.ShapeDtypeStruct((t, MEM_DIM), BF16)],
        compiler_params=_params("parallel"),
        name="conv_inproj",
    )(h, gn, w, b, bd, gmq)


def _mem_kv_kernel(m_ref, gn_ref, w_ref, bd_ref, gk_ref, mk_ref, mv_ref):
    xb = _rms(m_ref[...], gn_ref[...]).astype(BF16)
    mk = jnp.dot(xb, w_ref[:, 0:MEM_DIM], preferred_element_type=F32)
    mk_ref[...] = _head_rms(mk, bd_ref[...], gk_ref[...]).astype(BF16)
    mv_ref[...] = jnp.dot(xb, w_ref[:, MEM_DIM:], preferred_element_type=F32).astype(BF16)


def _mem_kv(mem2d, mem_len, gn, w, bd, gk):
    rows = mem2d.shape[0]
    row = lambda n: pl.BlockSpec((mem_len, n), lambda i: (i, 0))
    return pl.pallas_call(
        _mem_kv_kernel,
        grid=(rows // mem_len,),
        in_specs=[row(D_MODEL), _resident(gn.shape), _resident(w.shape), _resident(bd.shape),
                  _resident(gk.shape)],
        out_specs=[row(MEM_DIM), row(MEM_DIM)],
        out_shape=[jax.ShapeDtypeStruct((rows, MEM_DIM), BF16)] * 2,
        compiler_params=_params("parallel"),
        name="mem_kv",
    )(mem2d, gn, w, bd, gk)


def _head_lane_mask(shape, head_in_pair):
    lane = lax.broadcasted_iota(jnp.int32, shape, len(shape) - 1)
    return (lane // HEAD_DIM) == head_in_pair


def _fox_attn_kernel(tq, q_ref, k_ref, kb_ref, v_ref, o_ref, m_ref, acc_ref):
    i = pl.program_id(2)
    tk = tq
    q2 = q_ref[0]
    zero = jnp.zeros_like(q2)
    lane = lax.broadcasted_iota(jnp.int32, q2.shape, 1)
    q_stack = jnp.concatenate(
        [jnp.concatenate(
            [jnp.where(_head_lane_mask(q2.shape, hh), q2, zero),
             ((lane >= BIAS_PARTS * hh) & (lane < BIAS_PARTS * (hh + 1))).astype(BF16)], axis=1)
         for hh in range(2)], axis=0)
    m_ref[...] = jnp.full(m_ref.shape, NEG_INF, F32)
    acc_ref[...] = jnp.zeros(acc_ref.shape, F32)
    on_or_below_diag = (lax.broadcasted_iota(jnp.int32, (tq, tk), 1)
                        <= lax.broadcasted_iota(jnp.int32, (tq, tk), 0))

    def key_rows(j):
        return pl.ds(pl.multiple_of(j * tk, tk), tk)

    def block(j, masked):
        rows = key_rows(j)
        kb = jnp.concatenate([k_ref[0, rows, :], kb_ref[0, rows, :]], axis=1)
        s2 = lax.dot_general(q_stack, kb, (((1,), (1,)), ((), ())), preferred_element_type=F32)
        vb = v_ref[0, rows, :]
        for hh in range(2):
            s = s2[hh * tq:(hh + 1) * tq, :]
            if masked:
                s = jnp.where(on_or_below_diag, s, NEG_INF)
            m_old = m_ref[hh]
            m_new = jnp.maximum(m_old, jnp.max(s, axis=-1, keepdims=True))
            alpha = jnp.exp2(m_old - m_new)
            p = jnp.concatenate([jnp.exp2(s[:, t0:t0 + LANES] - m_new)
                                 for t0 in range(0, tk, LANES)], axis=1).astype(BF16)
            v_h = jnp.where(_head_lane_mask(vb.shape, hh), vb, jnp.ones_like(vb))
            acc_ref[hh] = alpha * acc_ref[hh] + jnp.dot(p, v_h, preferred_element_type=F32)
            m_ref[hh] = m_new

    def pair(j2, _):
        block(2 * j2, masked=False)
        block(2 * j2 + 1, masked=False)
        return 0

    lax.fori_loop(0, i // 2, pair, 0)

    @pl.when(i % 2 == 1)
    def _():
        block(i - 1, masked=False)
        block(i, masked=True)

    @pl.when(i % 2 == 0)
    def _():
        block(i, masked=True)

    acc0, acc1 = acc_ref[0], acc_ref[1]
    out0 = acc0 * (1.0 / acc0[:, HEAD_DIM:HEAD_DIM + 1])
    out1 = acc1 * (1.0 / acc1[:, 0:1])
    o_ref[0] = jnp.where(_head_lane_mask(out0.shape, 0), out0, out1).astype(o_ref.dtype)


def _fox_attn(q, k, kbias, v):
    b, s, _ = q.shape
    tq = min(ATTN_TILE, s)
    assert s % tq == 0
    seq = pl.BlockSpec((1, s, LANES), lambda bi, hp, i: (bi, 0, hp))
    tile = pl.BlockSpec((1, tq, LANES), lambda bi, hp, i: (bi, i, hp))
    return pl.pallas_call(
        functools.partial(_fox_attn_kernel, tq),
        grid=(b, FOX_HEADS // 2, s // tq),
        in_specs=[tile, seq, seq, seq],
        out_specs=tile,
        out_shape=jax.ShapeDtypeStruct((b, s, MIX_DIM), BF16),
        scratch_shapes=[pltpu.VMEM((2, tq, LANES), F32)] * 2,
        compiler_params=_params("parallel", "parallel", "arbitrary"),
        name="fox_attn",
    )(q, k, kbias, v)


def _mem_attn_kernel(q_ref, k_ref, v_ref, o_ref):
    for pair in range(MEM_DIM // LANES):
        lanes = slice(pair * LANES, (pair + 1) * LANES)
        q2 = q_ref[:, lanes]
        k2 = k_ref[:, lanes]
        v2 = v_ref[:, lanes]
        outs = []
        for hh in range(2):
            qh = jnp.where(_head_lane_mask(q2.shape, hh), q2, jnp.zeros_like(q2))
            s = lax.dot_general(qh, k2, (((1,), (1,)), ((), ())), preferred_element_type=F32)
            p = jnp.exp(s - jnp.max(s, axis=-1, keepdims=True))
            l = jnp.sum(p, axis=-1, keepdims=True)
            outs.append(jnp.dot(p.astype(BF16), v2, preferred_element_type=F32) * (1.0 / l))
        o_ref[:, lanes] = jnp.where(_head_lane_mask(outs[0].shape, 0), outs[0],
                                    outs[1]).astype(o_ref.dtype)


def _mem_attn(mq, mk, mv, seq_len, mem_len):
    t = mq.shape[0]
    tm = min(ROW_TILE, seq_len)
    per_seq = seq_len // tm
    kv = pl.BlockSpec((mem_len, MEM_DIM), lambda i: (i // per_seq, 0))
    return pl.pallas_call(
        _mem_attn_kernel,
        grid=(t // tm,),
        in_specs=[pl.BlockSpec((tm, MEM_DIM), lambda i: (i, 0)), kv, kv],
        out_specs=pl.BlockSpec((tm, MEM_DIM), lambda i: (i, 0)),
        out_shape=jax.ShapeDtypeStruct((t, MEM_DIM), BF16),
        compiler_params=_params("parallel"),
        name="mem_attn",
    )(mq, mk, mv)


def _conv_kernel(ts, u_ref, dw_ref, dwb_ref, lng_ref, lnb_ref, o_ref, ext_ref, sh_ref, acc_ref):
    @pl.when(pl.program_id(1) == 0)
    def _():
        ext_ref[0:CONV_HALO, :] = jnp.zeros((CONV_HALO, MIX_DIM), F32)

    @pl.when(pl.program_id(1) > 0)
    def _():
        ext_ref[0:CONV_HALO, :] = ext_ref[ts:ts + CONV_HALO, :]

    ext_ref[CONV_HALO:CONV_HALO + ts, :] = u_ref[0]

    base = CONV_HALO - (CONV_WIDTH - 1)
    n_shift = SUBLANES
    for b in range(n_shift):
        n_rows = ts + ((CONV_WIDTH - 1 - b) // n_shift) * n_shift
        sh_ref[b, 0:n_rows, :] = ext_ref[base + b:base + b + n_rows, :]
    rc = min(CONV_ROW_CHUNK, ts)
    for r0 in range(0, ts, rc):
        for l0 in range(0, MIX_DIM, LANES):
            groups = (rc // SUBLANES, SUBLANES, LANES)
            acc = jnp.zeros(groups, F32) + dwb_ref[:, l0:l0 + LANES]
            for j in range(CONV_WIDTH):
                a, b = divmod(j, n_shift)
                x = sh_ref[b, r0 + a * n_shift:r0 + a * n_shift + rc, l0:l0 + LANES]
                w = jnp.broadcast_to(dw_ref[j, :, l0:l0 + LANES][None], groups)
                acc = acc + w * x.reshape(groups)
            acc_ref[r0:r0 + rc, l0:l0 + LANES] = acc.reshape(rc, LANES)

    y = acc_ref[...]
    mu = jnp.mean(y, axis=-1, keepdims=True)
    yc = y - mu
    var = jnp.mean(yc * yc, axis=-1, keepdims=True)
    z = yc * lax.rsqrt(var + EPS) * lng_ref[...] + lnb_ref[...]
    o_ref[0] = _silu(z).astype(o_ref.dtype)


def _conv_module(u, dw, dwb, lng, lnb):
    b, s, _ = u.shape
    ts = min(CONV_TILE, s)
    assert s % ts == 0 and ts >= CONV_HALO
    blk = pl.BlockSpec((1, ts, MIX_DIM), lambda bi, i: (bi, i, 0))
    return pl.pallas_call(
        functools.partial(_conv_kernel, ts),
        grid=(b, s // ts),
        in_specs=[blk, _resident(dw.shape), _resident(dwb.shape), _resident(lng.shape),
                  _resident(lnb.shape)],
        out_specs=blk,
        out_shape=jax.ShapeDtypeStruct((b, s, MIX_DIM), BF16),
        scratch_shapes=[pltpu.VMEM((ts + CONV_HALO, MIX_DIM), F32),
                        pltpu.VMEM((SUBLANES, ts + CONV_HALO - SUBLANES, MIX_DIM), F32),
                        pltpu.VMEM((ts, MIX_DIM), F32)],
        compiler_params=_params("parallel", "arbitrary"),
        name="conv_module",
    )(u, dw, dwb, lng, lnb)


def _outproj_math(h_ref, mix_ref, mem_ref, w_ref, gn_ref):
    h2 = (h_ref[...]
          + jnp.dot(mix_ref[...], w_ref[0:MIX_DIM, :], preferred_element_type=F32)
          + jnp.dot(mem_ref[...], w_ref[MIX_DIM:, :], preferred_element_type=F32))
    return h2, _rms(h2, gn_ref[...])


def _outproj_dense_kernel(h_ref, mix_ref, mem_ref, w_ref, gn_ref, h2_ref, z_ref):
    h2, z = _outproj_math(h_ref, mix_ref, mem_ref, w_ref, gn_ref)
    h2_ref[...] = h2
    z_ref[...] = z.astype(BF16)


def _outproj_router_kernel(h_ref, mix_ref, mem_ref, w_ref, gn_ref, wr_ref,
                           h2_ref, zp_ref, route_ref):
    h2, z = _outproj_math(h_ref, mix_ref, mem_ref, w_ref, gn_ref)
    h2_ref[...] = h2

    half = D_MODEL // 2
    lo_bits = pltpu.bitcast(z[:, :half].astype(BF16).astype(F32), jnp.uint32) >> 16
    hi_bits = pltpu.bitcast(z[:, half:].astype(BF16).astype(F32), jnp.uint32) & jnp.uint32(0xFFFF0000)
    packed = lo_bits | hi_bits
    for j in range(half // LANES):
        zp_ref[pl.ds(j, packed.shape[0], stride=PACKED_SLABS), :] = packed[:, j * LANES:(j + 1) * LANES]

    z_hi, z_lo = _split_bf16(z, 2)
    wr_hi, wr_lo = _split_bf16(wr_ref[...], 2)
    hi_terms = jnp.dot(z_hi, jnp.concatenate([wr_hi, wr_lo], axis=1),
                       preferred_element_type=F32)
    logits = (hi_terms[:, :LANES] + hi_terms[:, LANES:]
              + jnp.dot(z_lo, wr_hi, preferred_element_type=F32))
    lane = lax.broadcasted_iota(jnp.int32, logits.shape, 1)
    lane_f = lane.astype(F32)
    logits = jnp.where(lane < N_EXPERTS, logits, -jnp.inf)
    l1 = jnp.max(logits, axis=-1, keepdims=True)
    e1 = jnp.min(jnp.where(logits == l1, lane_f, float(LANES)), axis=-1, keepdims=True)
    rest = jnp.where(lane_f == e1, -jnp.inf, logits)
    l2 = jnp.max(rest, axis=-1, keepdims=True)
    e2 = jnp.min(jnp.where(rest == l2, lane_f, float(LANES)), axis=-1, keepdims=True)
    g2 = 1.0 / (1.0 + jnp.exp(l1 - l2))
    g1 = 1.0 - g2
    route = jnp.where(lane == 0, g1, 0.0) + jnp.where(lane == 1, g2, 0.0)
    route = route + jnp.where(lane == 2, e1, 0.0)
    route_ref[...] = route + jnp.where(lane == 3, e2, 0.0)


def _outproj(h, mix, mem, w, gn, router=None):
    t = h.shape[0]
    tm = min(ROW_TILE, t)
    assert t % tm == 0
    row = lambda n: pl.BlockSpec((tm, n), lambda i: (i, 0))
    in_specs = [row(D_MODEL), row(MIX_DIM), row(MEM_DIM), _resident(w.shape), _resident(gn.shape)]
    if router is None:
        return pl.pallas_call(
            _outproj_dense_kernel, grid=(t // tm,), in_specs=in_specs,
            out_specs=[row(D_MODEL), row(D_MODEL)],
            out_shape=[jax.ShapeDtypeStruct((t, D_MODEL), F32),
                       jax.ShapeDtypeStruct((t, D_MODEL), BF16)],
            compiler_params=_params("parallel"), name="outproj_dense",
        )(h, mix, mem, w, gn)
    return pl.pallas_call(
        _outproj_router_kernel, grid=(t // tm,),
        in_specs=in_specs + [_resident(router.shape)],
        out_specs=[row(D_MODEL), pl.BlockSpec((tm * PACKED_SLABS, LANES), lambda i: (i, 0)),
                   row(LANES)],
        out_shape=[jax.ShapeDtypeStruct((t, D_MODEL), F32),
                   jax.ShapeDtypeStruct((t * PACKED_SLABS, LANES), jnp.uint32),
                   jax.ShapeDtypeStruct((t, LANES), F32)],
        compiler_params=_params("parallel"), name="outproj_router",
    )(h, mix, mem, w, gn, router)


def _ffn_kernel(h_ref, z_ref, wg_ref, wu_ref, wd_ref, o_ref, acc_ref):
    z = z_ref[...]
    d_ff = wg_ref.shape[1]
    acc_ref[...] = h_ref[...]
    for f0 in range(0, d_ff, FFN_CHUNK):
        g = jnp.dot(z, wg_ref[:, f0:f0 + FFN_CHUNK], preferred_element_type=F32)
        u = jnp.dot(z, wu_ref[:, f0:f0 + FFN_CHUNK], preferred_element_type=F32)
        a = (_silu(g) * u).astype(BF16)
        acc_ref[...] += jnp.dot(a, wd_ref[f0:f0 + FFN_CHUNK, :], preferred_element_type=F32)
    o_ref[...] = acc_ref[...]


def _ffn(h2, z, wg, wu, wd):
    t = h2.shape[0]
    tm = min(ROW_TILE, t)
    assert t % tm == 0 and wg.shape[1] % FFN_CHUNK == 0
    row = pl.BlockSpec((tm, D_MODEL), lambda i: (i, 0))
    return pl.pallas_call(
        _ffn_kernel, grid=(t // tm,),
        in_specs=[row, row, _resident(wg.shape), _resident(wu.shape), _resident(wd.shape)],
        out_specs=row,
        out_shape=jax.ShapeDtypeStruct((t, D_MODEL), F32),
        scratch_shapes=[pltpu.VMEM((tm, D_MODEL), F32)],
        compiler_params=_params("parallel"), name="ffn_dense",
    )(h2, z, wg, wu, wd)


def _token_copy(src, src_row, dst, dst_row, slabs, sem):
    return pltpu.make_async_copy(src.at[pl.ds(src_row, slabs), :], dst.at[pl.ds(dst_row, slabs), :],
                                 sem)


def _block_copy(src, dst, slabs, sem):
    n = EXPERT_ROWS * slabs
    return pltpu.make_async_copy(src.at[pl.ds(0, n), :], dst.at[pl.ds(0, n), :], sem)


def _experts_kernel(n_f, be_ref, nact_ref, tok_cur_ref, tok_next_ref, dst_prev_ref, dst_cur_ref,
                    zp_hbm, wg_ref, wu_ref, wd_ref, y_hbm, xp_ref, x_ref, acc_ref, stage_ref,
                    sem_in, sem_out):
    del be_ref
    i = pl.program_id(0)
    f = pl.program_id(1)
    n_active = nact_ref[0]
    active = i < n_active
    has_next = i + 1 < n_active
    has_prev = i >= 1
    first_step = f == 0
    last_step = f == n_f - 1
    half = D_MODEL // 2

    def gather(idx_ref, r, priority=0):
        src = pl.multiple_of(idx_ref[0, 0, r], PACKED_SLABS)
        _token_copy(zp_hbm, src, xp_ref, r * PACKED_SLABS, PACKED_SLABS, sem_in).start(
            priority=priority)

    def scatter(idx_ref, r, priority=0):
        dst = pl.multiple_of(idx_ref[0, 0, r], OUT_SLABS)
        _token_copy(stage_ref, r * OUT_SLABS, y_hbm, dst, OUT_SLABS, sem_out).start(
            priority=priority)

    def gather_next(r, parity):
        del parity
        gather(tok_next_ref, r)

    def scatter_prev(r, parity):
        scatter(dst_prev_ref, r, priority=parity)

    def rolled(fn):
        def body(r, _):
            fn(r)
            return 0
        lax.fori_loop(0, EXPERT_ROWS, body, 0)

    def unpack(r0):
        lo, hi = [], []
        for j in range(PACKED_SLABS):
            xp = xp_ref[pl.ds(r0 * PACKED_SLABS + j, EXPERT_SUB, stride=PACKED_SLABS), :]
            lo.append(pltpu.bitcast(xp << 16, F32).astype(BF16))
            hi.append(pltpu.bitcast(xp & jnp.uint32(0xFFFF0000), F32).astype(BF16))
        return jnp.concatenate(lo + hi, axis=1)

    def compute(phase, inline_dma=None):
        def sub_block(s, _):
            r0 = pl.multiple_of(s * EXPERT_SUB, EXPERT_SUB)
            rows = pl.ds(r0, EXPERT_SUB)
            if phase == "first":
                x = unpack(r0)
                x_ref[rows, :] = x
            else:
                x = x_ref[rows, :]
            g = jnp.dot(x, wg_ref[0, 0], preferred_element_type=F32)
            u = jnp.dot(x, wu_ref[0, 0], preferred_element_type=F32)
            a = (_silu(g) * u).astype(BF16)
            c = jnp.dot(a, wd_ref[0, 0], preferred_element_type=F32)
            if phase == "first":
                acc_ref[rows, :] = c
            elif phase == "mid":
                acc_ref[rows, :] += c
            else:
                out = acc_ref[rows, :] + c
                for j in range(OUT_SLABS):
                    stage_ref[pl.ds(r0 * OUT_SLABS + j, EXPERT_SUB, stride=OUT_SLABS), :] = (
                        out[:, j * LANES:(j + 1) * LANES])
            if inline_dma is not None:
                for k in range(EXPERT_SUB):
                    inline_dma(r0 + k, k % 2)
            return 0
        lax.fori_loop(0, EXPERT_ROWS // EXPERT_SUB, sub_block, 0)

    @pl.when((i == 0) & first_step)
    def _():
        stage_ref[...] = jnp.zeros(stage_ref.shape, F32)
        n_sink = EXPERT_ROWS * OUT_SLABS
        sink = pltpu.make_async_copy(
            stage_ref, y_hbm.at[pl.ds(y_hbm.shape[0] - n_sink, n_sink), :], sem_out)
        sink.start()
        sink.wait()
        rolled(functools.partial(gather, tok_cur_ref))

    @pl.when(active & first_step)
    def _():
        _block_copy(zp_hbm, xp_ref, PACKED_SLABS, sem_in).wait()

    @pl.when(active & first_step & has_prev)
    def _():
        compute("first", scatter_prev)

    @pl.when(active & first_step & jnp.logical_not(has_prev))
    def _():
        compute("first")

    if n_f > 2:
        @pl.when(active & jnp.logical_not(first_step) & jnp.logical_not(last_step))
        def _():
            compute("mid")

    @pl.when(active & last_step & has_prev)
    def _():
        _block_copy(stage_ref, y_hbm, OUT_SLABS, sem_out).wait()

    @pl.when(active & last_step & has_next)
    def _():
        compute("last", gather_next)

    @pl.when(active & last_step & jnp.logical_not(has_next))
    def _():
        compute("last")
        rolled(functools.partial(scatter, dst_cur_ref))
        _block_copy(stage_ref, y_hbm, OUT_SLABS, sem_out).wait()


def _experts(zp, plan, layer, wg, wu, wd):
    t = zp.shape[0] // PACKED_SLABS
    blk_expert, n_active, tok_buf, dst_buf = plan
    nb = blk_expert.shape[0]
    d_ff = wg.shape[3]
    tf = EXPERT_FF_TILE
    n_f = d_ff // tf
    assert d_ff % tf == 0 and n_f >= 2
    last = nb - 1

    def idx(shift):
        return pl.BlockSpec((1, 1, EXPERT_ROWS),
                            lambda i, f, *_: (jnp.clip(i + shift, 0, last), 0, 0),
                            memory_space=pltpu.SMEM)

    grid_spec = pltpu.PrefetchScalarGridSpec(
        num_scalar_prefetch=2,
        grid=(nb, n_f),
        in_specs=[idx(0), idx(1), idx(-1), idx(0), pl.BlockSpec(memory_space=pl.ANY),
                  pl.BlockSpec((1, 1, D_MODEL, tf), lambda i, f, be, *_: (layer, be[i], 0, f)),
                  pl.BlockSpec((1, 1, D_MODEL, tf), lambda i, f, be, *_: (layer, be[i], 0, f)),
                  pl.BlockSpec((1, 1, tf, D_MODEL), lambda i, f, be, *_: (layer, be[i], f, 0))],
        out_specs=pl.BlockSpec(memory_space=pl.ANY),
        scratch_shapes=[pltpu.VMEM((EXPERT_ROWS * PACKED_SLABS, LANES), jnp.uint32),
                        pltpu.VMEM((EXPERT_ROWS, D_MODEL), BF16),
                        pltpu.VMEM((EXPERT_ROWS, D_MODEL), F32),
                        pltpu.VMEM((EXPERT_ROWS * OUT_SLABS, LANES), F32),
                        pltpu.SemaphoreType.DMA(()), pltpu.SemaphoreType.DMA(())],
    )
    return pl.pallas_call(
        functools.partial(_experts_kernel, n_f), grid_spec=grid_spec,
        out_shape=jax.ShapeDtypeStruct(((TOP_K * t + EXPERT_ROWS) * OUT_SLABS, LANES), F32),
        compiler_params=_params("arbitrary", "arbitrary"), name="moe_experts",
    )(blk_expert, n_active, tok_buf, tok_buf, dst_buf, dst_buf, zp, wg, wu, wd)


def _route_plan(route, t):
    a = t * TOP_K
    nb = a // EXPERT_ROWS + N_EXPERTS
    rows = nb * EXPERT_ROWS
    flat_e = route[:, 2:4].astype(jnp.int32).reshape(a)
    experts = jnp.arange(N_EXPERTS, dtype=jnp.int32)
    sizes = jnp.sum((flat_e[:, None] == experts[None, :]).astype(jnp.int32), axis=0)
    start = jnp.cumsum(sizes) - sizes
    padded = ((sizes + EXPERT_ROWS - 1) // EXPERT_ROWS) * EXPERT_ROWS
    pad_end = jnp.cumsum(padded)
    pad_start = pad_end - padded
    sorted_assign = jnp.sort(flat_e * a + jnp.arange(a, dtype=jnp.int32)) % a
    blk_start = jnp.arange(nb, dtype=jnp.int32) * EXPERT_ROWS
    blk_expert = jnp.minimum(
        jnp.sum((pad_end[None, :] <= blk_start[:, None]).astype(jnp.int32), axis=1),
        N_EXPERTS - 1)
    row = jnp.arange(rows, dtype=jnp.int32)
    row_e = jnp.repeat(blk_expert, EXPERT_ROWS)
    rank = row - pad_start[row_e]
    real = (rank >= 0) & (rank < sizes[row_e])
    assign = sorted_assign[jnp.clip(start[row_e] + rank, 0, a - 1)]
    tok = assign // TOP_K
    tok_buf = jnp.where(real, tok, 0)
    dst_buf = jnp.where(real, tok + (assign % TOP_K) * t, TOP_K * t + row % EXPERT_ROWS)
    n_active = (pad_end[-1] // EXPERT_ROWS).reshape(1).astype(jnp.int32)
    return (blk_expert, n_active, (tok_buf * PACKED_SLABS).reshape(nb, 1, EXPERT_ROWS),
            (dst_buf * OUT_SLABS).reshape(nb, 1, EXPERT_ROWS))


def _combine_kernel(h_ref, y0_ref, y1_ref, route_ref, o_ref):
    o_ref[...] = _moe_combined(h_ref, y0_ref, y1_ref, route_ref)


def _combine(h2, y, route):
    t = h2.shape[0]
    tm = min(ROW_TILE, t)
    nt = t // tm
    row = pl.BlockSpec((tm, D_MODEL), lambda i: (i, 0))
    return pl.pallas_call(
        _combine_kernel, grid=(nt,),
        in_specs=[row, pl.BlockSpec((tm * OUT_SLABS, LANES), lambda i: (i, 0)),
                  pl.BlockSpec((tm * OUT_SLABS, LANES), lambda i: (i + nt, 0)),
                  pl.BlockSpec((tm, LANES), lambda i: (i, 0))],
        out_specs=row,
        out_shape=jax.ShapeDtypeStruct((t, D_MODEL), F32),
        compiler_params=_params("parallel"), name="moe_combine",
    )(h2, y, y, route)


def _row(v):
    return v.reshape(1, -1).astype(F32)


def _tile_heads(g, n, scale=1.0):
    return _row(jnp.tile(g.astype(F32) * scale, n))


def kernel(x, mem, norm_mix, norm_mem, norm_ffn, w_mem_kv, g_mq, g_mk, fox_w_in, fox_b_f, fox_g_q, fox_g_k, fox_w_out, conv_w_in, conv_b_in, conv_dw, conv_dw_b, conv_ln_g, conv_ln_b, conv_w_out, ffn_w_gate, ffn_w_up, ffn_w_down, moe_router, moe_w_gate, moe_w_up, moe_w_down):
    b, s, d = x.shape
    mem_len = mem.shape[1]
    t = b * s
    assert d == D_MODEL
    scale = HEAD_DIM ** -0.5

    group = jnp.arange(MXU_WIDTH, dtype=jnp.int32) // HEAD_DIM
    ones_bd = (group[:, None] == group[None, :]).astype(BF16)

    moe_wg, moe_wu, moe_wd = (w.astype(BF16) for w in (moe_w_gate, moe_w_up, moe_w_down))
    h = x.reshape(t, d)
    pending = None
    mem2d = mem.reshape(b * mem_len, d)
    for i in range(DEPTH):
        j = i // 2
        gmq = _tile_heads(g_mq[i], MEM_DIM // HEAD_DIM, scale)
        mk, mv = _mem_kv(mem2d, mem_len, _row(norm_mem[i]), w_mem_kv[i].astype(BF16),
                         ones_bd, _tile_heads(g_mk[i], MEM_DIM // HEAD_DIM))
        if i % 2 == 0:
            w = fox_w_in[j]
            m3 = 3 * MIX_DIM
            w_cat = jnp.concatenate(
                [w[:, :m3], w[:, m3 + FOX_HEADS:], w[:, m3:m3 + FOX_HEADS],
                 jnp.zeros((d, F_LANES - FOX_HEADS), w.dtype)], axis=1).astype(BF16)
            bf = jnp.pad(fox_b_f[j].astype(F32), (0, F_LANES - FOX_HEADS)).reshape(1, F_LANES)
            h, q, k, v, mq, kbias = _fox_inproj(
                h, s, _row(norm_mix[i]), w_cat, ones_bd,
                _tile_heads(fox_g_q[j], FOX_HEADS, scale * LOG2E),
                _tile_heads(fox_g_k[j], FOX_HEADS), gmq, bf, moe=pending)
            pending = None
            seq = lambda a: a.reshape(b, s, MIX_DIM)
            mix = _fox_attn(seq(q), seq(k), seq(kbias), seq(v)).reshape(t, MIX_DIM)
            w_out = fox_w_out[j]
        else:
            u, mq = _conv_inproj(h, s, _row(norm_mix[i]), conv_w_in[j].astype(BF16),
                                 _row(conv_b_in[j]), ones_bd, gmq)
            dw8 = jnp.broadcast_to(conv_dw[j].astype(F32)[:, None, :],
                                   (CONV_WIDTH, SUBLANES, MIX_DIM))
            mix = _conv_module(u.reshape(b, s, MIX_DIM), dw8,
                               _row(conv_dw_b[j]), _row(conv_ln_g[j]),
                               _row(conv_ln_b[j])).reshape(t, MIX_DIM)
            w_out = conv_w_out[j]
        mem_out = _mem_attn(mq, mk, mv, s, mem_len)
        if i % 2 == 0:
            h2, z = _outproj(h, mix, mem_out, w_out.astype(BF16), _row(norm_ffn[i]))
            h = _ffn(h2, z, ffn_w_gate[j].astype(BF16), ffn_w_up[j].astype(BF16),
                     ffn_w_down[j].astype(BF16))
        else:
            wr = jnp.pad(moe_router[j].astype(F32), ((0, 0), (0, LANES - N_EXPERTS)))
            h2, zp, route = _outproj(h, mix, mem_out, w_out.astype(BF16), _row(norm_ffn[i]),
                                     router=wr)
            plan = _route_plan(route, t)
            y = _experts(zp, plan, j, moe_wg, moe_wu, moe_wd)
            if i + 1 < DEPTH and (i + 1) % 2 == 0:
                h, pending = h2, (y, route)
            else:
                h = _combine(h2, y, route)
    return h.reshape(b, s, d)
```
